```python
import math
import numpy as np
import jax
import jax.numpy as jnp
from jax import lax

D_MODEL = 1024
BATCH = 2
SEQ = 16384
DEPTH = 2

GRID_W = 64
CTX_LEN = 256
BLK = 128
ROPE_BASE = 10000.0
NORM_EPS = 1e-6

A_HEADS = 8
A_KV_HEADS = 2
A_HD = 64
A_WINDOW = 128
B_HEADS = 4
B_HD = 128
B_CHUNK = 128
B_FORGET_BIAS = 3.0
C_HEADS = 4
C_Q_LORA = 512
C_KV_LORA = 256
C_NOPE = 128
C_ROPE = 64
C_VD = 128
D_HEADS = 4
D_HD = 64
D_VD = 2 * D_HD

N_BRANCH = 4
BRANCH_W = 512
FFN_HIDDEN = ((8 * D_MODEL + 3 * 256 - 1) // (3 * 256)) * 256

IN_SPLITS = (A_HEADS * A_HD, A_KV_HEADS * A_HD, A_KV_HEADS * A_HD,
             B_HEADS * B_HD, B_HEADS * B_HD, B_HEADS * B_HD, 4 * B_HEADS, B_HEADS * B_HD,
             C_Q_LORA, C_KV_LORA, C_ROPE,
             D_HEADS * 2 * D_HD, D_HEADS * 2 * D_HD, D_HEADS * D_VD,
             N_BRANCH * D_MODEL)
IN_OFFSETS = tuple(int(o) for o in np.cumsum(IN_SPLITS)[:-1])
D_IN = sum(IN_SPLITS)

kernel_name = 'hybrid_gated_parallel_mixer_dit'


def rmsnorm(x, g):
    xf = x.astype(jnp.float32)
    y = xf * lax.rsqrt(jnp.mean(xf * xf, axis=-1, keepdims=True) + NORM_EPS)
    return (y * g.astype(jnp.float32)).astype(x.dtype)


def modulate(x, shift, scale):
    return x * (1 + scale) + shift


def swiglu(u, w_gate_up, w_down):
    gate, up = jnp.split(u @ w_gate_up, 2, axis=-1)
    return (jax.nn.silu(gate) * up) @ w_down


def axial_rope_tables(n_rows, dim, dtype):
    row = jnp.repeat(jnp.arange(n_rows, dtype=jnp.float32), GRID_W)
    col = jnp.tile(jnp.arange(GRID_W, dtype=jnp.float32), n_rows)
    quarter = dim // 4
    inv_freq = ROPE_BASE ** (-jnp.arange(quarter, dtype=jnp.float32) / quarter)
    ang_r = row[:, None] * inv_freq
    ang_c = col[:, None] * inv_freq
    ang = jnp.concatenate([ang_r, ang_r, ang_c, ang_c], axis=-1)
    return jnp.cos(ang).astype(dtype), jnp.sin(ang).astype(dtype)


def rotate_axial(x):
    xs = x.reshape(x.shape[:-1] + (2, 2, x.shape[-1] // 4))
    return jnp.concatenate([-xs[..., 1:, :], xs[..., :1, :]], axis=-2).reshape(x.shape)


def apply_rope(x, cos, sin):
    shape = (cos.shape[0],) + (1,) * (x.ndim - 3) + (cos.shape[1],)
    return x * cos.reshape(shape) + rotate_axial(x) * sin.reshape(shape)


def joint_softmax(scores, sink=None):
    m = scores[0].max(axis=-1, keepdims=True)
    for s in scores[1:]:
        m = jnp.maximum(m, s.max(axis=-1, keepdims=True))
    if sink is not None:
        m = jnp.maximum(m, sink)
    e = [jnp.exp(s - m) for s in scores]
    den = sum(ei.sum(axis=-1, keepdims=True) for ei in e)
    if sink is not None:
        den = den + jnp.exp(sink - m)
    return [ei / den for ei in e]


def sweep_blocks(fn, *qs):
    b, n_tok = qs[0].shape[:2]
    nb = n_tok // BLK
    blocks = tuple(a.reshape((b, nb, BLK) + a.shape[2:]).swapaxes(0, 1) for a in qs)
    out = lax.map(lambda xs: fn(xs[0], *xs[1:]), (jnp.arange(nb),) + blocks)
    return out.swapaxes(0, 1).reshape((b, n_tok) + out.shape[3:])


def window_gqa(q, k, v, q_c, k_c, v_c, sink, with_ctx):
    n_tok = q.shape[1]
    group = A_HEADS // A_KV_HEADS
    scale = A_HD ** -0.5
    sink_f = sink.astype(jnp.float32).reshape(A_KV_HEADS, group)[None, :, :, None, None]
    span = BLK + 2 * A_WINDOW
    pad = ((0, 0), (A_WINDOW, A_WINDOW), (0, 0), (0, 0))
    k_pad, v_pad = jnp.pad(k, pad), jnp.pad(v, pad)

    def score(qg, kk):
        return jnp.einsum('bqhgd,bkhd->bhgqk', qg, kk, preferred_element_type=jnp.float32) * scale

    def values(p, vv):
        return jnp.einsum('bhgqk,bkhd->bqhgd', p.astype(vv.dtype), vv)

    def block(n, qg):
        start = n * BLK
        k_win = lax.dynamic_slice_in_dim(k_pad, start, span, axis=1)
        v_win = lax.dynamic_slice_in_dim(v_pad, start, span, axis=1)
        pos_q = start + jnp.arange(BLK)
        pos_k = start - A_WINDOW + jnp.arange(span)
        valid = ((jnp.abs(pos_k[None, :] - pos_q[:, None]) <= A_WINDOW)
                 & (pos_k[None, :] >= 0) & (pos_k[None, :] < n_tok))
        p_win, p_ctx = joint_softmax([jnp.where(valid, score(qg, k_win), -jnp.inf), score(qg, k_c)], sink_f)
        return values(p_win, v_win) + values(p_ctx, v_c)

    def grouped(a):
        return a.reshape(a.shape[:2] + (A_KV_HEADS, group, A_HD))

    flat = lambda o: o.reshape(o.shape[:2] + (-1,))
    y = flat(sweep_blocks(block, grouped(q)))
    y_c = None
    if with_ctx:
        qg_c = grouped(q_c)
        p_c = joint_softmax([score(qg_c, k_c)], sink_f)[0]
        y_c = flat(values(p_c, v_c))
    return y, y_c


def mlstm_chunkwise(q, k, v, ig, lf, state):
    b, n_tok = q.shape[:2]
    nc = n_tok // B_CHUNK
    tril = jnp.tril(jnp.ones((B_CHUNK, B_CHUNK), dtype=bool))

    def chunks(a):
        return a.reshape((b, nc, B_CHUNK) + a.shape[2:]).swapaxes(0, 1)

    def step(carry, xs):
        c_prev, n_prev, m_prev = carry
        qc, kc, vc, ic, fc = xs
        bcum = jnp.cumsum(fc, axis=1).transpose(0, 2, 1)
        it = ic.transpose(0, 2, 1)
        dmat = jnp.where(tril, bcum[..., :, None] - bcum[..., None, :] + it[..., None, :], -jnp.inf)
        m_t = jnp.maximum(bcum + m_prev[..., None], dmat.max(axis=-1))
        w = jnp.exp(dmat - m_t[..., None])
        wqk = w * jnp.einsum('bthd,bshd->bhts', qc, kc)
        a = jnp.exp(bcum + m_prev[..., None] - m_t)
        num = (jnp.einsum('bhts,bshv->bhtv', wqk, vc)
               + a[..., None] * jnp.einsum('bhvd,bthd->bhtv', c_prev, qc))
        den = wqk.sum(axis=-1) + a * jnp.einsum('bhd,bthd->bht', n_prev, qc)
        h = num / jnp.maximum(jnp.abs(den), jnp.exp(-m_t))[..., None]
        b_last = bcum[..., -1]
        g = b_last[..., None] - bcum + it
        m_new = jnp.maximum(b_last + m_prev, g.max(axis=-1))
        decay = jnp.exp(b_last + m_prev - m_new)
        ws = jnp.exp(g - m_new[..., None])
        c_new = decay[..., None, None] * c_prev + jnp.einsum('bhs,bshv,bshd->bhvd', ws, vc, kc)
        n_new = decay[..., None] * n_prev + jnp.einsum('bhs,bshd->bhd', ws, kc)
        return (c_new, n_new, m_new), h.transpose(0, 2, 1, 3)

    state, hs = lax.scan(step, state, tuple(chunks(a) for a in (q, k, v, ig, lf)))
    return hs.swapaxes(0, 1).reshape((b, n_tok) + hs.shape[3:]), state


def mlstm_bidir(q, k, v, gate_pre, o_pre, q_c, k_c, v_c, gate_pre_c, o_pre_c, gate_bias, with_ctx):
    f32 = jnp.float32

    def prep(qq, kk, vv, gp):
        g = (gp.astype(f32) + gate_bias.astype(f32)).reshape(gp.shape[:2] + (4, B_HEADS))
        qkv = (qq.astype(f32), kk.astype(f32) * B_HD ** -0.5, vv.astype(f32))
        fwd = (g[:, :, 0], jax.nn.log_sigmoid(g[:, :, 1]))
        bwd = (g[:, :, 2], jax.nn.log_sigmoid(g[:, :, 3]))
        return qkv, fwd, bwd

    def run(qkv, gates, state, reverse):
        arrs = qkv + gates
        if reverse:
            arrs = tuple(jnp.flip(a, axis=1) for a in arrs)
        h, st = mlstm_chunkwise(*arrs, state)
        return (jnp.flip(h, axis=1) if reverse else h), st

    b = q.shape[0]
    zero = (jnp.zeros((b, B_HEADS, B_HD, B_HD), f32), jnp.zeros((b, B_HEADS, B_HD), f32),
            jnp.zeros((b, B_HEADS), f32))
    qkv_c, fwd_c, bwd_c = prep(q_c, k_c, v_c, gate_pre_c)
    h_cf, st_f = run(qkv_c, fwd_c, zero, False)
    h_cb, st_b = run(qkv_c, bwd_c, zero, True)
    qkv, fwd, bwd = prep(q, k, v, gate_pre)
    h_f, _ = run(qkv, fwd, st_f, False)
    h_b, _ = run(qkv, bwd, st_b, True)

    def out(h, o):
        return jax.nn.sigmoid(o) * h.reshape(h.shape[:2] + (-1,)).astype(o.dtype)

    y = out(h_f + h_b, o_pre)
    y_c = out(h_cf + h_cb, o_pre_c) if with_ctx else None
    return y, y_c


def mla_attn(q_nope, q_rope, k_nope, k_rope, v, q_nope_c, q_rope_c, k_nope_c, k_rope_c, v_c, with_ctx):
    scale = (C_NOPE + C_ROPE) ** -0.5

    def score(qn, qr, kn, kr):
        return (jnp.einsum('bqhd,bkhd->bhqk', qn, kn, preferred_element_type=jnp.float32)
                + jnp.einsum('bqhr,bkr->bhqk', qr, kr, preferred_element_type=jnp.float32)) * scale

    def attend(qn, qr, key_sets):
        p = joint_softmax([score(qn, qr, kn, kr) for kn, kr, _ in key_sets])
        return sum(jnp.einsum('bhqk,bkhv->bqhv', pi.astype(vv.dtype), vv) for pi, (_, _, vv) in zip(p, key_sets))

    ctx_set = (k_nope_c, k_rope_c, v_c)
    lat_sets = [(k_nope, k_rope, v), ctx_set]
    flat = lambda o: o.reshape(o.shape[:2] + (-1,))
    y = flat(sweep_blocks(lambda n, qn, qr: attend(qn, qr, lat_sets), q_nope, q_rope))
    y_c = flat(attend(q_nope_c, q_rope_c, [ctx_set])) if with_ctx else None
    return y, y_c


def diff_attn(q, k, v, q_c, k_c, v_c, lam, lam_init, g_sub, with_ctx):
    scale = D_HD ** -0.5

    def score(qm, km):
        return jnp.einsum('bqhd,bkhd->bhqk', qm, km, preferred_element_type=jnp.float32) * scale

    def attend(q1, q2, key_sets):
        p1 = joint_softmax([score(q1, k1) for k1, _, _ in key_sets])
        p2 = joint_softmax([score(q2, k2) for _, k2, _ in key_sets])
        return sum(jnp.einsum('bhqk,bkhv->bqhv', (a1 - lam * a2).astype(vv.dtype), vv)
                   for a1, a2, (_, _, vv) in zip(p1, p2, key_sets))

    def post(o):
        return (rmsnorm(o, g_sub) * (1 - lam_init)).reshape(o.shape[:2] + (-1,))

    ctx_set = (k_c[:, :, :, 0], k_c[:, :, :, 1], v_c)
    lat_sets = [(k[:, :, :, 0], k[:, :, :, 1], v), ctx_set]
    y = post(sweep_blocks(lambda n, q1, q2: attend(q1, q2, lat_sets), q[:, :, :, 0], q[:, :, :, 1]))
    y_c = post(attend(q_c[:, :, :, 0], q_c[:, :, :, 1], [ctx_set])) if with_ctx else None
    return y, y_c


def mixer_inputs(parts, ropes, c_g_q, c_g_kv, c_w_uq, c_w_uk, c_w_uv):
    (pa_q, pa_k, pa_v, pb_q, pb_k, pb_v, pb_g, pb_o, pc_q, pc_kv, pc_kr, pd_q, pd_k, pd_v, p_gate) = parts

    def heads(a, *shape):
        return a.reshape(a.shape[:2] + shape)

    def rope(a, dim):
        return a if ropes is None else apply_rope(a, *ropes[dim])

    cq = heads(rmsnorm(pc_q, c_g_q) @ c_w_uq, C_HEADS, C_NOPE + C_ROPE)
    ckv = rmsnorm(pc_kv, c_g_kv)
    return {
        'a_q': rope(heads(pa_q, A_HEADS, A_HD), A_HD),
        'a_k': rope(heads(pa_k, A_KV_HEADS, A_HD), A_HD),
        'a_v': heads(pa_v, A_KV_HEADS, A_HD),
        'b_q': heads(pb_q, B_HEADS, B_HD),
        'b_k': heads(pb_k, B_HEADS, B_HD),
        'b_v': heads(pb_v, B_HEADS, B_HD),
        'b_gates': pb_g,
        'b_o': pb_o,
        'c_qn': cq[..., :C_NOPE],
        'c_qr': rope(cq[..., C_NOPE:], C_ROPE),
        'c_kn': heads(ckv @ c_w_uk, C_HEADS, C_NOPE),
        'c_kr': rope(pc_kr, C_ROPE),
        'c_v': heads(ckv @ c_w_uv, C_HEADS, C_VD),
        'd_q': rope(heads(pd_q, D_HEADS, 2, D_HD), D_HD),
        'd_k': rope(heads(pd_k, D_HEADS, 2, D_HD), D_HD),
        'd_v': heads(pd_v, D_HEADS, D_VD),
        'gates': p_gate,
    }


def merge_branches(ys, gate_logits, w_branch, w_out):
    g = jax.nn.sigmoid(gate_logits.reshape(gate_logits.shape[:2] + (N_BRANCH, D_MODEL)))
    merged = sum(g[:, :, i] * (y @ w_branch[i]) for i, y in enumerate(ys))
    return merged @ w_out


def token_mixers(u, uc, w_in, a_sink, b_gate_bias, c_g_q, c_g_kv, c_w_uq, c_w_uk, c_w_uv, d_g_sub,
                 w_branch, w_out, lam, lam_init, ropes, with_ctx):
    mla_w = (c_g_q, c_g_kv, c_w_uq, c_w_uk, c_w_uv)
    t = mixer_inputs(jnp.split(u @ w_in, IN_OFFSETS, axis=-1), ropes, *mla_w)
    s = mixer_inputs(jnp.split(uc @ w_in, IN_OFFSETS, axis=-1), None, *mla_w)
    y_a, y_a_c = window_gqa(t['a_q'], t['a_k'], t['a_v'], s['a_q'], s['a_k'], s['a_v'], a_sink, with_ctx)
    y_b, y_b_c = mlstm_bidir(t['b_q'], t['b_k'], t['b_v'], t['b_gates'], t['b_o'],
                             s['b_q'], s['b_k'], s['b_v'], s['b_gates'], s['b_o'], b_gate_bias, with_ctx)
    y_c, y_c_c = mla_attn(t['c_qn'], t['c_qr'], t['c_kn'], t['c_kr'], t['c_v'],
                          s['c_qn'], s['c_qr'], s['c_kn'], s['c_kr'], s['c_v'], with_ctx)
    y_d, y_d_c = diff_attn(t['d_q'], t['d_k'], t['d_v'], s['d_q'], s['d_k'], s['d_v'],
                           lam, lam_init, d_g_sub, with_ctx)
    y = merge_branches((y_a, y_b, y_c, y_d), t['gates'], w_branch, w_out)
    y_ctx = merge_branches((y_a_c, y_b_c, y_c_c, y_d_c), s['gates'], w_branch, w_out) if with_ctx else None
    return y, y_ctx


def setup_inputs(seed: int = 0) -> dict:
    key = jax.random.key(seed)
    ks = iter(jax.random.split(key, 32))
    f32 = jnp.float32

    def nrm(shape, scale):
        return jax.random.normal(next(ks), shape, f32) * scale

    def gain(shape):
        return 1.0 + nrm(shape, 0.05)

    L = DEPTH
    forget_offset = jnp.repeat(jnp.array([0.0, B_FORGET_BIAS, 0.0, B_FORGET_BIAS], f32), B_HEADS)
    return {
        'x': nrm((BATCH, SEQ, D_MODEL), 1.0),
        'c': nrm((BATCH, D_MODEL), 1.0),
        'ctx': nrm((BATCH, CTX_LEN, D_MODEL), 1.0),
        'c_ctx': nrm((D_MODEL,), 1.0),
        'w_mod': nrm((L, D_MODEL, 6 * D_MODEL), 0.5 * D_MODEL ** -0.5),
        'b_mod': nrm((L, 6 * D_MODEL), 0.01),
        'g_mix_pre': gain((L, D_MODEL)),
        'g_mix_post': gain((L, D_MODEL)),
        'g_ffn_pre': gain((L, D_MODEL)),
        'g_ffn_post': gain((L, D_MODEL)),
        'w_in': nrm((L, D_MODEL, D_IN), D_MODEL ** -0.5),
        'a_sink': nrm((L, A_HEADS), 1.0),
        'b_gate_bias': nrm((L, 4 * B_HEADS), 0.1) + forget_offset,
        'c_g_q': gain((L, C_Q_LORA)),
        'c_g_kv': gain((L, C_KV_LORA)),
        'c_w_uq': nrm((L, C_Q_LORA, C_HEADS * (C_NOPE + C_ROPE)), C_Q_LORA ** -0.5),
        'c_w_uk': nrm((L, C_KV_LORA, C_HEADS * C_NOPE), C_KV_LORA ** -0.5),
        'c_w_uv': nrm((L, C_KV_LORA, C_HEADS * C_VD), C_KV_LORA ** -0.5),
        'd_lam_q1': nrm((L, D_HD), 0.1),
        'd_lam_k1': nrm((L, D_HD), 0.1),
        'd_lam_q2': nrm((L, D_HD), 0.1),
        'd_lam_k2': nrm((L, D_HD), 0.1),
        'd_g_sub': gain((L, D_VD)),
        'w_branch': nrm((L, N_BRANCH, BRANCH_W, D_MODEL), BRANCH_W ** -0.5),
        'w_out': nrm((L, D_MODEL, D_MODEL), D_MODEL ** -0.5),
        'w_gate_up': nrm((L, D_MODEL, 2 * FFN_HIDDEN), D_MODEL ** -0.5),
        'w_down': nrm((L, FFN_HIDDEN, D_MODEL), FFN_HIDDEN ** -0.5),
    }


def reference(x, c, ctx, c_ctx, w_mod, b_mod, g_mix_pre, g_mix_post, g_ffn_pre, g_ffn_post, w_in, a_sink,
              b_gate_bias, c_g_q, c_g_kv, c_w_uq, c_w_uk, c_w_uv, d_lam_q1, d_lam_k1, d_lam_q2, d_lam_k2,
              d_g_sub, w_branch, w_out, w_gate_up, w_down):
    f32 = jnp.float32
    n_rows = x.shape[1] // GRID_W
    ropes = {dim: axial_rope_tables(n_rows, dim, x.dtype) for dim in sorted({A_HD, C_ROPE, D_HD})}
    silu_c = jax.nn.silu(c)
    silu_cc = jax.nn.silu(c_ctx)
    h, hc = x, ctx
    for l in range(DEPTH):
        with_ctx = l < DEPTH - 1
        mod = jnp.split((silu_c @ w_mod[l] + b_mod[l])[:, None, :], 6, axis=-1)
        modc = jnp.split(silu_cc @ w_mod[l] + b_mod[l], 6, axis=-1)
        lam_init = 0.8 - 0.6 * math.exp(-0.3 * l)
        lam = (jnp.exp(jnp.sum(d_lam_q1[l].astype(f32) * d_lam_k1[l].astype(f32)))
               - jnp.exp(jnp.sum(d_lam_q2[l].astype(f32) * d_lam_k2[l].astype(f32))) + lam_init)
        u = modulate(rmsnorm(h, g_mix_pre[l]), mod[0], mod[1])
        uc = modulate(rmsnorm(hc, g_mix_pre[l]), modc[0], modc[1])
        y, y_ctx = token_mixers(u, uc, w_in[l], a_sink[l], b_gate_bias[l], c_g_q[l], c_g_kv[l], c_w_uq[l],
                                c_w_uk[l], c_w_uv[l], d_g_sub[l], w_branch[l], w_out[l], lam, lam_init,
                                ropes, with_ctx)
        h = h + mod[2] * rmsnorm(y, g_mix_post[l])
        u = modulate(rmsnorm(h, g_ffn_pre[l]), mod[3], mod[4])
        h = h + mod[5] * rmsnorm(swiglu(u, w_gate_up[l], w_down[l]), g_ffn_post[l])
        if with_ctx:
            hc = hc + modc[2] * rmsnorm(y_ctx, g_mix_post[l])
            uc = modulate(rmsnorm(hc, g_ffn_pre[l]), modc[3], modc[4])
            hc = hc + modc[5] * rmsnorm(swiglu(uc, w_gate_up[l], w_down[l]), g_ffn_post[l])
    return h
```

```python
import functools
import math

import numpy as np
import jax
import jax.numpy as jnp
from jax import lax
from jax.experimental import pallas as pl
from jax.experimental.pallas import tpu as pltpu

F32 = jnp.float32
BF16 = jnp.bfloat16

GRID_W = 64
ROPE_BASE = 10000.0
NORM_EPS = 1e-6
A_HEADS, A_KV_HEADS, A_HD, A_WINDOW = 8, 2, 64, 128
B_HEADS, B_HD, B_CHUNK = 4, 128, 128
C_HEADS, C_Q_LORA, C_KV_LORA, C_NOPE, C_ROPE, C_VD = 4, 512, 256, 128, 64, 128
D_HEADS, D_HD, D_VD = 4, 64, 128
N_BRANCH = 4
BRANCH_W = 512
ROPE_DIM = 64
C_QK_PAD = 256

LANES = 128
NEG_BIG = -1e30
VMEM_LIMIT = 56 * 1024 * 1024

TOKEN_TILE = 256
WINDOW_TILE = 256
FLASH_TQ = 1280
FLASH_TK = 1280
FLASH_SUB = 256


def _resident(shape):
    nd = len(shape)
    return pl.BlockSpec(shape, lambda *_: (0,) * nd, pipeline_mode=pl.Buffered(1))


def _sigmoid(x):
    return 1.0 / (1.0 + jnp.exp(-x))


def _log_sigmoid(x):
    return jnp.minimum(x, 0.0) - jnp.log(1.0 + jnp.exp(-jnp.abs(x)))


def _rms(x, g):
    return x * lax.rsqrt(jnp.mean(x * x, axis=-1, keepdims=True) + NORM_EPS) * g


def _rope(x, cos, sin_lo, sin_hi):
    return x * cos + pltpu.roll(x, LANES - 16, 1) * sin_lo + pltpu.roll(x, 16, 1) * sin_hi


def _dot_nt(a, b):
    return lax.dot_general(a, b, (((1,), (1,)), ((), ())), preferred_element_type=F32)


def _dot_tn(a, b):
    return lax.dot_general(a, b, (((0,), (0,)), ((), ())), preferred_element_type=F32)


def _mod_kernel(c_ref, w_ref, b_ref, o_ref):
    c = c_ref[...]
    s = c * _sigmoid(c)
    o_ref[0] = jnp.dot(s, w_ref[0], preferred_element_type=F32,
                       precision=lax.Precision.HIGHEST) + b_ref[0]


def _modulation(rows, w_mod, b_mod):
    depth, d, n = w_mod.shape
    tn = 1024
    return pl.pallas_call(
        _mod_kernel,
        out_shape=jax.ShapeDtypeStruct((depth, rows.shape[0], n), F32),
        grid=(depth, n // tn),
        in_specs=[pl.BlockSpec(rows.shape, lambda l, j: (0, 0)),
                  pl.BlockSpec((1, d, tn), lambda l, j: (l, 0, j)),
                  pl.BlockSpec((1, 1, tn), lambda l, j: (l, 0, j))],
        out_specs=pl.BlockSpec((1, rows.shape[0], tn), lambda l, j: (l, 0, j)),
        name="modulation",
    )(rows, w_mod, b_mod.reshape(depth, 1, n))


def _proj_kernel(x_ref, mod_ref, gpre_ref, cos_ref, slo_ref, shi_ref,
                 wa_ref, wb_ref, wg_ref, wgt_ref, wc_ref, wd_ref, wgate_ref,
                 gbias_ref, gbiast_ref, cgq_ref, cgkv_ref, wuq_ref, wukv_ref,
                 aq_ref, ak_ref, av_ref, bq_ref, bk_ref, bv_ref, bo_ref, bg_ref, bgt_ref,
                 cq_ref, ck_ref, cv_ref, dq_ref, dk_ref, dv_ref, gate_ref):
    x = x_ref[0]
    y = _rms(x, gpre_ref[...])
    u = (y * (1.0 + mod_ref[0, 1:2, :]) + mod_ref[0, 0:1, :]).astype(BF16)
    cos, slo, shi = cos_ref[...], slo_ref[...], shi_ref[...]

    def rope(v):
        return _rope(v, cos, slo, shi)

    def chunks(v, n):
        return [v[:, i * LANES:(i + 1) * LANES] for i in range(n)]

    pa = jnp.dot(u, wa_ref[...], preferred_element_type=F32)
    for i, v in enumerate(chunks(pa, 4)):
        aq_ref[0, :, i * LANES:(i + 1) * LANES] = (rope(v) * (A_HD ** -0.5)).astype(BF16)
    ak_ref[0] = rope(pa[:, 512:640]).astype(BF16)
    av_ref[0] = pa[:, 640:768].astype(BF16)

    pb = jnp.dot(u, wb_ref[...], preferred_element_type=F32)
    bq_ref[0] = pb[:, 0:512].astype(BF16)
    bk_ref[0] = (pb[:, 512:1024] * (B_HD ** -0.5)).astype(BF16)
    bv_ref[0] = pb[:, 1024:1536].astype(BF16)
    bo_ref[0] = pb[:, 1536:2048].astype(BF16)
    bg_ref[0] = jnp.dot(u, wg_ref[...], preferred_element_type=F32) + gbias_ref[...]
    bgt_ref[0] = _dot_nt(wgt_ref[...], u) + gbiast_ref[...]

    pc = jnp.dot(u, wc_ref[...], preferred_element_type=F32)
    cqn = _rms(pc[:, 0:C_Q_LORA], cgq_ref[...]).astype(BF16)
    ckvn = _rms(pc[:, C_Q_LORA:C_Q_LORA + C_KV_LORA], cgkv_ref[...]).astype(BF16)
    kr = rope(pc[:, 768:896]).astype(BF16)
    cq = jnp.dot(cqn, wuq_ref[...], preferred_element_type=F32)
    ckv = jnp.dot(ckvn, wukv_ref[...], preferred_element_type=F32)
    c_scale = (C_NOPE + C_ROPE) ** -0.5
    for h in range(C_HEADS):
        o = h * C_QK_PAD
        cq_ref[0, :, o:o + LANES] = (cq[:, o:o + LANES] * c_scale).astype(BF16)
        cq_ref[0, :, o + LANES:o + 2 * LANES] = (rope(cq[:, o + LANES:o + 2 * LANES]) * c_scale).astype(BF16)
        ck_ref[0, :, o:o + LANES] = ckv[:, h * LANES:(h + 1) * LANES].astype(BF16)
        ck_ref[0, :, o + LANES:o + 2 * LANES] = kr
    cv_ref[0] = ckv[:, 512:1024].astype(BF16)

    pd = jnp.dot(u, wd_ref[...], preferred_element_type=F32)
    for i in range(4):
        sl = slice(i * LANES, (i + 1) * LANES)
        dq_ref[0, :, sl] = (rope(pd[:, sl]) * (D_HD ** -0.5)).astype(BF16)
        dk_ref[0, :, sl] = rope(pd[:, 512 + i * LANES:512 + (i + 1) * LANES]).astype(BF16)
    dv_ref[0] = pd[:, 1024:1536].astype(BF16)

    for i in range(N_BRANCH):
        sl = slice(i * 1024, (i + 1) * 1024)
        gate_ref[0, :, sl] = jnp.dot(u, wgate_ref[:, sl], preferred_element_type=F32).astype(BF16)


def _project(hh, modall, gpre, tables, w):
    b, tt, d = hh.shape
    tp = TOKEN_TILE
    nt = tt // tp
    cos, slo, shi = tables

    def tok(width):
        return pl.BlockSpec((1, tp, width), lambda bi, t: (bi, t, 0))

    tab = pl.BlockSpec((tp, LANES), lambda bi, t: (t, 0))
    in_specs = [
        tok(d),
        pl.BlockSpec((1, 6, d), lambda bi, t: (bi * 2 + jnp.minimum(t, 1), 0, 0)),
        _resident((1, d)), tab, tab, tab,
        _resident(w['wa'].shape), _resident(w['wb'].shape), _resident(w['wg'].shape),
        _resident(w['wgt'].shape), _resident(w['wc'].shape), _resident(w['wd'].shape),
        _resident(w['wgate'].shape), _resident(w['gbias'].shape), _resident(w['gbiast'].shape),
        _resident(w['cgq'].shape), _resident(w['cgkv'].shape), _resident(w['wuq'].shape),
        _resident(w['wukv'].shape),
    ]
    widths = dict(aq=512, ak=128, av=128, bq=512, bk=512, bv=512, bo=512,
                  cq=C_HEADS * C_QK_PAD, ck=C_HEADS * C_QK_PAD, cv=512,
                  dq=512, dk=512, dv=512, gate=N_BRANCH * d)
    names = ['aq', 'ak', 'av', 'bq', 'bk', 'bv', 'bo', 'bg', 'bgt',
             'cq', 'ck', 'cv', 'dq', 'dk', 'dv', 'gate']
    out_shape, out_specs = [], []
    for n in names:
        if n == 'bg':
            out_shape.append(jax.ShapeDtypeStruct((b, tt, LANES), F32))
            out_specs.append(tok(LANES))
        elif n == 'bgt':
            out_shape.append(jax.ShapeDtypeStruct((b, 16, tt), F32))
            out_specs.append(pl.BlockSpec((1, 16, tp), lambda bi, t: (bi, 0, t)))
        else:
            out_shape.append(jax.ShapeDtypeStruct((b, tt, widths[n]), BF16))
            out_specs.append(tok(widths[n]))
    outs = pl.pallas_call(
        _proj_kernel,
        out_shape=out_shape,
        grid=(b, nt),
        in_specs=in_specs,
        out_specs=out_specs,
        compiler_params=pltpu.CompilerParams(
            dimension_semantics=("parallel", "parallel"), vmem_limit_bytes=VMEM_LIMIT),
        name="project",
    )(hh, modall, gpre, cos, slo, shi, w['wa'], w['wb'], w['wg'], w['wgt'], w['wc'], w['wd'],
      w['wgate'], w['gbias'], w['gbiast'], w['cgq'], w['cgkv'], w['wuq'], w['wukv'])
    return dict(zip(names, outs))


def _window_kernel(sink_ref, q_ref, kc_ref, kp_ref, kk_ref, kn_ref,
                   vc_ref, vp_ref, vk_ref, vn_ref, o_ref, *, n_tiles, ctx_tiles):
    t = pl.program_id(1)
    tw = WINDOW_TILE
    kcat = jnp.concatenate([kc_ref[0], kp_ref[0], kk_ref[0], kn_ref[0]], axis=0)
    vcat = jnp.concatenate([vc_ref[0], vp_ref[0], vk_ref[0], vn_ref[0]], axis=0)
    n_ctx = kc_ref.shape[1]
    row = lax.broadcasted_iota(jnp.int32, (tw, n_ctx + 3 * tw), 0)
    col = lax.broadcasted_iota(jnp.int32, (tw, n_ctx + 3 * tw), 1)
    wcol = col - n_ctx
    near = jnp.abs(wcol - tw - row) <= A_WINDOW
    prev_ok = (t >= ctx_tiles + 1).astype(jnp.int32)
    cur_ok = (t >= ctx_tiles).astype(jnp.int32)
    next_ok = jnp.logical_and(t >= ctx_tiles, t <= n_tiles - 2).astype(jnp.int32)
    seg_ok = jnp.where(wcol < tw, prev_ok, jnp.where(wcol < 2 * tw, cur_ok, next_ok))
    valid = jnp.logical_or(col < n_ctx, jnp.logical_and(near, seg_ok > 0))
    lane = lax.broadcasted_iota(jnp.int32, (tw, LANES), 1)
    low = lane < A_HD
    for j in range(A_HEADS // 2):
        qb = q_ref[0, :, j * LANES:(j + 1) * LANES]
        halves = []
        for half in range(2):
            head = j + (A_HEADS // 2) * half
            qm = jnp.where(low if half == 0 else jnp.logical_not(low), qb, jnp.zeros_like(qb))
            s = jnp.where(valid, _dot_nt(qm, kcat), NEG_BIG)
            sink = sink_ref[head]
            m = jnp.maximum(jnp.max(s, axis=-1, keepdims=True), sink)
            e = jnp.exp(s - m)
            den = jnp.sum(e, axis=-1, keepdims=True) + jnp.exp(sink - m)
            p = (e / den).astype(BF16)
            halves.append(jnp.dot(p, vcat, preferred_element_type=F32))
        o_ref[0, :, j * LANES:(j + 1) * LANES] = jnp.where(low, halves[0], halves[1]).astype(BF16)


def _window_attention(q, k, v, sink, ctx_len):
    b, tt, _ = q.shape
    tw = WINDOW_TILE
    assert ctx_len == tw, "context must be exactly one key tile"
    nt = tt // tw
    kv = lambda f: pl.BlockSpec((1, tw, LANES), lambda bi, t: (bi, f(t), 0))
    specs = [kv(lambda t: 0), kv(lambda t: jnp.maximum(t - 1, 0)), kv(lambda t: t),
             kv(lambda t: jnp.minimum(t + 1, nt - 1))]
    return pl.pallas_call(
        functools.partial(_window_kernel, n_tiles=nt, ctx_tiles=ctx_len // tw),
        out_shape=jax.ShapeDtypeStruct((b, tt, 512), BF16),
        grid=(b, nt),
        in_specs=[pl.BlockSpec(memory_space=pltpu.SMEM),
                  pl.BlockSpec((1, tw, 512), lambda bi, t: (bi, t, 0))] + specs + specs,
        out_specs=pl.BlockSpec((1, tw, 512), lambda bi, t: (bi, t, 0)),
        compiler_params=pltpu.CompilerParams(
            dimension_semantics=("parallel", "parallel"), vmem_limit_bytes=VMEM_LIMIT),
        name="window_attention",
    )(sink, q, k, k, k, k, v, v, v, v)


def _mlstm_kernel(qf_ref, kf_ref, vf_ref, gf_ref, gtf_ref, qb_ref, kb_ref, vb_ref, gb_ref, gtb_ref,
                  hf_ref, hb_ref, c_ref, n_ref, m_ref):
    step = pl.program_id(1)

    @pl.when(step == 0)
    def _():
        c_ref[...] = jnp.zeros_like(c_ref)
        n_ref[...] = jnp.zeros_like(n_ref)
        m_ref[...] = jnp.zeros_like(m_ref)

    L = B_CHUNK
    row = lax.broadcasted_iota(jnp.int32, (L, L), 0)
    col = lax.broadcasted_iota(jnp.int32, (L, L), 1)
    dirs = ((qf_ref, kf_ref, vf_ref, gf_ref, gtf_ref, hf_ref),
            (qb_ref, kb_ref, vb_ref, gb_ref, gtb_ref, hb_ref))
    hi = lax.Precision.HIGHEST
    for d, (q_ref, k_ref, v_ref, g_ref, gt_ref, out_ref) in enumerate(dirs):
        before = (col <= row) if d == 0 else (col >= row)
        cum = before.astype(F32)
        cum_t = ((row <= col) if d == 0 else (row >= col)).astype(F32)
        last = L - 1 if d == 0 else 0
        g = g_ref[0]
        gt = gt_ref[0]
        fcol_all = jnp.dot(cum, _log_sigmoid(g), preferred_element_type=F32, precision=hi)
        frow_all = jnp.dot(_log_sigmoid(gt), cum_t, preferred_element_type=F32, precision=hi)
        for h in range(B_HEADS):
            idx = d * B_HEADS + h
            ci, cf = (2 * d) * B_HEADS + h, (2 * d + 1) * B_HEADS + h
            i_col, i_row = g[:, ci:ci + 1], gt[ci:ci + 1, :]
            f_col, f_row = fcol_all[:, cf:cf + 1], frow_all[cf:cf + 1, :]
            b_last = f_row[:, last:last + 1]
            m_prev = m_ref[idx][:, 0:1]
            q = q_ref[0, :, h * B_HD:(h + 1) * B_HD]
            k = k_ref[0, :, h * B_HD:(h + 1) * B_HD]
            v = v_ref[0, :, h * B_HD:(h + 1) * B_HD]
            dmat = jnp.where(before, f_col - f_row + i_row, -jnp.inf)
            m_t = jnp.maximum(f_col + m_prev, jnp.max(dmat, axis=-1, keepdims=True))
            wqk = jnp.exp(dmat - m_t) * _dot_nt(q, k)
            a = jnp.exp(f_col + m_prev - m_t)
            ct = c_ref[idx]
            n_row = n_ref[idx]
            num = (jnp.dot(wqk.astype(BF16), v, preferred_element_type=F32)
                   + a * jnp.dot(q, ct.astype(BF16), preferred_element_type=F32))
            den = (jnp.sum(wqk, axis=-1, keepdims=True)
                   + a * jnp.sum(q.astype(F32) * n_row, axis=-1, keepdims=True))
            hval = num / jnp.maximum(jnp.abs(den), jnp.exp(-m_t))
            out_ref[0, :, h * B_HD:(h + 1) * B_HD] = hval.astype(BF16)
            g_col = b_last - f_col + i_col
            g_row = b_last - f_row + i_row
            m_new = jnp.maximum(b_last + m_prev, jnp.max(g_row, axis=-1, keepdims=True))
            decay = jnp.exp(b_last + m_prev - m_new)
            ws = jnp.exp(g_col - m_new)
            vs = (v.astype(F32) * ws).astype(BF16)
            c_ref[idx] = decay * ct + _dot_tn(k, vs)
            n_ref[idx] = decay * n_row + jnp.sum(k.astype(F32) * ws, axis=0, keepdims=True)
            m_ref[idx] = jnp.broadcast_to(m_new, (1, LANES))


def _mlstm(q, k, v, g, gt, ctx_len):
    b, tt, w = q.shape
    L = B_CHUNK
    nc = tt // L
    cc = ctx_len // L

    def fwd(s):
        return s

    def bwd(s):
        return jnp.where(s < cc, cc - 1 - s, nc - 1 + cc - s)

    def specs(f):
        tok = pl.BlockSpec((1, L, w), lambda bi, s: (bi, f(s), 0))
        return [tok, tok, tok,
                pl.BlockSpec((1, L, LANES), lambda bi, s: (bi, f(s), 0)),
                pl.BlockSpec((1, 16, L), lambda bi, s: (bi, 0, f(s)))]

    out = jax.ShapeDtypeStruct((b, tt, w), BF16)
    return pl.pallas_call(
        _mlstm_kernel,
        out_shape=[out, out],
        grid=(b, nc),
        in_specs=specs(fwd) + specs(bwd),
        out_specs=[pl.BlockSpec((1, L, w), lambda bi, s: (bi, fwd(s), 0)),
                   pl.BlockSpec((1, L, w), lambda bi, s: (bi, bwd(s), 0))],
        scratch_shapes=[pltpu.VMEM((2 * B_HEADS, B_HD, B_HD), F32),
                        pltpu.VMEM((2 * B_HEADS, 1, B_HD), F32),
                        pltpu.VMEM((2 * B_HEADS, 1, LANES), F32)],
        compiler_params=pltpu.CompilerParams(
            dimension_semantics=("parallel", "arbitrary"), vmem_limit_bytes=VMEM_LIMIT),
        name="mlstm",
    )(q, k, v, g, gt, q, k, v, g, gt)


def _online_update(s, v, m_ref, l_ref, acc_ref):
    m_old = m_ref[...]
    m_new = jnp.maximum(m_old, jnp.max(s, axis=-1, keepdims=True))
    alpha = jnp.exp(m_old - m_new)
    p = jnp.exp(s - m_new)
    l_ref[...] = alpha * l_ref[...] + jnp.sum(p, axis=-1, keepdims=True)
    acc_ref[...] = alpha * acc_ref[...] + jnp.dot(p.astype(BF16), v, preferred_element_type=F32)
    m_ref[...] = m_new


def _flash_steps(q_list, k_ref, v_ref, stats, ctx_len, masked):
    j = pl.program_id(3)
    tq = q_list[0].shape[0]
    n_sub = k_ref.shape[1] // FLASH_SUB
    if masked:
        ctx_row = lax.broadcasted_iota(jnp.int32, (tq, 1), 0) < ctx_len
    for c in range(n_sub):
        k = k_ref[0, c * FLASH_SUB:(c + 1) * FLASH_SUB, :]
        v = v_ref[0, c * FLASH_SUB:(c + 1) * FLASH_SUB, :]
        for q, (m_ref, l_ref, acc_ref) in zip(q_list, stats):
            s = _dot_nt(q, k)
            if masked:
                hide = ctx_row if c > 0 else jnp.logical_and(ctx_row, j > 0)
                s = jnp.where(hide, NEG_BIG, s)
            _online_update(s, v, m_ref, l_ref, acc_ref)


def _flash_init(stats):
    for m_ref, l_ref, acc_ref in stats:
        m_ref[...] = jnp.full_like(m_ref, NEG_BIG)
        l_ref[...] = jnp.zeros_like(l_ref)
        acc_ref[...] = jnp.zeros_like(acc_ref)


def _mla_kernel(q_ref, k_ref, v_ref, o_ref, m_ref, l_ref, acc_ref, *, ctx_len):
    i, j = pl.program_id(2), pl.program_id(3)
    stats = [(m_ref, l_ref, acc_ref)]

    @pl.when(j == 0)
    def _():
        _flash_init(stats)

    @pl.when(i == 0)
    def _():
        _flash_steps([q_ref[0]], k_ref, v_ref, stats, ctx_len, True)

    @pl.when(i > 0)
    def _():
        _flash_steps([q_ref[0]], k_ref, v_ref, stats, ctx_len, False)

    @pl.when(j == pl.num_programs(3) - 1)
    def _():
        o_ref[0] = (acc_ref[...] / l_ref[...]).astype(BF16)


def _diff_kernel(lam_ref, gsub_ref, q_ref, k_ref, v_ref, o_ref,
                 m1_ref, l1_ref, acc1_ref, m2_ref, l2_ref, acc2_ref, *, ctx_len, lam_init):
    i, j = pl.program_id(2), pl.program_id(3)
    stats = [(m1_ref, l1_ref, acc1_ref), (m2_ref, l2_ref, acc2_ref)]

    @pl.when(j == 0)
    def _():
        _flash_init(stats)

    def run(masked):
        q = q_ref[0]
        low = lax.broadcasted_iota(jnp.int32, q.shape, 1) < D_HD
        zero = jnp.zeros_like(q)
        _flash_steps([jnp.where(low, q, zero), jnp.where(low, zero, q)], k_ref, v_ref, stats, ctx_len, masked)

    @pl.when(i == 0)
    def _():
        run(True)

    @pl.when(i > 0)
    def _():
        run(False)

    @pl.when(j == pl.num_programs(3) - 1)
    def _():
        lq1, lk1, lq2, lk2 = (lam_ref[r:r + 1, :] for r in range(4))
        lam = (jnp.exp(jnp.sum(lq1 * lk1, axis=-1, keepdims=True))
               - jnp.exp(jnp.sum(lq2 * lk2, axis=-1, keepdims=True)) + lam_init)
        o = acc1_ref[...] / l1_ref[...] - lam * (acc2_ref[...] / l2_ref[...])
        o_ref[0] = (_rms(o, gsub_ref[...]) * (1.0 - lam_init)).astype(BF16)


def _flash_call(kernel, extra_inputs, extra_specs, q, k, v, heads, qk_width, n_stats, name):
    b, tt, _ = q.shape
    tq, tk = min(FLASH_TQ, tt), min(FLASH_TK, tt)
    scratch = []
    for _ in range(n_stats):
        scratch += [pltpu.VMEM((tq, 1), F32), pltpu.VMEM((tq, 1), F32), pltpu.VMEM((tq, LANES), F32)]
    return pl.pallas_call(
        kernel,
        out_shape=jax.ShapeDtypeStruct((b, tt, heads * LANES), BF16),
        grid=(b, heads, tt // tq, tt // tk),
        in_specs=extra_specs + [
            pl.BlockSpec((1, tq, qk_width), lambda bi, h, i, j: (bi, i, h)),
            pl.BlockSpec((1, tk, qk_width), lambda bi, h, i, j: (bi, j, h)),
            pl.BlockSpec((1, tk, LANES), lambda bi, h, i, j: (bi, j, h))],
        out_specs=pl.BlockSpec((1, tq, LANES), lambda bi, h, i, j: (bi, i, h)),
        scratch_shapes=scratch,
        compiler_params=pltpu.CompilerParams(
            dimension_semantics=("parallel", "parallel", "parallel", "arbitrary"),
            vmem_limit_bytes=VMEM_LIMIT),
        name=name,
    )(*extra_inputs, q, k, v)


def _mla_attention(q, k, v, ctx_len):
    return _flash_call(functools.partial(_mla_kernel, ctx_len=ctx_len), [], [],
                       q, k, v, C_HEADS, C_QK_PAD, 1, "mla_attention")


def _diff_attention(q, k, v, lam_vecs, gsub, lam_init, ctx_len):
    whole = lambda shape: pl.BlockSpec(shape, lambda bi, h, i, j: (0, 0))
    return _flash_call(functools.partial(_diff_kernel, ctx_len=ctx_len, lam_init=lam_init),
                       [lam_vecs, gsub], [whole(lam_vecs.shape), whole(gsub.shape)],
                       q, k, v, D_HEADS, 2 * D_HD, 2, "diff_attention")


def _merge_kernel(ya_ref, hf_ref, hb_ref, bo_ref, yc_ref, yd_ref, gate_ref, wbr_ref, wout_ref,
                  h_ref, mod_ref, gpost_ref, o_ref):
    d = h_ref.shape[2]
    yb = (_sigmoid(bo_ref[0].astype(F32))
          * (hf_ref[0].astype(F32) + hb_ref[0].astype(F32))).astype(BF16)
    ys = (ya_ref[0], yb, yc_ref[0], yd_ref[0])
    merged = None
    for i, y in enumerate(ys):
        g = _sigmoid(gate_ref[0, :, i * d:(i + 1) * d].astype(F32))
        term = g * jnp.dot(y, wbr_ref[i], preferred_element_type=F32)
        merged = term if merged is None else merged + term
    out = jnp.dot(merged.astype(BF16), wout_ref[...], preferred_element_type=F32)
    o_ref[0] = h_ref[0] + mod_ref[0, 2:3, :] * _rms(out, gpost_ref[...])


def _merge(p, ya, hf, hb, yc, yd, wbr, wout, hh, modall, gpost):
    b, tt, d = hh.shape
    tp = TOKEN_TILE
    tok = lambda width: pl.BlockSpec((1, tp, width), lambda bi, t: (bi, t, 0))
    return pl.pallas_call(
        _merge_kernel,
        out_shape=jax.ShapeDtypeStruct(hh.shape, F32),
        grid=(b, tt // tp),
        in_specs=[tok(BRANCH_W)] * 6 + [tok(N_BRANCH * d), _resident(wbr.shape), _resident(wout.shape),
                  tok(d),
                  pl.BlockSpec((1, 6, d), lambda bi, t: (bi * 2 + jnp.minimum(t, 1), 0, 0)),
                  _resident((1, d))],
        out_specs=tok(d),
        compiler_params=pltpu.CompilerParams(
            dimension_semantics=("parallel", "parallel"), vmem_limit_bytes=VMEM_LIMIT),
        name="merge",
    )(ya, hf, hb, p['bo'], yc, yd, p['gate'], wbr, wout, hh, modall, gpost)


def _ffn_kernel(h_ref, mod_ref, gpre_ref, gpost_ref, wg_ref, wu_ref, wd_ref, o_ref):
    x = h_ref[0]
    u = (_rms(x, gpre_ref[...]) * (1.0 + mod_ref[0, 4:5, :]) + mod_ref[0, 3:4, :]).astype(BF16)
    gate = jnp.dot(u, wg_ref[...], preferred_element_type=F32)
    up = jnp.dot(u, wu_ref[...], preferred_element_type=F32)
    act = (gate * _sigmoid(gate) * up).astype(BF16)
    out = jnp.dot(act, wd_ref[...], preferred_element_type=F32)
    o_ref[0] = x + mod_ref[0, 5:6, :] * _rms(out, gpost_ref[...])


def _ffn(hh, modall, gpre, gpost, wg, wu, wd):
    b, tt, d = hh.shape
    tp = TOKEN_TILE
    tok = pl.BlockSpec((1, tp, d), lambda bi, t: (bi, t, 0))
    return pl.pallas_call(
        _ffn_kernel,
        out_shape=jax.ShapeDtypeStruct(hh.shape, F32),
        grid=(b, tt // tp),
        in_specs=[tok, pl.BlockSpec((1, 6, d), lambda bi, t: (bi * 2 + jnp.minimum(t, 1), 0, 0)),
                  _resident((1, d)), _resident((1, d)),
                  _resident(wg.shape), _resident(wu.shape), _resident(wd.shape)],
        out_specs=tok,
        compiler_params=pltpu.CompilerParams(
            dimension_semantics=("parallel", "parallel"), vmem_limit_bytes=VMEM_LIMIT),
        name="ffn",
    )(hh, modall, gpre, gpost, wg, wu, wd)


def _rope_tables(n_rows, ctx_len):
    row = jnp.repeat(jnp.arange(n_rows, dtype=F32), GRID_W)
    col = jnp.tile(jnp.arange(GRID_W, dtype=F32), n_rows)
    quarter = ROPE_DIM // 4
    inv_freq = ROPE_BASE ** (-jnp.arange(quarter, dtype=F32) / quarter)
    ang_r, ang_c = row[:, None] * inv_freq, col[:, None] * inv_freq
    ang = jnp.concatenate([ang_r, ang_r, ang_c, ang_c], axis=-1)
    cos, sin = jnp.cos(ang), jnp.sin(ang)
    cos = jnp.concatenate([jnp.ones((ctx_len, ROPE_DIM), F32), cos], axis=0)
    sin = jnp.concatenate([jnp.zeros((ctx_len, ROPE_DIM), F32), sin], axis=0)
    cos, sin = jnp.tile(cos, (1, LANES // ROPE_DIM)), jnp.tile(sin, (1, LANES // ROPE_DIM))
    first = (jnp.arange(LANES) % (ROPE_DIM // 2)) < quarter
    return cos, jnp.where(first, -sin, 0.0), jnp.where(first, 0.0, sin)


def _a_head_perm():
    idx = []
    for j in range(A_HEADS // 2):
        for half in range(2):
            head = j + (A_HEADS // 2) * half
            idx.extend(range(head * A_HD, (head + 1) * A_HD))
    return np.asarray(idx)


def _prep_layer(w_in, b_gate_bias, c_g_q, c_g_kv, c_w_uq, c_w_uk, c_w_uv, w_branch):
    d = w_in.shape[0]
    splits = (512, 128, 128, 512, 512, 512, 16, 512, 512, 256, 64, 512, 512, 512, N_BRANCH * d)
    offs = np.concatenate([[0], np.cumsum(splits)])
    part = [w_in[:, offs[i]:offs[i + 1]] for i in range(len(splits))]
    (a_q, a_k, a_v, b_q, b_k, b_v, b_g, b_o, c_q, c_kv, c_kr, d_q, d_k, d_v, gate) = part
    perm = _a_head_perm()
    zeros = lambda n: jnp.zeros((d, n), w_in.dtype)
    uq = c_w_uq.reshape(C_Q_LORA, C_HEADS, C_NOPE + C_ROPE)
    uq = jnp.pad(uq, ((0, 0), (0, 0), (0, C_QK_PAD - C_NOPE - C_ROPE))).reshape(C_Q_LORA, C_HEADS * C_QK_PAD)
    w = {
        'wa': jnp.concatenate([a_q[:, perm], a_k, a_v], axis=1).astype(BF16),
        'wb': jnp.concatenate([b_q, b_k, b_v, b_o], axis=1).astype(BF16),
        'wg': jnp.concatenate([b_g, zeros(LANES - 16)], axis=1).astype(BF16),
        'wgt': b_g.T.astype(BF16),
        'wc': jnp.concatenate([c_q, c_kv, c_kr, zeros(LANES - C_ROPE)], axis=1).astype(BF16),
        'wd': jnp.concatenate([d_q, d_k, d_v], axis=1).astype(BF16),
        'wgate': gate.astype(BF16),
        'gbias': jnp.pad(b_gate_bias, (0, LANES - 16)).reshape(1, LANES),
        'gbiast': b_gate_bias.reshape(16, 1),
        'cgq': c_g_q.reshape(1, -1),
        'cgkv': c_g_kv.reshape(1, -1),
        'wuq': uq.astype(BF16),
        'wukv': jnp.concatenate([c_w_uk, c_w_uv], axis=1).astype(BF16),
    }
    wbr = jnp.concatenate([w_branch[0:1, perm], w_branch[1:]], axis=0).astype(BF16)
    return w, wbr


def kernel(x, c, ctx, c_ctx, w_mod, b_mod, g_mix_pre, g_mix_post, g_ffn_pre, g_ffn_post, w_in, a_sink,
           b_gate_bias, c_g_q, c_g_kv, c_w_uq, c_w_uk, c_w_uv, d_lam_q1, d_lam_k1, d_lam_q2, d_lam_k2,
           d_g_sub, w_branch, w_out, w_gate_up, w_down):
    b, seq, d = x.shape
    ctx_len = ctx.shape[1]
    depth = w_in.shape[0]
    tables = _rope_tables(seq // GRID_W, ctx_len)

    rows = jnp.concatenate([c, c_ctx[None, :], jnp.zeros((8 - b - 1, d), F32)], axis=0)
    mods = _modulation(rows, w_mod, b_mod).reshape(depth, 8, 6, d)

    hh = jnp.concatenate([ctx, x], axis=1)
    for l in range(depth):
        modall = jnp.stack([jnp.broadcast_to(mods[l, b], (b, 6, d)), mods[l, :b]], axis=1).reshape(2 * b, 6, d)
        w, wbr = _prep_layer(w_in[l], b_gate_bias[l], c_g_q[l], c_g_kv[l], c_w_uq[l], c_w_uk[l],
                             c_w_uv[l], w_branch[l])
        lam_init = 0.8 - 0.6 * math.exp(-0.3 * l)
        lam_vecs = jnp.stack([d_lam_q1[l], d_lam_k1[l], d_lam_q2[l], d_lam_k2[l]], axis=0)
        hidden = w_down.shape[1]

        p = _project(hh, modall, g_mix_pre[l].reshape(1, d), tables, w)
        ya = _window_attention(p['aq'], p['ak'], p['av'], a_sink[l], ctx_len)
        hf, hb = _mlstm(p['bq'], p['bk'], p['bv'], p['bg'], p['bgt'], ctx_len)
        yc = _mla_attention(p['cq'], p['ck'], p['cv'], ctx_len)
        yd = _diff_attention(p['dq'], p['dk'], p['dv'], lam_vecs, d_g_sub[l].reshape(1, -1), lam_init, ctx_len)
        hh = _merge(p, ya, hf, hb, yc, yd, wbr, w_out[l].astype(BF16), hh, modall,
                    g_mix_post[l].reshape(1, d))
        hh = _ffn(hh, modall, g_ffn_pre[l].reshape(1, d), g_ffn_post[l].reshape(1, d),
                  w_gate_up[l, :, :hidden].astype(BF16), w_gate_up[l, :, hidden:].astype(BF16),
                  w_down[l].astype(BF16))
    return hh[:, ctx_len:]
```

```python
import functools
import math

import numpy as np
import jax
import jax.numpy as jnp
from jax import lax
from jax.experimental import pallas as pl
from jax.experimental.pallas import tpu as pltpu

F32 = jnp.float32
BF16 = jnp.bfloat16

GRID_W = 64
ROPE_BASE = 10000.0
NORM_EPS = 1e-6
A_HEADS, A_KV_HEADS, A_HD, A_WINDOW = 8, 2, 64, 128
B_HEADS, B_HD, B_CHUNK = 4, 128, 128
C_HEADS, C_Q_LORA, C_KV_LORA, C_NOPE, C_ROPE, C_VD = 4, 512, 256, 128, 64, 128
D_HEADS, D_HD, D_VD = 4, 64, 128
N_BRANCH = 4
BRANCH_W = 512
ROPE_DIM = 64
C_QK_PAD = 256

LANES = 128
NEG_BIG = -1e30
VMEM_LIMIT = 56 * 1024 * 1024

TOKEN_TILE = 256
WINDOW_TILE = 256
FLASH_TQ = 2048
FLASH_TK = 2048
FLASH_SUB = 512
LOG2E = math.log2(math.e)


def _resident(shape):
    nd = len(shape)
    return pl.BlockSpec(shape, lambda *_: (0,) * nd, pipeline_mode=pl.Buffered(1))


def _mod_spec(d, ctx_tile):
    return pl.BlockSpec((1, 6, d), lambda bi, t: (bi * 2 + jnp.where(t == ctx_tile, 0, 1), 0, 0))


def _sigmoid(x):
    return 1.0 / (1.0 + jnp.exp(-x))


def _log_sigmoid(x):
    return jnp.minimum(x, 0.0) - jnp.log(1.0 + jnp.exp(-jnp.abs(x)))


def _rms(x, g):
    return x * lax.rsqrt(jnp.mean(x * x, axis=-1, keepdims=True) + NORM_EPS) * g


def _rope(x, cos, sin_lo, sin_hi):
    return x * cos + pltpu.roll(x, LANES - 16, 1) * sin_lo + pltpu.roll(x, 16, 1) * sin_hi


def _dot_nt(a, b):
    return lax.dot_general(a, b, (((1,), (1,)), ((), ())), preferred_element_type=F32)


def _dot_tn(a, b):
    return lax.dot_general(a, b, (((0,), (0,)), ((), ())), preferred_element_type=F32)


def _mod_kernel(c_ref, w_ref, b_ref, o_ref):
    c = c_ref[...]
    s = c * _sigmoid(c)
    o_ref[0] = jnp.dot(s, w_ref[0], preferred_element_type=F32,
                       precision=lax.Precision.HIGHEST) + b_ref[0]


def _modulation(rows, w_mod, b_mod):
    depth, d, n = w_mod.shape
    tn = 1024
    return pl.pallas_call(
        _mod_kernel,
        out_shape=jax.ShapeDtypeStruct((depth, rows.shape[0], n), F32),
        grid=(depth, n // tn),
        in_specs=[pl.BlockSpec(rows.shape, lambda l, j: (0, 0)),
                  pl.BlockSpec((1, d, tn), lambda l, j: (l, 0, j)),
                  pl.BlockSpec((1, 1, tn), lambda l, j: (l, 0, j))],
        out_specs=pl.BlockSpec((1, rows.shape[0], tn), lambda l, j: (l, 0, j)),
        name="modulation",
    )(rows, w_mod, b_mod.reshape(depth, 1, n))


def _proj_kernel(x_ref, mod_ref, gpre_ref, cos_ref, slo_ref, shi_ref,
                 wa_ref, wb_ref, wg_ref, wgt_ref, wc_ref, wd_ref, wgate_ref,
                 gbias_ref, gbiast_ref, cgq_ref, cgkv_ref, wuq_ref, wukv_ref,
                 aq_ref, ak_ref, av_ref, bq_ref, bk_ref, bv_ref, bo_ref, bg_ref, bgt_ref,
                 cq_ref, ck_ref, cv_ref, dq_ref, dk_ref, dv_ref, gate_ref):
    x = x_ref[0]
    y = _rms(x, gpre_ref[...])
    u = (y * (1.0 + mod_ref[0, 1:2, :]) + mod_ref[0, 0:1, :]).astype(BF16)
    cos, slo, shi = cos_ref[...], slo_ref[...], shi_ref[...]

    def rope(v):
        return _rope(v, cos, slo, shi)

    def chunks(v, n):
        return [v[:, i * LANES:(i + 1) * LANES] for i in range(n)]

    pa = jnp.dot(u, wa_ref[...], preferred_element_type=F32)
    for i, v in enumerate(chunks(pa, 4)):
        aq_ref[0, :, i * LANES:(i + 1) * LANES] = (rope(v) * (A_HD ** -0.5)).astype(BF16)
    ak_ref[0] = rope(pa[:, 512:640]).astype(BF16)
    av_ref[0] = pa[:, 640:768].astype(BF16)

    pb = jnp.dot(u, wb_ref[...], preferred_element_type=F32)
    bq_ref[0] = pb[:, 0:512].astype(BF16)
    bk_ref[0] = (pb[:, 512:1024] * (B_HD ** -0.5)).astype(BF16)
    bv_ref[0] = pb[:, 1024:1536].astype(BF16)
    bo_ref[0] = pb[:, 1536:2048].astype(BF16)
    bg_ref[0] = jnp.dot(u, wg_ref[...], preferred_element_type=F32) + gbias_ref[...]
    bgt_ref[0] = _dot_nt(wgt_ref[...], u) + gbiast_ref[...]

    pc = jnp.dot(u, wc_ref[...], preferred_element_type=F32)
    cqn = _rms(pc[:, 0:C_Q_LORA], cgq_ref[...]).astype(BF16)
    ckvn = _rms(pc[:, C_Q_LORA:C_Q_LORA + C_KV_LORA], cgkv_ref[...]).astype(BF16)
    kr = rope(pc[:, 768:896]).astype(BF16)
    cq = jnp.dot(cqn, wuq_ref[...], preferred_element_type=F32)
    ckv = jnp.dot(ckvn, wukv_ref[...], preferred_element_type=F32)
    c_scale = (C_NOPE + C_ROPE) ** -0.5 * LOG2E
    for h in range(C_HEADS):
        o = h * C_QK_PAD
        cq_ref[0, :, o:o + LANES] = (cq[:, o:o + LANES] * c_scale).astype(BF16)
        cq_ref[0, :, o + LANES:o + 2 * LANES] = (rope(cq[:, o + LANES:o + 2 * LANES]) * c_scale).astype(BF16)
        ck_ref[0, :, o:o + LANES] = ckv[:, h * LANES:(h + 1) * LANES].astype(BF16)
        ck_ref[0, :, o + LANES:o + 2 * LANES] = kr
    cv_ref[0] = ckv[:, 512:1024].astype(BF16)

    pd = jnp.dot(u, wd_ref[...], preferred_element_type=F32)
    for i in range(4):
        sl = slice(i * LANES, (i + 1) * LANES)
        dq_ref[0, :, sl] = (rope(pd[:, sl]) * (D_HD ** -0.5 * LOG2E)).astype(BF16)
        dk_ref[0, :, sl] = rope(pd[:, 512 + i * LANES:512 + (i + 1) * LANES]).astype(BF16)
    dv_ref[0] = pd[:, 1024:1536].astype(BF16)

    for i in range(N_BRANCH):
        sl = slice(i * 1024, (i + 1) * 1024)
        gate_ref[0, :, sl] = jnp.dot(u, wgate_ref[:, sl], preferred_element_type=F32).astype(BF16)


def _project(hh, modall, gpre, tables, w):
    b, tt, d = hh.shape
    tp = TOKEN_TILE
    nt = tt // tp
    cos, slo, shi = tables

    def tok(width):
        return pl.BlockSpec((1, tp, width), lambda bi, t: (bi, t, 0))

    tab = pl.BlockSpec((tp, LANES), lambda bi, t: (t, 0))
    in_specs = [
        tok(d),
        _mod_spec(d, nt - 1),
        _resident((1, d)), tab, tab, tab,
        _resident(w['wa'].shape), _resident(w['wb'].shape), _resident(w['wg'].shape),
        _resident(w['wgt'].shape), _resident(w['wc'].shape), _resident(w['wd'].shape),
        _resident(w['wgate'].shape), _resident(w['gbias'].shape), _resident(w['gbiast'].shape),
        _resident(w['cgq'].shape), _resident(w['cgkv'].shape), _resident(w['wuq'].shape),
        _resident(w['wukv'].shape),
    ]
    widths = dict(aq=512, ak=128, av=128, bq=512, bk=512, bv=512, bo=512,
                  cq=C_HEADS * C_QK_PAD, ck=C_HEADS * C_QK_PAD, cv=512,
                  dq=512, dk=512, dv=512, gate=N_BRANCH * d)
    names = ['aq', 'ak', 'av', 'bq', 'bk', 'bv', 'bo', 'bg', 'bgt',
             'cq', 'ck', 'cv', 'dq', 'dk', 'dv', 'gate']
    out_shape, out_specs = [], []
    for n in names:
        if n == 'bg':
            out_shape.append(jax.ShapeDtypeStruct((b, tt, LANES), F32))
            out_specs.append(tok(LANES))
        elif n == 'bgt':
            out_shape.append(jax.ShapeDtypeStruct((b, 16, tt), F32))
            out_specs.append(pl.BlockSpec((1, 16, tp), lambda bi, t: (bi, 0, t)))
        else:
            out_shape.append(jax.ShapeDtypeStruct((b, tt, widths[n]), BF16))
            out_specs.append(tok(widths[n]))
    outs = pl.pallas_call(
        _proj_kernel,
        out_shape=out_shape,
        grid=(b, nt),
        in_specs=in_specs,
        out_specs=out_specs,
        compiler_params=pltpu.CompilerParams(
            dimension_semantics=("parallel", "parallel"), vmem_limit_bytes=VMEM_LIMIT),
        name="project",
    )(hh, modall, gpre, cos, slo, shi, w['wa'], w['wb'], w['wg'], w['wgt'], w['wc'], w['wd'],
      w['wgate'], w['gbias'], w['gbiast'], w['cgq'], w['cgkv'], w['wuq'], w['wukv'])
    return dict(zip(names, outs))


def _window_kernel(sink_ref, q_ref, kc_ref, kp_ref, kk_ref, kn_ref,
                   vc_ref, vp_ref, vk_ref, vn_ref, o_ref, *, lat_tiles):
    t = pl.program_id(1)
    tw = WINDOW_TILE
    kcat = jnp.concatenate([kc_ref[0], kp_ref[0], kk_ref[0], kn_ref[0]], axis=0)
    vcat = jnp.concatenate([vc_ref[0], vp_ref[0], vk_ref[0], vn_ref[0]], axis=0)
    n_ctx = kc_ref.shape[1]
    row = lax.broadcasted_iota(jnp.int32, (tw, n_ctx + 3 * tw), 0)
    col = lax.broadcasted_iota(jnp.int32, (tw, n_ctx + 3 * tw), 1)
    wcol = col - n_ctx
    near = jnp.abs(wcol - tw - row) <= A_WINDOW
    prev_ok = jnp.logical_and(t >= 1, t < lat_tiles).astype(jnp.int32)
    cur_ok = (t < lat_tiles).astype(jnp.int32)
    next_ok = (t < lat_tiles - 1).astype(jnp.int32)
    seg_ok = jnp.where(wcol < tw, prev_ok, jnp.where(wcol < 2 * tw, cur_ok, next_ok))
    valid = jnp.logical_or(col < n_ctx, jnp.logical_and(near, seg_ok > 0))
    lane = lax.broadcasted_iota(jnp.int32, (tw, LANES), 1)
    low = lane < A_HD
    for j in range(A_HEADS // 2):
        qb = q_ref[0, :, j * LANES:(j + 1) * LANES]
        halves = []
        for half in range(2):
            head = j + (A_HEADS // 2) * half
            qm = jnp.where(low if half == 0 else jnp.logical_not(low), qb, jnp.zeros_like(qb))
            s = jnp.where(valid, _dot_nt(qm, kcat), NEG_BIG)
            sink = sink_ref[head]
            m = jnp.maximum(jnp.max(s, axis=-1, keepdims=True), sink)
            e = jnp.exp(s - m)
            den = jnp.sum(e, axis=-1, keepdims=True) + jnp.exp(sink - m)
            p = (e / den).astype(BF16)
            halves.append(jnp.dot(p, vcat, preferred_element_type=F32))
        o_ref[0, :, j * LANES:(j + 1) * LANES] = jnp.where(low, halves[0], halves[1]).astype(BF16)


def _window_attention(q, k, v, sink, ctx_len):
    b, tt, _ = q.shape
    tw = WINDOW_TILE
    assert ctx_len == tw, "context must be exactly one key tile"
    nt = tt // tw
    nl = nt - 1
    kv = lambda f: pl.BlockSpec((1, tw, LANES), lambda bi, t: (bi, f(t), 0))
    specs = [kv(lambda t: nl), kv(lambda t: jnp.maximum(t - 1, 0)), kv(lambda t: t),
             kv(lambda t: jnp.minimum(t + 1, nl - 1))]
    return pl.pallas_call(
        functools.partial(_window_kernel, lat_tiles=nl),
        out_shape=jax.ShapeDtypeStruct((b, tt, 512), BF16),
        grid=(b, nt),
        in_specs=[pl.BlockSpec(memory_space=pltpu.SMEM),
                  pl.BlockSpec((1, tw, 512), lambda bi, t: (bi, t, 0))] + specs + specs,
        out_specs=pl.BlockSpec((1, tw, 512), lambda bi, t: (bi, t, 0)),
        compiler_params=pltpu.CompilerParams(
            dimension_semantics=("parallel", "parallel"), vmem_limit_bytes=VMEM_LIMIT),
        name="window_attention",
    )(sink, q, k, k, k, k, v, v, v, v)


def _mlstm_kernel(qf_ref, kf_ref, vf_ref, gf_ref, gtf_ref, qb_ref, kb_ref, vb_ref, gb_ref, gtb_ref,
                  hf_ref, hb_ref, c_ref, n_ref, m_ref):
    step = pl.program_id(1)

    @pl.when(step == 0)
    def _():
        c_ref[...] = jnp.zeros_like(c_ref)
        n_ref[...] = jnp.zeros_like(n_ref)
        m_ref[...] = jnp.zeros_like(m_ref)

    L = B_CHUNK
    row = lax.broadcasted_iota(jnp.int32, (L, L), 0)
    col = lax.broadcasted_iota(jnp.int32, (L, L), 1)
    dirs = ((qf_ref, kf_ref, vf_ref, gf_ref, gtf_ref, hf_ref),
            (qb_ref, kb_ref, vb_ref, gb_ref, gtb_ref, hb_ref))
    hi = lax.Precision.HIGHEST
    for d, (q_ref, k_ref, v_ref, g_ref, gt_ref, out_ref) in enumerate(dirs):
        before = (col <= row) if d == 0 else (col >= row)
        cum = before.astype(F32)
        cum_t = ((row <= col) if d == 0 else (row >= col)).astype(F32)
        last = L - 1 if d == 0 else 0
        g = g_ref[0]
        gt = gt_ref[0]
        fcol_all = jnp.dot(cum, _log_sigmoid(g), preferred_element_type=F32, precision=hi)
        frow_all = jnp.dot(_log_sigmoid(gt), cum_t, preferred_element_type=F32, precision=hi)
        for h in range(B_HEADS):
            idx = d * B_HEADS + h
            ci, cf = (2 * d) * B_HEADS + h, (2 * d + 1) * B_HEADS + h
            i_col, i_row = g[:, ci:ci + 1], gt[ci:ci + 1, :]
            f_col, f_row = fcol_all[:, cf:cf + 1], frow_all[cf:cf + 1, :]
            b_last = f_row[:, last:last + 1]
            m_prev = m_ref[idx][:, 0:1]
            q = q_ref[0, :, h * B_HD:(h + 1) * B_HD]
            k = k_ref[0, :, h * B_HD:(h + 1) * B_HD]
            v = v_ref[0, :, h * B_HD:(h + 1) * B_HD]
            dmat = jnp.where(before, f_col - f_row + i_row, -jnp.inf)
            m_t = jnp.maximum(f_col + m_prev, jnp.max(dmat, axis=-1, keepdims=True))
            wqk = jnp.exp(dmat - m_t) * _dot_nt(q, k)
            a = jnp.exp(f_col + m_prev - m_t)
            ct = c_ref[idx]
            n_row = n_ref[idx]
            num = (jnp.dot(wqk.astype(BF16), v, preferred_element_type=F32)
                   + a * jnp.dot(q, ct.astype(BF16), preferred_element_type=F32))
            den = (jnp.sum(wqk, axis=-1, keepdims=True)
                   + a * jnp.sum(q.astype(F32) * n_row, axis=-1, keepdims=True))
            hval = num / jnp.maximum(jnp.abs(den), jnp.exp(-m_t))
            out_ref[0, :, h * B_HD:(h + 1) * B_HD] = hval.astype(BF16)
            g_col = b_last - f_col + i_col
            g_row = b_last - f_row + i_row
            m_new = jnp.maximum(b_last + m_prev, jnp.max(g_row, axis=-1, keepdims=True))
            decay = jnp.exp(b_last + m_prev - m_new)
            ws = jnp.exp(g_col - m_new)
            vs = (v.astype(F32) * ws).astype(BF16)
            c_ref[idx] = decay * ct + _dot_tn(k, vs)
            n_ref[idx] = decay * n_row + jnp.sum(k.astype(F32) * ws, axis=0, keepdims=True)
            m_ref[idx] = jnp.broadcast_to(m_new, (1, LANES))


def _mlstm(q, k, v, g, gt, ctx_len):
    b, tt, w = q.shape
    L = B_CHUNK
    nc = tt // L
    cc = ctx_len // L

    nl = nc - cc

    def fwd(s):
        return jnp.where(s < cc, nl + s, s - cc)

    def bwd(s):
        return nc - 1 - s

    def specs(f):
        tok = pl.BlockSpec((1, L, w), lambda bi, s: (bi, f(s), 0))
        return [tok, tok, tok,
                pl.BlockSpec((1, L, LANES), lambda bi, s: (bi, f(s), 0)),
                pl.BlockSpec((1, 16, L), lambda bi, s: (bi, 0, f(s)))]

    out = jax.ShapeDtypeStruct((b, tt, w), BF16)
    return pl.pallas_call(
        _mlstm_kernel,
        out_shape=[out, out],
        grid=(b, nc),
        in_specs=specs(fwd) + specs(bwd),
        out_specs=[pl.BlockSpec((1, L, w), lambda bi, s: (bi, fwd(s), 0)),
                   pl.BlockSpec((1, L, w), lambda bi, s: (bi, bwd(s), 0))],
        scratch_shapes=[pltpu.VMEM((2 * B_HEADS, B_HD, B_HD), F32),
                        pltpu.VMEM((2 * B_HEADS, 1, B_HD), F32),
                        pltpu.VMEM((2 * B_HEADS, 1, LANES), F32)],
        compiler_params=pltpu.CompilerParams(
            dimension_semantics=("parallel", "arbitrary"), vmem_limit_bytes=VMEM_LIMIT),
        name="mlstm",
    )(q, k, v, g, gt, q, k, v, g, gt)


def _softmax_update(q, k, v_ext, m_ref, acc_ref):
    s = _dot_nt(q, k)
    m_old = m_ref[...]
    m_new = jnp.maximum(m_old, jnp.max(s, axis=-1, keepdims=True))
    p = jnp.exp2(s - jnp.tile(m_new, (1, s.shape[1] // LANES)))
    alpha = jnp.exp2(m_old - m_new)
    acc_ref[...] = (jnp.tile(alpha, (1, 2)) * acc_ref[...]
                    + jnp.dot(p.astype(BF16), v_ext, preferred_element_type=F32))
    m_ref[...] = m_new


def _attend(qs, stats, k_ref, v_ref):
    sub = min(FLASH_SUB, k_ref.shape[1])
    for c in range(k_ref.shape[1] // sub):
        k = k_ref[0, c * sub:(c + 1) * sub, :]
        v = v_ref[0, c * sub:(c + 1) * sub, :]
        v_ext = jnp.concatenate([v, jnp.ones_like(v)], axis=1)
        for q, (m_ref, acc_ref) in zip(qs, stats):
            _softmax_update(q, k, v_ext, m_ref, acc_ref)


def _flash_init(stats):
    for m_ref, acc_ref in stats:
        m_ref[...] = jnp.full_like(m_ref, NEG_BIG)
        acc_ref[...] = jnp.zeros_like(acc_ref)


def _normalized(acc_ref):
    return acc_ref[:, :LANES] / acc_ref[:, LANES:]


def _mla_queries(q):
    return [q]


def _mla_finish(stats):
    return _normalized(stats[0][1])


def _diff_queries(q):
    low = lax.broadcasted_iota(jnp.int32, q.shape, 1) < D_HD
    zero = jnp.zeros_like(q)
    return [jnp.where(low, q, zero), jnp.where(low, zero, q)]


def _diff_finish(stats, lam_ref, gsub_ref, *, lam_init):
    lq1, lk1, lq2, lk2 = (lam_ref[r:r + 1, :] for r in range(4))
    lam = (jnp.exp(jnp.sum(lq1 * lk1, axis=-1, keepdims=True))
           - jnp.exp(jnp.sum(lq2 * lk2, axis=-1, keepdims=True)) + lam_init)
    o = _normalized(stats[0][1]) - lam * _normalized(stats[1][1])
    return _rms(o, gsub_ref[...]) * (1.0 - lam_init)


def _flash_kernel(*refs, n_extra, n_maps, queries, finish, latent):
    extra, refs = refs[:n_extra], refs[n_extra:]
    if latent:
        q_ref, kc_ref, vc_ref, k_ref, v_ref, o_ref = refs[:6]
        scratch = refs[6:]
    else:
        q_ref, kc_ref, vc_ref, _, o_ref = refs[:5]
        scratch = refs[5:]
    stats = [(scratch[2 * i], scratch[2 * i + 1]) for i in range(n_maps)]
    qs = queries(q_ref[0])
    if not latent:
        _flash_init(stats)
        _attend(qs, stats, kc_ref, vc_ref)
        o_ref[0] = finish(stats, *extra).astype(BF16)
        return
    j = pl.program_id(3)

    @pl.when(j == 0)
    def _():
        _flash_init(stats)
        _attend(qs, stats, kc_ref, vc_ref)

    _attend(qs, stats, k_ref, v_ref)

    @pl.when(j == pl.num_programs(3) - 1)
    def _():
        o_ref[0] = finish(stats, *extra).astype(BF16)


def _full_attention(q, k, v, seq, heads, qk_width, n_maps, queries, finish, extra, ctx_queries, name):
    b, tt, _ = q.shape
    ctx_len = tt - seq
    cb = seq // ctx_len
    tq, tk = min(FLASH_TQ, seq), min(FLASH_TK, seq)
    out_shape = jax.ShapeDtypeStruct((b, tt, heads * LANES), BF16)

    def scratch(rows):
        return [pltpu.VMEM((rows, LANES), F32), pltpu.VMEM((rows, 2 * LANES), F32)] * n_maps

    body = functools.partial(_flash_kernel, n_extra=len(extra), n_maps=n_maps, queries=queries, finish=finish)
    whole4 = [pl.BlockSpec(e.shape, lambda bi, h, i, j: (0, 0)) for e in extra]
    y = pl.pallas_call(
        functools.partial(body, latent=True),
        out_shape=out_shape,
        grid=(b, heads, seq // tq, seq // tk),
        in_specs=whole4 + [
            pl.BlockSpec((1, tq, qk_width), lambda bi, h, i, j: (bi, i, h)),
            pl.BlockSpec((1, ctx_len, qk_width), lambda bi, h, i, j: (bi, cb, h)),
            pl.BlockSpec((1, ctx_len, LANES), lambda bi, h, i, j: (bi, cb, h)),
            pl.BlockSpec((1, tk, qk_width), lambda bi, h, i, j: (bi, j, h)),
            pl.BlockSpec((1, tk, LANES), lambda bi, h, i, j: (bi, j, h))],
        out_specs=pl.BlockSpec((1, tq, LANES), lambda bi, h, i, j: (bi, i, h)),
        scratch_shapes=scratch(tq),
        compiler_params=pltpu.CompilerParams(
            dimension_semantics=("parallel", "parallel", "parallel", "arbitrary"),
            vmem_limit_bytes=VMEM_LIMIT),
        name=name,
    )(*extra, q, k, v, k, v)
    if not ctx_queries:
        return y
    whole2 = [pl.BlockSpec(e.shape, lambda bi, h: (0, 0)) for e in extra]
    return pl.pallas_call(
        functools.partial(body, latent=False),
        out_shape=out_shape,
        grid=(b, heads),
        in_specs=whole2 + [
            pl.BlockSpec((1, ctx_len, qk_width), lambda bi, h: (bi, cb, h)),
            pl.BlockSpec((1, ctx_len, qk_width), lambda bi, h: (bi, cb, h)),
            pl.BlockSpec((1, ctx_len, LANES), lambda bi, h: (bi, cb, h)),
            pl.BlockSpec(memory_space=pl.ANY)],
        out_specs=pl.BlockSpec((1, ctx_len, LANES), lambda bi, h: (bi, cb, h)),
        scratch_shapes=scratch(ctx_len),
        input_output_aliases={len(extra) + 3: 0},
        compiler_params=pltpu.CompilerParams(
            dimension_semantics=("parallel", "parallel"), vmem_limit_bytes=VMEM_LIMIT),
        name=name + "_ctx",
    )(*extra, q, k, v, y)


def _mla_attention(q, k, v, seq, ctx_queries):
    return _full_attention(q, k, v, seq, C_HEADS, C_QK_PAD, 1, _mla_queries, _mla_finish, [],
                           ctx_queries, "mla_attention")


def _diff_attention(q, k, v, lam_vecs, gsub, lam_init, seq, ctx_queries):
    return _full_attention(q, k, v, seq, D_HEADS, 2 * D_HD, 2, _diff_queries,
                           functools.partial(_diff_finish, lam_init=lam_init), [lam_vecs, gsub],
                           ctx_queries, "diff_attention")


def _merge_kernel(ya_ref, hf_ref, hb_ref, bo_ref, yc_ref, yd_ref, gate_ref, wbr_ref, wout_ref,
                  h_ref, mod_ref, gpost_ref, o_ref):
    d = h_ref.shape[2]
    yb = (_sigmoid(bo_ref[0].astype(F32))
          * (hf_ref[0].astype(F32) + hb_ref[0].astype(F32))).astype(BF16)
    ys = (ya_ref[0], yb, yc_ref[0], yd_ref[0])
    merged = None
    for i, y in enumerate(ys):
        g = _sigmoid(gate_ref[0, :, i * d:(i + 1) * d].astype(F32))
        term = g * jnp.dot(y, wbr_ref[i], preferred_element_type=F32)
        merged = term if merged is None else merged + term
    out = jnp.dot(merged.astype(BF16), wout_ref[...], preferred_element_type=F32)
    o_ref[0] = h_ref[0] + mod_ref[0, 2:3, :] * _rms(out, gpost_ref[...])


def _merge(p, ya, hf, hb, yc, yd, wbr, wout, hh, modall, gpost, seq, rows):
    b, _, d = hh.shape
    tp = TOKEN_TILE
    tok = lambda width: pl.BlockSpec((1, tp, width), lambda bi, t: (bi, t, 0))
    return pl.pallas_call(
        _merge_kernel,
        out_shape=jax.ShapeDtypeStruct((b, rows, d), F32),
        grid=(b, rows // tp),
        in_specs=[tok(BRANCH_W)] * 6 + [tok(N_BRANCH * d), _resident(wbr.shape), _resident(wout.shape),
                  tok(d),
                  _mod_spec(d, seq // tp),
                  _resident((1, d))],
        out_specs=tok(d),
        compiler_params=pltpu.CompilerParams(
            dimension_semantics=("parallel", "parallel"), vmem_limit_bytes=VMEM_LIMIT),
        name="merge",
    )(ya, hf, hb, p['bo'], yc, yd, p['gate'], wbr, wout, hh, modall, gpost)


def _ffn_kernel(h_ref, mod_ref, gpre_ref, gpost_ref, wg_ref, wu_ref, wd_ref, o_ref):
    x = h_ref[0]
    u = (_rms(x, gpre_ref[...]) * (1.0 + mod_ref[0, 4:5, :]) + mod_ref[0, 3:4, :]).astype(BF16)
    gate = jnp.dot(u, wg_ref[...], preferred_element_type=F32)
    up = jnp.dot(u, wu_ref[...], preferred_element_type=F32)
    act = (gate * _sigmoid(gate) * up).astype(BF16)
    out = jnp.dot(act, wd_ref[...], preferred_element_type=F32)
    o_ref[0] = x + mod_ref[0, 5:6, :] * _rms(out, gpost_ref[...])


def _ffn(hh, modall, gpre, gpost, wg, wu, wd, seq):
    b, rows, d = hh.shape
    tp = TOKEN_TILE
    tok = pl.BlockSpec((1, tp, d), lambda bi, t: (bi, t, 0))
    return pl.pallas_call(
        _ffn_kernel,
        out_shape=jax.ShapeDtypeStruct(hh.shape, F32),
        grid=(b, rows // tp),
        in_specs=[tok, _mod_spec(d, seq // tp),
                  _resident((1, d)), _resident((1, d)),
                  _resident(wg.shape), _resident(wu.shape), _resident(wd.shape)],
        out_specs=tok,
        compiler_params=pltpu.CompilerParams(
            dimension_semantics=("parallel", "parallel"), vmem_limit_bytes=VMEM_LIMIT),
        name="ffn",
    )(hh, modall, gpre, gpost, wg, wu, wd)


def _rope_tables(n_rows, ctx_len):
    row = jnp.repeat(jnp.arange(n_rows, dtype=F32), GRID_W)
    col = jnp.tile(jnp.arange(GRID_W, dtype=F32), n_rows)
    quarter = ROPE_DIM // 4
    inv_freq = ROPE_BASE ** (-jnp.arange(quarter, dtype=F32) / quarter)
    ang_r, ang_c = row[:, None] * inv_freq, col[:, None] * inv_freq
    ang = jnp.concatenate([ang_r, ang_r, ang_c, ang_c], axis=-1)
    cos, sin = jnp.cos(ang), jnp.sin(ang)
    cos = jnp.concatenate([cos, jnp.ones((ctx_len, ROPE_DIM), F32)], axis=0)
    sin = jnp.concatenate([sin, jnp.zeros((ctx_len, ROPE_DIM), F32)], axis=0)
    cos, sin = jnp.tile(cos, (1, LANES // ROPE_DIM)), jnp.tile(sin, (1, LANES // ROPE_DIM))
    first = (jnp.arange(LANES) % (ROPE_DIM // 2)) < quarter
    return cos, jnp.where(first, -sin, 0.0), jnp.where(first, 0.0, sin)


def _a_head_perm():
    idx = []
    for j in range(A_HEADS // 2):
        for half in range(2):
            head = j + (A_HEADS // 2) * half
            idx.extend(range(head * A_HD, (head + 1) * A_HD))
    return np.asarray(idx)


def _prep_layer(w_in, b_gate_bias, c_g_q, c_g_kv, c_w_uq, c_w_uk, c_w_uv, w_branch):
    d = w_in.shape[0]
    splits = (512, 128, 128, 512, 512, 512, 16, 512, 512, 256, 64, 512, 512, 512, N_BRANCH * d)
    offs = np.concatenate([[0], np.cumsum(splits)])
    part = [w_in[:, offs[i]:offs[i + 1]] for i in range(len(splits))]
    (a_q, a_k, a_v, b_q, b_k, b_v, b_g, b_o, c_q, c_kv, c_kr, d_q, d_k, d_v, gate) = part
    perm = _a_head_perm()
    zeros = lambda n: jnp.zeros((d, n), w_in.dtype)
    uq = c_w_uq.reshape(C_Q_LORA, C_HEADS, C_NOPE + C_ROPE)
    uq = jnp.pad(uq, ((0, 0), (0, 0), (0, C_QK_PAD - C_NOPE - C_ROPE))).reshape(C_Q_LORA, C_HEADS * C_QK_PAD)
    w = {
        'wa': jnp.concatenate([a_q[:, perm], a_k, a_v], axis=1).astype(BF16),
        'wb': jnp.concatenate([b_q, b_k, b_v, b_o], axis=1).astype(BF16),
        'wg': jnp.concatenate([b_g, zeros(LANES - 16)], axis=1).astype(BF16),
        'wgt': b_g.T.astype(BF16),
        'wc': jnp.concatenate([c_q, c_kv, c_kr, zeros(LANES - C_ROPE)], axis=1).astype(BF16),
        'wd': jnp.concatenate([d_q, d_k, d_v], axis=1).astype(BF16),
        'wgate': gate.astype(BF16),
        'gbias': jnp.pad(b_gate_bias, (0, LANES - 16)).reshape(1, LANES),
        'gbiast': b_gate_bias.reshape(16, 1),
        'cgq': c_g_q.reshape(1, -1),
        'cgkv': c_g_kv.reshape(1, -1),
        'wuq': uq.astype(BF16),
        'wukv': jnp.concatenate([c_w_uk, c_w_uv], axis=1).astype(BF16),
    }
    wbr = jnp.concatenate([w_branch[0:1, perm], w_branch[1:]], axis=0).astype(BF16)
    return w, wbr


def kernel(x, c, ctx, c_ctx, w_mod, b_mod, g_mix_pre, g_mix_post, g_ffn_pre, g_ffn_post, w_in, a_sink,
           b_gate_bias, c_g_q, c_g_kv, c_w_uq, c_w_uk, c_w_uv, d_lam_q1, d_lam_k1, d_lam_q2, d_lam_k2,
           d_g_sub, w_branch, w_out, w_gate_up, w_down):
    b, seq, d = x.shape
    ctx_len = ctx.shape[1]
    depth = w_in.shape[0]
    tables = _rope_tables(seq // GRID_W, ctx_len)

    rows = jnp.concatenate([c, c_ctx[None, :], jnp.zeros((8 - b - 1, d), F32)], axis=0)
    mods = _modulation(rows, w_mod, b_mod).reshape(depth, 8, 6, d)

    assert ctx_len == TOKEN_TILE == WINDOW_TILE, "the context must be exactly one token tile"
    assert seq % min(FLASH_TQ, seq) == 0 and seq % min(FLASH_TK, seq) == 0 and seq % ctx_len == 0
    hh = jnp.concatenate([x, ctx], axis=1)
    for l in range(depth):
        with_ctx = l < depth - 1
        modall = jnp.stack([jnp.broadcast_to(mods[l, b], (b, 6, d)), mods[l, :b]], axis=1).reshape(2 * b, 6, d)
        w, wbr = _prep_layer(w_in[l], b_gate_bias[l], c_g_q[l], c_g_kv[l], c_w_uq[l], c_w_uk[l],
                             c_w_uv[l], w_branch[l])
        lam_init = 0.8 - 0.6 * math.exp(-0.3 * l)
        lam_vecs = jnp.stack([d_lam_q1[l], d_lam_k1[l], d_lam_q2[l], d_lam_k2[l]], axis=0)
        hidden = w_down.shape[1]

        p = _project(hh, modall, g_mix_pre[l].reshape(1, d), tables, w)
        ya = _window_attention(p['aq'], p['ak'], p['av'], a_sink[l], ctx_len)
        hf, hb = _mlstm(p['bq'], p['bk'], p['bv'], p['bg'], p['bgt'], ctx_len)
        yc = _mla_attention(p['cq'], p['ck'], p['cv'], seq, with_ctx)
        yd = _diff_attention(p['dq'], p['dk'], p['dv'], lam_vecs, d_g_sub[l].reshape(1, -1), lam_init,
                             seq, with_ctx)
        rows = seq + ctx_len if with_ctx else seq
        hh = _merge(p, ya, hf, hb, yc, yd, wbr, w_out[l].astype(BF16), hh, modall,
                    g_mix_post[l].reshape(1, d), seq, rows)
        hh = _ffn(hh, modall, g_ffn_pre[l].reshape(1, d), g_ffn_post[l].reshape(1, d),
                  w_gate_up[l, :, :hidden].astype(BF16), w_gate_up[l, :, hidden:].astype(BF16),
                  w_down[l].astype(BF16), seq)
    return hh
```

```python
import functools
import math

import numpy as np
import jax
import jax.numpy as jnp
from jax import lax
from jax.experimental import pallas as pl
from jax.experimental.pallas import tpu as pltpu

F32 = jnp.float32
BF16 = jnp.bfloat16

GRID_W = 64
ROPE_BASE = 10000.0
NORM_EPS = 1e-6
A_HEADS, A_KV_HEADS, A_HD, A_WINDOW = 8, 2, 64, 128
B_HEADS, B_HD, B_CHUNK = 4, 128, 128
C_HEADS, C_Q_LORA, C_KV_LORA, C_NOPE, C_ROPE, C_VD = 4, 512, 256, 128, 64, 128
D_HEADS, D_HD, D_VD = 4, 64, 128
N_BRANCH = 4
BRANCH_W = 512
ROPE_DIM = 64
C_QK_PAD = 256

LANES = 128
NEG_BIG = -1e30
VMEM_LIMIT = 56 * 1024 * 1024

TOKEN_TILE = 256
WINDOW_TILE = 256
FLASH_TQ = 2048
MLA_TK = 4096
DIFF_TK = 2048
FLASH_SUB = 256
LOG2E = math.log2(math.e)


def _resident(shape):
    nd = len(shape)
    return pl.BlockSpec(shape, lambda *_: (0,) * nd, pipeline_mode=pl.Buffered(1))


def _mod_spec(d, ctx_tile):
    return pl.BlockSpec((1, 6, d), lambda bi, t: (bi * 2 + jnp.where(t == ctx_tile, 0, 1), 0, 0))


def _sigmoid(x):
    return 1.0 / (1.0 + jnp.exp(-x))


def _log_sigmoid(x):
    return jnp.minimum(x, 0.0) - jnp.log(1.0 + jnp.exp(-jnp.abs(x)))


def _rms(x, g):
    return x * lax.rsqrt(jnp.mean(x * x, axis=-1, keepdims=True) + NORM_EPS) * g


def _rope(x, cos, sin_lo, sin_hi):
    return x * cos + pltpu.roll(x, LANES - 16, 1) * sin_lo + pltpu.roll(x, 16, 1) * sin_hi


def _dot_nt(a, b):
    return lax.dot_general(a, b, (((1,), (1,)), ((), ())), preferred_element_type=F32)


def _dot_tn(a, b):
    return lax.dot_general(a, b, (((0,), (0,)), ((), ())), preferred_element_type=F32)


def _mod_kernel(c_ref, w_ref, b_ref, o_ref):
    c = c_ref[...]
    s = c * _sigmoid(c)
    o_ref[0] = jnp.dot(s, w_ref[0], preferred_element_type=F32,
                       precision=lax.Precision.HIGHEST) + b_ref[0]


def _modulation(rows, w_mod, b_mod):
    depth, d, n = w_mod.shape
    tn = 1024
    return pl.pallas_call(
        _mod_kernel,
        out_shape=jax.ShapeDtypeStruct((depth, rows.shape[0], n), F32),
        grid=(depth, n // tn),
        in_specs=[pl.BlockSpec(rows.shape, lambda l, j: (0, 0)),
                  pl.BlockSpec((1, d, tn), lambda l, j: (l, 0, j)),
                  pl.BlockSpec((1, 1, tn), lambda l, j: (l, 0, j))],
        out_specs=pl.BlockSpec((1, rows.shape[0], tn), lambda l, j: (l, 0, j)),
        name="modulation",
    )(rows, w_mod, b_mod.reshape(depth, 1, n))


def _proj_kernel(x_ref, mod_ref, gpre_ref, cos_ref, slo_ref, shi_ref,
                 wa_ref, wb_ref, wgt_ref, wc_ref, wd_ref, wgate_ref,
                 gbiast_ref, cgq_ref, cgkv_ref, wuq_ref, wukv_ref,
                 aq_ref, ak_ref, av_ref, bq_ref, bk_ref, bv_ref, bo_ref, bgt_ref,
                 cq_ref, ck_ref, cv_ref, dq_ref, dk_ref, dv_ref, gate_ref):
    x = x_ref[0]
    y = _rms(x, gpre_ref[...])
    u = (y * (1.0 + mod_ref[0, 1:2, :]) + mod_ref[0, 0:1, :]).astype(BF16)
    cos, slo, shi = cos_ref[...], slo_ref[...], shi_ref[...]

    def rope(v):
        return _rope(v, cos, slo, shi)

    def chunks(v, n):
        return [v[:, i * LANES:(i + 1) * LANES] for i in range(n)]

    pa = jnp.dot(u, wa_ref[...], preferred_element_type=F32)
    for i, v in enumerate(chunks(pa, 4)):
        aq_ref[0, :, i * LANES:(i + 1) * LANES] = (rope(v) * (A_HD ** -0.5 * LOG2E)).astype(BF16)
    ak_ref[0] = rope(pa[:, 512:640]).astype(BF16)
    av_ref[0] = pa[:, 640:768].astype(BF16)

    pb = jnp.dot(u, wb_ref[...], preferred_element_type=F32)
    bq_ref[0] = pb[:, 0:512].astype(BF16)
    bk_ref[0] = (pb[:, 512:1024] * (B_HD ** -0.5)).astype(BF16)
    bv_ref[0] = pb[:, 1024:1536].astype(BF16)
    bo_ref[0] = pb[:, 1536:2048].astype(BF16)
    bgt_ref[0] = _dot_nt(wgt_ref[...], u) + gbiast_ref[...]

    pc = jnp.dot(u, wc_ref[...], preferred_element_type=F32)
    cqn = _rms(pc[:, 0:C_Q_LORA], cgq_ref[...]).astype(BF16)
    ckvn = _rms(pc[:, C_Q_LORA:C_Q_LORA + C_KV_LORA], cgkv_ref[...]).astype(BF16)
    kr = rope(pc[:, 768:896]).astype(BF16)
    cq = jnp.dot(cqn, wuq_ref[...], preferred_element_type=F32)
    ckv = jnp.dot(ckvn, wukv_ref[...], preferred_element_type=F32)
    c_scale = (C_NOPE + C_ROPE) ** -0.5 * LOG2E
    for h in range(C_HEADS):
        o = h * C_QK_PAD
        cq_ref[0, :, o:o + LANES] = (cq[:, o:o + LANES] * c_scale).astype(BF16)
        cq_ref[0, :, o + LANES:o + 2 * LANES] = (rope(cq[:, o + LANES:o + 2 * LANES]) * c_scale).astype(BF16)
        ck_ref[0, :, o:o + LANES] = ckv[:, h * LANES:(h + 1) * LANES].astype(BF16)
        ck_ref[0, :, o + LANES:o + 2 * LANES] = kr
    cv_ref[0] = ckv[:, 512:1024].astype(BF16)

    pd = jnp.dot(u, wd_ref[...], preferred_element_type=F32)
    for i in range(4):
        sl = slice(i * LANES, (i + 1) * LANES)
        dq_ref[0, :, sl] = (rope(pd[:, sl]) * (D_HD ** -0.5 * LOG2E)).astype(BF16)
        dk_ref[0, :, sl] = rope(pd[:, 512 + i * LANES:512 + (i + 1) * LANES]).astype(BF16)
    dv_ref[0] = pd[:, 1024:1536].astype(BF16)

    for i in range(N_BRANCH):
        sl = slice(i * 1024, (i + 1) * 1024)
        gate_ref[0, :, sl] = jnp.dot(u, wgate_ref[:, sl], preferred_element_type=F32).astype(BF16)


def _project(hh, modall, gpre, tables, w):
    b, tt, d = hh.shape
    tp = TOKEN_TILE
    nt = tt // tp
    cos, slo, shi = tables

    def tok(width):
        return pl.BlockSpec((1, tp, width), lambda bi, t: (bi, t, 0))

    tab = pl.BlockSpec((tp, LANES), lambda bi, t: (t, 0))
    in_specs = [
        tok(d),
        _mod_spec(d, nt - 1),
        _resident((1, d)), tab, tab, tab,
        _resident(w['wa'].shape), _resident(w['wb'].shape),
        _resident(w['wgt'].shape), _resident(w['wc'].shape), _resident(w['wd'].shape),
        _resident(w['wgate'].shape), _resident(w['gbiast'].shape),
        _resident(w['cgq'].shape), _resident(w['cgkv'].shape), _resident(w['wuq'].shape),
        _resident(w['wukv'].shape),
    ]
    widths = dict(aq=512, ak=128, av=128, bq=512, bk=512, bv=512, bo=512,
                  cq=C_HEADS * C_QK_PAD, ck=C_HEADS * C_QK_PAD, cv=512,
                  dq=512, dk=512, dv=512, gate=N_BRANCH * d)
    names = ['aq', 'ak', 'av', 'bq', 'bk', 'bv', 'bo', 'bgt',
             'cq', 'ck', 'cv', 'dq', 'dk', 'dv', 'gate']
    out_shape, out_specs = [], []
    for n in names:
        if n == 'bgt':
            out_shape.append(jax.ShapeDtypeStruct((b, 16, tt), F32))
            out_specs.append(pl.BlockSpec((1, 16, tp), lambda bi, t: (bi, 0, t)))
        else:
            out_shape.append(jax.ShapeDtypeStruct((b, tt, widths[n]), BF16))
            out_specs.append(tok(widths[n]))
    outs = pl.pallas_call(
        _proj_kernel,
        out_shape=out_shape,
        grid=(b, nt),
        in_specs=in_specs,
        out_specs=out_specs,
        compiler_params=pltpu.CompilerParams(
            dimension_semantics=("parallel", "parallel"), vmem_limit_bytes=VMEM_LIMIT),
        name="project",
    )(hh, modall, gpre, cos, slo, shi, w['wa'], w['wb'], w['wgt'], w['wc'], w['wd'],
      w['wgate'], w['gbiast'], w['cgq'], w['cgkv'], w['wuq'], w['wukv'])
    return dict(zip(names, outs))


def _window_kernel(sink_ref, q_ref, kc_ref, kp_ref, kk_ref, kn_ref,
                   vc_ref, vp_ref, vk_ref, vn_ref, o_ref, *, lat_tiles):
    t = pl.program_id(1)
    hb = WINDOW_TILE // 2
    n_ctx = kc_ref.shape[1]
    group = A_HEADS // A_KV_HEADS
    width = n_ctx + 3 * hb
    kwin = jnp.concatenate([kp_ref[0], kk_ref[0], kn_ref[0]], axis=0)
    vwin = jnp.concatenate([vp_ref[0], vk_ref[0], vn_ref[0]], axis=0)
    prev_ok = jnp.logical_and(t >= 1, t < lat_tiles).astype(jnp.int32)
    cur_ok = (t < lat_tiles).astype(jnp.int32)
    next_ok = (t < lat_tiles - 1).astype(jnp.int32)
    row = lax.broadcasted_iota(jnp.int32, (group * hb, width), 0) & (hb - 1)
    col = lax.broadcasted_iota(jnp.int32, (group * hb, width), 1)
    wcol = col - n_ctx
    near = jnp.abs(wcol - hb - row) <= A_WINDOW
    low = lax.broadcasted_iota(jnp.int32, (hb, LANES), 1) < A_HD
    for blk in range(2):
        k0 = hb * (1 + blk)
        kb = jnp.concatenate([kc_ref[0], kwin[k0:k0 + 3 * hb]], axis=0)
        vb = jnp.concatenate([vc_ref[0], vwin[k0:k0 + 3 * hb]], axis=0)
        vb_ext = jnp.concatenate([vb, jnp.ones_like(vb)], axis=1)
        if blk == 0:
            seg_ok = jnp.where(wcol < hb, prev_ok, cur_ok)
        else:
            seg_ok = jnp.where(wcol < 2 * hb, cur_ok, next_ok)
        valid = jnp.logical_or(col < n_ctx, jnp.logical_and(near, seg_ok > 0))
        rows = slice(blk * hb, (blk + 1) * hb)
        outs = []
        for g in range(A_KV_HEADS):
            keep = low if g == 0 else jnp.logical_not(low)
            qs, sinks = [], []
            for j in range(group):
                qb = q_ref[0, rows, j * LANES:(j + 1) * LANES]
                qs.append(jnp.where(keep, qb, jnp.zeros_like(qb)))
                sinks.append(jnp.full((hb, LANES), sink_ref[j + group * g] * LOG2E, F32))
            sink = jnp.concatenate(sinks, axis=0)
            s = jnp.where(valid, _dot_nt(jnp.concatenate(qs, axis=0), kb), NEG_BIG)
            m = jnp.maximum(jnp.max(s, axis=-1, keepdims=True), sink)
            e = jnp.exp2(s - jnp.tile(m, (1, width // LANES)))
            pv = jnp.dot(e.astype(BF16), vb_ext, preferred_element_type=F32)
            outs.append(pv[:, :LANES] / (pv[:, LANES:] + jnp.exp2(sink - m)))
        for j in range(group):
            o_ref[0, rows, j * LANES:(j + 1) * LANES] = jnp.where(
                low, outs[0][j * hb:(j + 1) * hb], outs[1][j * hb:(j + 1) * hb]).astype(BF16)


def _window_attention(q, k, v, sink, ctx_len):
    b, tt, _ = q.shape
    tw = WINDOW_TILE
    assert ctx_len == tw == 2 * A_WINDOW, "context is one key tile; a query block is one window radius"
    nt = tt // tw
    nl = nt - 1
    kv = lambda f: pl.BlockSpec((1, tw, LANES), lambda bi, t: (bi, f(t), 0))
    specs = [kv(lambda t: nl), kv(lambda t: jnp.maximum(t - 1, 0)), kv(lambda t: t),
             kv(lambda t: jnp.minimum(t + 1, nl - 1))]
    return pl.pallas_call(
        functools.partial(_window_kernel, lat_tiles=nl),
        out_shape=jax.ShapeDtypeStruct((b, tt, 512), BF16),
        grid=(b, nt),
        in_specs=[pl.BlockSpec(memory_space=pltpu.SMEM),
                  pl.BlockSpec((1, tw, 512), lambda bi, t: (bi, t, 0))] + specs + specs,
        out_specs=pl.BlockSpec((1, tw, 512), lambda bi, t: (bi, t, 0)),
        compiler_params=pltpu.CompilerParams(
            dimension_semantics=("parallel", "parallel"), vmem_limit_bytes=VMEM_LIMIT),
        name="window_attention",
    )(sink, q, k, k, k, k, v, v, v, v)


def _mlstm_kernel(qf_ref, kf_ref, vf_ref, gtf_ref, qb_ref, kb_ref, vb_ref, gtb_ref,
                  hf_ref, hb_ref, c_ref, m_ref):
    step = pl.program_id(1)

    @pl.when(step == 0)
    def _():
        c_ref[...] = jnp.zeros_like(c_ref)
        m_ref[...] = jnp.zeros_like(m_ref)

    L = B_CHUNK
    row = lax.broadcasted_iota(jnp.int32, (L, L), 0)
    col = lax.broadcasted_iota(jnp.int32, (L, L), 1)
    dirs = ((qf_ref, kf_ref, vf_ref, gtf_ref, hf_ref), (qb_ref, kb_ref, vb_ref, gtb_ref, hb_ref))
    for d, (q_ref, k_ref, v_ref, gt_ref, out_ref) in enumerate(dirs):
        before = (col <= row) if d == 0 else (col >= row)
        cum = before.astype(F32)
        cum_t = ((row <= col) if d == 0 else (row >= col)).astype(F32)
        last = L - 1 if d == 0 else 0
        gt = gt_ref[0]
        lf_all = _log_sigmoid(gt)
        frow_all = jnp.dot(lf_all, cum_t, preferred_element_type=F32, precision=lax.Precision.HIGHEST)
        for h in range(B_HEADS):
            idx = d * B_HEADS + h
            ci, cf = (2 * d) * B_HEADS + h, (2 * d + 1) * B_HEADS + h
            i_row, lf_row, f_row = gt[ci:ci + 1, :], lf_all[cf:cf + 1, :], frow_all[cf:cf + 1, :]
            f_col = jnp.sum(cum * lf_row, axis=-1, keepdims=True)
            a_row = i_row - f_row
            b_last = f_row[:, last:last + 1]
            m_prev = m_ref[idx][:, 0:1]
            q = q_ref[0, :, h * B_HD:(h + 1) * B_HD]
            k = k_ref[0, :, h * B_HD:(h + 1) * B_HD]
            v = v_ref[0, :, h * B_HD:(h + 1) * B_HD]
            v_ext = jnp.concatenate([v, jnp.ones_like(v)], axis=1)
            dmat = jnp.where(before, f_col + a_row, -jnp.inf)
            m_t = jnp.maximum(f_col + m_prev, jnp.max(dmat, axis=-1, keepdims=True))
            wqk = jnp.exp(dmat - m_t) * _dot_nt(q, k)
            a = jnp.exp(f_col + m_prev - m_t)
            ct = c_ref[idx]
            num_den = (jnp.dot(wqk.astype(BF16), v_ext, preferred_element_type=F32)
                       + a * jnp.dot(q, ct.astype(BF16), preferred_element_type=F32))
            hval = num_den[:, :B_HD] / jnp.maximum(jnp.abs(num_den[:, B_HD:]), jnp.exp(-m_t))
            out_ref[0, :, h * B_HD:(h + 1) * B_HD] = hval.astype(BF16)
            g_row = b_last + a_row
            m_new = jnp.maximum(b_last + m_prev, jnp.max(g_row, axis=-1, keepdims=True))
            decay = jnp.exp(b_last + m_prev - m_new)
            kts = (k.astype(F32).T * jnp.exp(g_row - m_new)).astype(BF16)
            c_ref[idx] = decay * ct + jnp.dot(kts, v_ext, preferred_element_type=F32)
            m_ref[idx] = jnp.broadcast_to(m_new, (1, LANES))


def _mlstm(q, k, v, gt, ctx_len):
    b, tt, w = q.shape
    L = B_CHUNK
    nc = tt // L
    cc = ctx_len // L

    nl = nc - cc

    def fwd(s):
        return jnp.where(s < cc, nl + s, s - cc)

    def bwd(s):
        return nc - 1 - s

    def specs(f):
        tok = pl.BlockSpec((1, L, w), lambda bi, s: (bi, f(s), 0))
        return [tok, tok, tok, pl.BlockSpec((1, 16, L), lambda bi, s: (bi, 0, f(s)))]

    out = jax.ShapeDtypeStruct((b, tt, w), BF16)
    return pl.pallas_call(
        _mlstm_kernel,
        out_shape=[out, out],
        grid=(b, nc),
        in_specs=specs(fwd) + specs(bwd),
        out_specs=[pl.BlockSpec((1, L, w), lambda bi, s: (bi, fwd(s), 0)),
                   pl.BlockSpec((1, L, w), lambda bi, s: (bi, bwd(s), 0))],
        scratch_shapes=[pltpu.VMEM((2 * B_HEADS, B_HD, 2 * B_HD), F32),
                        pltpu.VMEM((2 * B_HEADS, 1, LANES), F32)],
        compiler_params=pltpu.CompilerParams(
            dimension_semantics=("parallel", "arbitrary"), vmem_limit_bytes=VMEM_LIMIT),
        name="mlstm",
    )(q, k, v, gt, q, k, v, gt)


def _softmax_update(q, k, v_ext, m_old, acc):
    s = _dot_nt(q, k)
    m_new = jnp.maximum(m_old, jnp.max(s, axis=-1, keepdims=True))
    p = jnp.exp2(s - jnp.tile(m_new, (1, s.shape[1] // LANES)))
    alpha = jnp.exp2(m_old - m_new)
    acc = jnp.tile(alpha, (1, 2)) * acc + jnp.dot(p.astype(BF16), v_ext, preferred_element_type=F32)
    return m_new, acc


def _attend(qs, stats, k_ref, v_ref):
    vals = [(m_ref[...], acc_ref[...]) for m_ref, acc_ref in stats]
    sub = min(FLASH_SUB, k_ref.shape[1])
    for c in range(k_ref.shape[1] // sub):
        k = k_ref[0, c * sub:(c + 1) * sub, :]
        v = v_ref[0, c * sub:(c + 1) * sub, :]
        v_ext = jnp.concatenate([v, jnp.ones_like(v)], axis=1)
        vals = [_softmax_update(q, k, v_ext, m, acc) for q, (m, acc) in zip(qs, vals)]
    for (m_ref, acc_ref), (m, acc) in zip(stats, vals):
        m_ref[...] = m
        acc_ref[...] = acc


def _flash_init(stats):
    for m_ref, acc_ref in stats:
        m_ref[...] = jnp.full_like(m_ref, NEG_BIG)
        acc_ref[...] = jnp.zeros_like(acc_ref)


def _normalized(acc_ref):
    return acc_ref[:, :LANES] / acc_ref[:, LANES:]


def _mla_queries(q):
    return [q]


def _mla_finish(stats):
    return _normalized(stats[0][1])


def _diff_queries(q):
    low = lax.broadcasted_iota(jnp.int32, q.shape, 1) < D_HD
    zero = jnp.zeros_like(q)
    return [jnp.where(low, q, zero), jnp.where(low, zero, q)]


def _diff_finish(stats, lam_ref, gsub_ref, *, lam_init):
    lq1, lk1, lq2, lk2 = (lam_ref[r:r + 1, :] for r in range(4))
    lam = (jnp.exp(jnp.sum(lq1 * lk1, axis=-1, keepdims=True))
           - jnp.exp(jnp.sum(lq2 * lk2, axis=-1, keepdims=True)) + lam_init)
    o = _normalized(stats[0][1]) - lam * _normalized(stats[1][1])
    return _rms(o, gsub_ref[...]) * (1.0 - lam_init)


def _flash_kernel(*refs, n_extra, n_maps, queries, finish, latent):
    extra, refs = refs[:n_extra], refs[n_extra:]
    if latent:
        q_ref, kc_ref, vc_ref, k_ref, v_ref, o_ref = refs[:6]
        scratch = refs[6:]
    else:
        q_ref, kc_ref, vc_ref, _, o_ref = refs[:5]
        scratch = refs[5:]
    stats = [(scratch[2 * i], scratch[2 * i + 1]) for i in range(n_maps)]
    qs = queries(q_ref[0])
    if not latent:
        _flash_init(stats)
        _attend(qs, stats, kc_ref, vc_ref)
        o_ref[0] = finish(stats, *extra).astype(BF16)
        return
    j = pl.program_id(3)

    @pl.when(j == 0)
    def _():
        _flash_init(stats)
        _attend(qs, stats, kc_ref, vc_ref)

    _attend(qs, stats, k_ref, v_ref)

    @pl.when(j == pl.num_programs(3) - 1)
    def _():
        o_ref[0] = finish(stats, *extra).astype(BF16)


def _full_attention(q, k, v, seq, key_block, heads, qk_width, n_maps, queries, finish, extra, ctx_queries, name):
    b, tt, _ = q.shape
    ctx_len = tt - seq
    cb = seq // ctx_len
    tq, tk = min(FLASH_TQ, seq), min(key_block, seq)
    assert seq % tq == 0 and seq % tk == 0 and seq % ctx_len == 0
    out_shape = jax.ShapeDtypeStruct((b, tt, heads * LANES), BF16)

    def scratch(rows):
        return [pltpu.VMEM((rows, LANES), F32), pltpu.VMEM((rows, 2 * LANES), F32)] * n_maps

    body = functools.partial(_flash_kernel, n_extra=len(extra), n_maps=n_maps, queries=queries, finish=finish)
    whole4 = [pl.BlockSpec(e.shape, lambda bi, h, i, j: (0, 0)) for e in extra]
    y = pl.pallas_call(
        functools.partial(body, latent=True),
        out_shape=out_shape,
        grid=(b, heads, seq // tq, seq // tk),
        in_specs=whole4 + [
            pl.BlockSpec((1, tq, qk_width), lambda bi, h, i, j: (bi, i, h)),
            pl.BlockSpec((1, ctx_len, qk_width), lambda bi, h, i, j: (bi, cb, h)),
            pl.BlockSpec((1, ctx_len, LANES), lambda bi, h, i, j: (bi, cb, h)),
            pl.BlockSpec((1, tk, qk_width), lambda bi, h, i, j: (bi, j, h)),
            pl.BlockSpec((1, tk, LANES), lambda bi, h, i, j: (bi, j, h))],
        out_specs=pl.BlockSpec((1, tq, LANES), lambda bi, h, i, j: (bi, i, h)),
        scratch_shapes=scratch(tq),
        compiler_params=pltpu.CompilerParams(
            dimension_semantics=("parallel", "parallel", "parallel", "arbitrary"),
            vmem_limit_bytes=VMEM_LIMIT),
        name=name,
    )(*extra, q, k, v, k, v)
    if not ctx_queries:
        return y
    whole2 = [pl.BlockSpec(e.shape, lambda bi, h: (0, 0)) for e in extra]
    return pl.pallas_call(
        functools.partial(body, latent=False),
        out_shape=out_shape,
        grid=(b, heads),
        in_specs=whole2 + [
            pl.BlockSpec((1, ctx_len, qk_width), lambda bi, h: (bi, cb, h)),
            pl.BlockSpec((1, ctx_len, qk_width), lambda bi, h: (bi, cb, h)),
            pl.BlockSpec((1, ctx_len, LANES), lambda bi, h: (bi, cb, h)),
            pl.BlockSpec(memory_space=pl.ANY)],
        out_specs=pl.BlockSpec((1, ctx_len, LANES), lambda bi, h: (bi, cb, h)),
        scratch_shapes=scratch(ctx_len),
        input_output_aliases={len(extra) + 3: 0},
        compiler_params=pltpu.CompilerParams(
            dimension_semantics=("parallel", "parallel"), vmem_limit_bytes=VMEM_LIMIT),
        name=name + "_ctx",
    )(*extra, q, k, v, y)


def _mla_attention(q, k, v, seq, ctx_queries):
    return _full_attention(q, k, v, seq, MLA_TK, C_HEADS, C_QK_PAD, 1, _mla_queries, _mla_finish, [],
                           ctx_queries, "mla_attention")


def _diff_attention(q, k, v, lam_vecs, gsub, lam_init, seq, ctx_queries):
    return _full_attention(q, k, v, seq, DIFF_TK, D_HEADS, 2 * D_HD, 2, _diff_queries,
                           functools.partial(_diff_finish, lam_init=lam_init), [lam_vecs, gsub],
                           ctx_queries, "diff_attention")


def _merge_kernel(ya_ref, hf_ref, hb_ref, bo_ref, yc_ref, yd_ref, gate_ref, wbr_ref, wout_ref,
                  h_ref, mod_ref, gpost_ref, o_ref):
    d = h_ref.shape[2]
    yb = (_sigmoid(bo_ref[0].astype(F32))
          * (hf_ref[0].astype(F32) + hb_ref[0].astype(F32))).astype(BF16)
    ys = (ya_ref[0], yb, yc_ref[0], yd_ref[0])
    merged = None
    for i, y in enumerate(ys):
        g = _sigmoid(gate_ref[0, :, i * d:(i + 1) * d].astype(F32))
        term = g * jnp.dot(y, wbr_ref[i], preferred_element_type=F32)
        merged = term if merged is None else merged + term
    out = jnp.dot(merged.astype(BF16), wout_ref[...], preferred_element_type=F32)
    o_ref[0] = h_ref[0] + mod_ref[0, 2:3, :] * _rms(out, gpost_ref[...])


def _merge(p, ya, hf, hb, yc, yd, wbr, wout, hh, modall, gpost, seq, rows):
    b, _, d = hh.shape
    tp = TOKEN_TILE
    tok = lambda width: pl.BlockSpec((1, tp, width), lambda bi, t: (bi, t, 0))
    return pl.pallas_call(
        _merge_kernel,
        out_shape=jax.ShapeDtypeStruct((b, rows, d), F32),
        grid=(b, rows // tp),
        in_specs=[tok(BRANCH_W)] * 6 + [tok(N_BRANCH * d), _resident(wbr.shape), _resident(wout.shape),
                  tok(d),
                  _mod_spec(d, seq // tp),
                  _resident((1, d))],
        out_specs=tok(d),
        compiler_params=pltpu.CompilerParams(
            dimension_semantics=("parallel", "parallel"), vmem_limit_bytes=VMEM_LIMIT),
        name="merge",
    )(ya, hf, hb, p['bo'], yc, yd, p['gate'], wbr, wout, hh, modall, gpost)


def _ffn_kernel(h_ref, mod_ref, gpre_ref, gpost_ref, wg_ref, wu_ref, wd_ref, o_ref):
    x = h_ref[0]
    u = (_rms(x, gpre_ref[...]) * (1.0 + mod_ref[0, 4:5, :]) + mod_ref[0, 3:4, :]).astype(BF16)
    gate = jnp.dot(u, wg_ref[...], preferred_element_type=F32)
    up = jnp.dot(u, wu_ref[...], preferred_element_type=F32)
    act = (gate * _sigmoid(gate) * up).astype(BF16)
    out = jnp.dot(act, wd_ref[...], preferred_element_type=F32)
    o_ref[0] = x + mod_ref[0, 5:6, :] * _rms(out, gpost_ref[...])


def _ffn(hh, modall, gpre, gpost, wg, wu, wd, seq):
    b, rows, d = hh.shape
    tp = TOKEN_TILE
    tok = pl.BlockSpec((1, tp, d), lambda bi, t: (bi, t, 0))
    return pl.pallas_call(
        _ffn_kernel,
        out_shape=jax.ShapeDtypeStruct(hh.shape, F32),
        grid=(b, rows // tp),
        in_specs=[tok, _mod_spec(d, seq // tp),
                  _resident((1, d)), _resident((1, d)),
                  _resident(wg.shape), _resident(wu.shape), _resident(wd.shape)],
        out_specs=tok,
        compiler_params=pltpu.CompilerParams(
            dimension_semantics=("parallel", "parallel"), vmem_limit_bytes=VMEM_LIMIT),
        name="ffn",
    )(hh, modall, gpre, gpost, wg, wu, wd)


def _rope_tables(n_rows, ctx_len):
    row = jnp.repeat(jnp.arange(n_rows, dtype=F32), GRID_W)
    col = jnp.tile(jnp.arange(GRID_W, dtype=F32), n_rows)
    quarter = ROPE_DIM // 4
    inv_freq = ROPE_BASE ** (-jnp.arange(quarter, dtype=F32) / quarter)
    ang_r, ang_c = row[:, None] * inv_freq, col[:, None] * inv_freq
    ang = jnp.concatenate([ang_r, ang_r, ang_c, ang_c], axis=-1)
    cos, sin = jnp.cos(ang), jnp.sin(ang)
    cos = jnp.concatenate([cos, jnp.ones((ctx_len, ROPE_DIM), F32)], axis=0)
    sin = jnp.concatenate([sin, jnp.zeros((ctx_len, ROPE_DIM), F32)], axis=0)
    cos, sin = jnp.tile(cos, (1, LANES // ROPE_DIM)), jnp.tile(sin, (1, LANES // ROPE_DIM))
    first = (jnp.arange(LANES) % (ROPE_DIM // 2)) < quarter
    return cos, jnp.where(first, -sin, 0.0), jnp.where(first, 0.0, sin)


def _a_head_perm():
    idx = []
    for j in range(A_HEADS // 2):
        for half in range(2):
            head = j + (A_HEADS // 2) * half
            idx.extend(range(head * A_HD, (head + 1) * A_HD))
    return np.asarray(idx)


def _prep_layer(w_in, b_gate_bias, c_g_q, c_g_kv, c_w_uq, c_w_uk, c_w_uv, w_branch):
    d = w_in.shape[0]
    splits = (512, 128, 128, 512, 512, 512, 16, 512, 512, 256, 64, 512, 512, 512, N_BRANCH * d)
    offs = np.concatenate([[0], np.cumsum(splits)])
    part = [w_in[:, offs[i]:offs[i + 1]] for i in range(len(splits))]
    (a_q, a_k, a_v, b_q, b_k, b_v, b_g, b_o, c_q, c_kv, c_kr, d_q, d_k, d_v, gate) = part
    perm = _a_head_perm()
    zeros = lambda n: jnp.zeros((d, n), w_in.dtype)
    uq = c_w_uq.reshape(C_Q_LORA, C_HEADS, C_NOPE + C_ROPE)
    uq = jnp.pad(uq, ((0, 0), (0, 0), (0, C_QK_PAD - C_NOPE - C_ROPE))).reshape(C_Q_LORA, C_HEADS * C_QK_PAD)
    w = {
        'wa': jnp.concatenate([a_q[:, perm], a_k, a_v], axis=1).astype(BF16),
        'wb': jnp.concatenate([b_q, b_k, b_v, b_o], axis=1).astype(BF16),
        'wgt': b_g.T.astype(BF16),
        'wc': jnp.concatenate([c_q, c_kv, c_kr, zeros(LANES - C_ROPE)], axis=1).astype(BF16),
        'wd': jnp.concatenate([d_q, d_k, d_v], axis=1).astype(BF16),
        'wgate': gate.astype(BF16),
        'gbiast': b_gate_bias.reshape(16, 1),
        'cgq': c_g_q.reshape(1, -1),
        'cgkv': c_g_kv.reshape(1, -1),
        'wuq': uq.astype(BF16),
        'wukv': jnp.concatenate([c_w_uk, c_w_uv], axis=1).astype(BF16),
    }
    wbr = jnp.concatenate([w_branch[0:1, perm], w_branch[1:]], axis=0).astype(BF16)
    return w, wbr


def kernel(x, c, ctx, c_ctx, w_mod, b_mod, g_mix_pre, g_mix_post, g_ffn_pre, g_ffn_post, w_in, a_sink,
           b_gate_bias, c_g_q, c_g_kv, c_w_uq, c_w_uk, c_w_uv, d_lam_q1, d_lam_k1, d_lam_q2, d_lam_k2,
           d_g_sub, w_branch, w_out, w_gate_up, w_down):
    b, seq, d = x.shape
    ctx_len = ctx.shape[1]
    depth = w_in.shape[0]
    tables = _rope_tables(seq // GRID_W, ctx_len)

    rows = jnp.concatenate([c, c_ctx[None, :], jnp.zeros((8 - b - 1, d), F32)], axis=0)
    mods = _modulation(rows, w_mod, b_mod).reshape(depth, 8, 6, d)

    assert ctx_len == TOKEN_TILE == WINDOW_TILE, "the context must be exactly one token tile"
    hh = jnp.concatenate([x, ctx], axis=1)
    for l in range(depth):
        with_ctx = l < depth - 1
        modall = jnp.stack([jnp.broadcast_to(mods[l, b], (b, 6, d)), mods[l, :b]], axis=1).reshape(2 * b, 6, d)
        w, wbr = _prep_layer(w_in[l], b_gate_bias[l], c_g_q[l], c_g_kv[l], c_w_uq[l], c_w_uk[l],
                             c_w_uv[l], w_branch[l])
        lam_init = 0.8 - 0.6 * math.exp(-0.3 * l)
        lam_vecs = jnp.stack([d_lam_q1[l], d_lam_k1[l], d_lam_q2[l], d_lam_k2[l]], axis=0)
        hidden = w_down.shape[1]

        p = _project(hh, modall, g_mix_pre[l].reshape(1, d), tables, w)
        ya = _window_attention(p['aq'], p['ak'], p['av'], a_sink[l], ctx_len)
        hf, hb = _mlstm(p['bq'], p['bk'], p['bv'], p['bgt'], ctx_len)
        yc = _mla_attention(p['cq'], p['ck'], p['cv'], seq, with_ctx)
        yd = _diff_attention(p['dq'], p['dk'], p['dv'], lam_vecs, d_g_sub[l].reshape(1, -1), lam_init,
                             seq, with_ctx)
        rows = seq + ctx_len if with_ctx else seq
        hh = _merge(p, ya, hf, hb, yc, yd, wbr, w_out[l].astype(BF16), hh, modall,
                    g_mix_post[l].reshape(1, d), seq, rows)
        hh = _ffn(hh, modall, g_ffn_pre[l].reshape(1, d), g_ffn_post[l].reshape(1, d),
                  w_gate_up[l, :, :hidden].astype(BF16), w_gate_up[l, :, hidden:].astype(BF16),
                  w_down[l].astype(BF16), seq)
    return hh
```

```python
import functools
import math

import numpy as np
import jax
import jax.numpy as jnp
from jax import lax
from jax.experimental import pallas as pl
from jax.experimental.pallas import tpu as pltpu

F32 = jnp.float32
BF16 = jnp.bfloat16

GRID_W = 64
ROPE_BASE = 10000.0
NORM_EPS = 1e-6
A_HEADS, A_KV_HEADS, A_HD, A_WINDOW = 8, 2, 64, 128
B_HEADS, B_HD, B_CHUNK = 4, 128, 128
C_HEADS, C_Q_LORA, C_KV_LORA, C_NOPE, C_ROPE, C_VD = 4, 512, 256, 128, 64, 128
D_HEADS, D_HD, D_VD = 4, 64, 128
N_BRANCH = 4
BRANCH_W = 512
ROPE_DIM = 64
C_QK_PAD = 256

LANES = 128
NEG_BIG = -1e30
VMEM_LIMIT = 56 * 1024 * 1024

TOKEN_TILE = 256
WINDOW_TILE = 256
FLASH_TQ = 2048
MLA_TK = 4096
DIFF_TK = 2048
FLASH_SUB = 256
MLSTM_STEP_CHUNKS = 2
LOG2E = math.log2(math.e)


def _resident(shape):
    nd = len(shape)
    return pl.BlockSpec(shape, lambda *_: (0,) * nd, pipeline_mode=pl.Buffered(1))


def _mod_spec(d, ctx_tile):
    return pl.BlockSpec((1, 6, d), lambda bi, t: (bi * 2 + jnp.where(t == ctx_tile, 0, 1), 0, 0))


def _sigmoid(x):
    return 1.0 / (1.0 + jnp.exp(-x))


def _log_sigmoid(x):
    return jnp.minimum(x, 0.0) - jnp.log(1.0 + jnp.exp(-jnp.abs(x)))


def _rms(x, g):
    return x * lax.rsqrt(jnp.mean(x * x, axis=-1, keepdims=True) + NORM_EPS) * g


def _rope(x, cos, sin_lo, sin_hi):
    return x * cos + pltpu.roll(x, LANES - 16, 1) * sin_lo + pltpu.roll(x, 16, 1) * sin_hi


def _dot_nt(a, b):
    return lax.dot_general(a, b, (((1,), (1,)), ((), ())), preferred_element_type=F32)


def _dot_tn(a, b):
    return lax.dot_general(a, b, (((0,), (0,)), ((), ())), preferred_element_type=F32)


def _mod_kernel(c_ref, w_ref, b_ref, o_ref):
    c = c_ref[...]
    s = c * _sigmoid(c)
    o_ref[0] = jnp.dot(s, w_ref[0], preferred_element_type=F32,
                       precision=lax.Precision.HIGHEST) + b_ref[0]


def _modulation(rows, w_mod, b_mod):
    depth, d, n = w_mod.shape
    tn = 1024
    return pl.pallas_call(
        _mod_kernel,
        out_shape=jax.ShapeDtypeStruct((depth, rows.shape[0], n), F32),
        grid=(depth, n // tn),
        in_specs=[pl.BlockSpec(rows.shape, lambda l, j: (0, 0)),
                  pl.BlockSpec((1, d, tn), lambda l, j: (l, 0, j)),
                  pl.BlockSpec((1, 1, tn), lambda l, j: (l, 0, j))],
        out_specs=pl.BlockSpec((1, rows.shape[0], tn), lambda l, j: (l, 0, j)),
        name="modulation",
    )(rows, w_mod, b_mod.reshape(depth, 1, n))


def _stream_specs(stream, tp):
    d = stream[0].shape[2]
    if len(stream) == 1:
        return [pl.BlockSpec((1, tp, d), lambda bi, t: (bi, t, 0))]
    last = stream[0].shape[1] // tp - 1
    return [pl.BlockSpec((1, tp, d), lambda bi, t: (bi, jnp.minimum(t, last), 0)),
            pl.BlockSpec((1, tp, d), lambda bi, t: (bi, 0, 0))]


def _stream_tile(refs, lat_tiles):
    if len(refs) == 1:
        return refs[0][0]
    return jnp.where(pl.program_id(1) == lat_tiles, refs[1][0], refs[0][0])


def _proj_kernel(*refs, n_stream, lat_tiles):
    (mod_ref, gpre_ref, cos_ref, slo_ref, shi_ref,
     wa_ref, wb_ref, wgt_ref, wc_ref, wd_ref, wgate_ref,
     gbiast_ref, cgq_ref, cgkv_ref, wuq_ref, wukv_ref,
     aq_ref, ak_ref, av_ref, bq_ref, bk_ref, bv_ref, bo_ref, bgt_ref,
     cq_ref, ck_ref, cv_ref, dq_ref, dk_ref, dv_ref, gate_ref) = refs[n_stream:]
    x = _stream_tile(refs[:n_stream], lat_tiles)
    y = _rms(x, gpre_ref[...])
    u = (y * (1.0 + mod_ref[0, 1:2, :]) + mod_ref[0, 0:1, :]).astype(BF16)
    cos, slo, shi = cos_ref[...], slo_ref[...], shi_ref[...]

    def rope(v):
        return _rope(v, cos, slo, shi)

    def chunks(v, n):
        return [v[:, i * LANES:(i + 1) * LANES] for i in range(n)]

    pa = jnp.dot(u, wa_ref[...], preferred_element_type=F32)
    for i, v in enumerate(chunks(pa, 4)):
        aq_ref[0, :, i * LANES:(i + 1) * LANES] = (rope(v) * (A_HD ** -0.5 * LOG2E)).astype(BF16)
    ak_ref[0] = rope(pa[:, 512:640]).astype(BF16)
    av_ref[0] = pa[:, 640:768].astype(BF16)

    pb = jnp.dot(u, wb_ref[...], preferred_element_type=F32)
    bq_ref[0] = pb[:, 0:512].astype(BF16)
    bk_ref[0] = (pb[:, 512:1024] * (B_HD ** -0.5)).astype(BF16)
    bv_ref[0] = pb[:, 1024:1536].astype(BF16)
    bo_ref[0] = pb[:, 1536:2048].astype(BF16)
    bgt_ref[0] = _dot_nt(wgt_ref[...], u) + gbiast_ref[...]

    pc = jnp.dot(u, wc_ref[...], preferred_element_type=F32)
    cqn = _rms(pc[:, 0:C_Q_LORA], cgq_ref[...]).astype(BF16)
    ckvn = _rms(pc[:, C_Q_LORA:C_Q_LORA + C_KV_LORA], cgkv_ref[...]).astype(BF16)
    kr = rope(pc[:, 768:896]).astype(BF16)
    cq = jnp.dot(cqn, wuq_ref[...], preferred_element_type=F32)
    ckv = jnp.dot(ckvn, wukv_ref[...], preferred_element_type=F32)
    c_scale = (C_NOPE + C_ROPE) ** -0.5 * LOG2E
    for h in range(C_HEADS):
        o = h * C_QK_PAD
        cq_ref[0, :, o:o + LANES] = (cq[:, o:o + LANES] * c_scale).astype(BF16)
        cq_ref[0, :, o + LANES:o + 2 * LANES] = (rope(cq[:, o + LANES:o + 2 * LANES]) * c_scale).astype(BF16)
        ck_ref[0, :, o:o + LANES] = ckv[:, h * LANES:(h + 1) * LANES].astype(BF16)
        ck_ref[0, :, o + LANES:o + 2 * LANES] = kr
    cv_ref[0] = ckv[:, 512:1024].astype(BF16)

    pd = jnp.dot(u, wd_ref[...], preferred_element_type=F32)
    for i in range(4):
        sl = slice(i * LANES, (i + 1) * LANES)
        dq_ref[0, :, sl] = (rope(pd[:, sl]) * (D_HD ** -0.5 * LOG2E)).astype(BF16)
        dk_ref[0, :, sl] = rope(pd[:, 512 + i * LANES:512 + (i + 1) * LANES]).astype(BF16)
    dv_ref[0] = pd[:, 1024:1536].astype(BF16)

    for i in range(N_BRANCH):
        sl = slice(i * 1024, (i + 1) * 1024)
        gate_ref[0, :, sl] = jnp.dot(u, wgate_ref[:, sl], preferred_element_type=F32).astype(BF16)


def _project(stream, modall, gpre, tables, w):
    b, _, d = stream[0].shape
    tt = sum(s.shape[1] for s in stream)
    tp = TOKEN_TILE
    nt = tt // tp
    cos, slo, shi = tables

    def tok(width):
        return pl.BlockSpec((1, tp, width), lambda bi, t: (bi, t, 0))

    tab = pl.BlockSpec((tp, LANES), lambda bi, t: (t, 0))
    in_specs = _stream_specs(stream, tp) + [
        _mod_spec(d, nt - 1),
        _resident((1, d)), tab, tab, tab,
        _resident(w['wa'].shape), _resident(w['wb'].shape),
        _resident(w['wgt'].shape), _resident(w['wc'].shape), _resident(w['wd'].shape),
        _resident(w['wgate'].shape), _resident(w['gbiast'].shape),
        _resident(w['cgq'].shape), _resident(w['cgkv'].shape), _resident(w['wuq'].shape),
        _resident(w['wukv'].shape),
    ]
    widths = dict(aq=512, ak=128, av=128, bq=512, bk=512, bv=512, bo=512,
                  cq=C_HEADS * C_QK_PAD, ck=C_HEADS * C_QK_PAD, cv=512,
                  dq=512, dk=512, dv=512, gate=N_BRANCH * d)
    names = ['aq', 'ak', 'av', 'bq', 'bk', 'bv', 'bo', 'bgt',
             'cq', 'ck', 'cv', 'dq', 'dk', 'dv', 'gate']
    out_shape, out_specs = [], []
    for n in names:
        if n == 'bgt':
            out_shape.append(jax.ShapeDtypeStruct((b, 16, tt), F32))
            out_specs.append(pl.BlockSpec((1, 16, tp), lambda bi, t: (bi, 0, t)))
        else:
            out_shape.append(jax.ShapeDtypeStruct((b, tt, widths[n]), BF16))
            out_specs.append(tok(widths[n]))
    outs = pl.pallas_call(
        functools.partial(_proj_kernel, n_stream=len(stream), lat_tiles=nt - 1),
        out_shape=out_shape,
        grid=(b, nt),
        in_specs=in_specs,
        out_specs=out_specs,
        compiler_params=pltpu.CompilerParams(
            dimension_semantics=("parallel", "parallel"), vmem_limit_bytes=VMEM_LIMIT),
        name="project",
    )(*stream, modall, gpre, cos, slo, shi, w['wa'], w['wb'], w['wgt'], w['wc'], w['wd'],
      w['wgate'], w['gbiast'], w['cgq'], w['cgkv'], w['wuq'], w['wukv'])
    return dict(zip(names, outs))


def _window_kernel(sink_ref, q_ref, kc_ref, kp_ref, kk_ref, kn_ref,
                   vc_ref, vp_ref, vk_ref, vn_ref, o_ref, *, lat_tiles):
    t = pl.program_id(1)
    hb = WINDOW_TILE // 2
    n_ctx = kc_ref.shape[1]
    group = A_HEADS // A_KV_HEADS
    width = n_ctx + 3 * hb
    kwin = jnp.concatenate([kp_ref[0], kk_ref[0], kn_ref[0]], axis=0)
    vwin = jnp.concatenate([vp_ref[0], vk_ref[0], vn_ref[0]], axis=0)
    prev_ok = jnp.logical_and(t >= 1, t < lat_tiles).astype(jnp.int32)
    cur_ok = (t < lat_tiles).astype(jnp.int32)
    next_ok = (t < lat_tiles - 1).astype(jnp.int32)
    row = lax.broadcasted_iota(jnp.int32, (group * hb, width), 0) & (hb - 1)
    col = lax.broadcasted_iota(jnp.int32, (group * hb, width), 1)
    wcol = col - n_ctx
    near = jnp.abs(wcol - hb - row) <= A_WINDOW
    low = lax.broadcasted_iota(jnp.int32, (hb, LANES), 1) < A_HD
    for blk in range(2):
        k0 = hb * (1 + blk)
        kb = jnp.concatenate([kc_ref[0], kwin[k0:k0 + 3 * hb]], axis=0)
        vb = jnp.concatenate([vc_ref[0], vwin[k0:k0 + 3 * hb]], axis=0)
        vb_ext = jnp.concatenate([vb, jnp.ones_like(vb)], axis=1)
        if blk == 0:
            seg_ok = jnp.where(wcol < hb, prev_ok, cur_ok)
        else:
            seg_ok = jnp.where(wcol < 2 * hb, cur_ok, next_ok)
        valid = jnp.logical_or(col < n_ctx, jnp.logical_and(near, seg_ok > 0))
        rows = slice(blk * hb, (blk + 1) * hb)
        outs = []
        for g in range(A_KV_HEADS):
            keep = low if g == 0 else jnp.logical_not(low)
            qs, sinks = [], []
            for j in range(group):
                qb = q_ref[0, rows, j * LANES:(j + 1) * LANES]
                qs.append(jnp.where(keep, qb, jnp.zeros_like(qb)))
                sinks.append(jnp.full((hb, LANES), sink_ref[j + group * g] * LOG2E, F32))
            sink = jnp.concatenate(sinks, axis=0)
            s = jnp.where(valid, _dot_nt(jnp.concatenate(qs, axis=0), kb), NEG_BIG)
            m = jnp.maximum(jnp.max(s, axis=-1, keepdims=True), sink)
            e = jnp.exp2(s - jnp.tile(m, (1, width // LANES)))
            pv = jnp.dot(e.astype(BF16), vb_ext, preferred_element_type=F32)
            outs.append(pv[:, :LANES] / (pv[:, LANES:] + jnp.exp2(sink - m)))
        for j in range(group):
            o_ref[0, rows, j * LANES:(j + 1) * LANES] = jnp.where(
                low, outs[0][j * hb:(j + 1) * hb], outs[1][j * hb:(j + 1) * hb]).astype(BF16)


def _window_attention(q, k, v, sink, ctx_len):
    b, tt, _ = q.shape
    tw = WINDOW_TILE
    assert ctx_len == tw == 2 * A_WINDOW, "context is one key tile; a query block is one window radius"
    nt = tt // tw
    nl = nt - 1
    kv = lambda f: pl.BlockSpec((1, tw, LANES), lambda bi, t: (bi, f(t), 0))
    specs = [kv(lambda t: nl), kv(lambda t: jnp.maximum(t - 1, 0)), kv(lambda t: t),
             kv(lambda t: jnp.minimum(t + 1, nl - 1))]
    return pl.pallas_call(
        functools.partial(_window_kernel, lat_tiles=nl),
        out_shape=jax.ShapeDtypeStruct((b, tt, 512), BF16),
        grid=(b, nt),
        in_specs=[pl.BlockSpec(memory_space=pltpu.SMEM),
                  pl.BlockSpec((1, tw, 512), lambda bi, t: (bi, t, 0))] + specs + specs,
        out_specs=pl.BlockSpec((1, tw, 512), lambda bi, t: (bi, t, 0)),
        compiler_params=pltpu.CompilerParams(
            dimension_semantics=("parallel", "parallel"), vmem_limit_bytes=VMEM_LIMIT),
        name="window_attention",
    )(sink, q, k, k, k, k, v, v, v, v)


def _mlstm_kernel(qf_ref, kf_ref, vf_ref, gtf_ref, qb_ref, kb_ref, vb_ref, gtb_ref,
                  hf_ref, hb_ref, c_ref, m_ref):
    step = pl.program_id(1)

    @pl.when(step == 0)
    def _():
        c_ref[...] = jnp.zeros_like(c_ref)
        m_ref[...] = jnp.zeros_like(m_ref)

    L = B_CHUNK
    row = lax.broadcasted_iota(jnp.int32, (L, L), 0)
    col = lax.broadcasted_iota(jnp.int32, (L, L), 1)
    dirs = ((qf_ref, kf_ref, vf_ref, gtf_ref, hf_ref), (qb_ref, kb_ref, vb_ref, gtb_ref, hb_ref))
    for d, (q_ref, k_ref, v_ref, gt_ref, out_ref) in enumerate(dirs):
        before = (col <= row) if d == 0 else (col >= row)
        cum = before.astype(F32)
        cum_t = ((row <= col) if d == 0 else (row >= col)).astype(F32)
        last = L - 1 if d == 0 else 0
        n_chunks = q_ref.shape[1] // L
        for c in (range(n_chunks) if d == 0 else reversed(range(n_chunks))):
            rows = slice(c * L, (c + 1) * L)
            gt = gt_ref[0, :, rows]
            lf_all = _log_sigmoid(gt)
            frow_all = jnp.dot(lf_all, cum_t, preferred_element_type=F32, precision=lax.Precision.HIGHEST)
            for h in range(B_HEADS):
                idx = d * B_HEADS + h
                ci, cf = (2 * d) * B_HEADS + h, (2 * d + 1) * B_HEADS + h
                i_row, lf_row, f_row = gt[ci:ci + 1, :], lf_all[cf:cf + 1, :], frow_all[cf:cf + 1, :]
                f_col = jnp.sum(cum * lf_row, axis=-1, keepdims=True)
                a_row = i_row - f_row
                b_last = f_row[:, last:last + 1]
                m_prev = m_ref[idx][:, 0:1]
                q = q_ref[0, rows, h * B_HD:(h + 1) * B_HD]
                k = k_ref[0, rows, h * B_HD:(h + 1) * B_HD]
                v = v_ref[0, rows, h * B_HD:(h + 1) * B_HD]
                v_ext = jnp.concatenate([v, jnp.ones_like(v)], axis=1)
                dmat = jnp.where(before, f_col + a_row, -jnp.inf)
                m_t = jnp.maximum(f_col + m_prev, jnp.max(dmat, axis=-1, keepdims=True))
                wqk = jnp.exp(dmat - m_t) * _dot_nt(q, k)
                a = jnp.exp(f_col + m_prev - m_t)
                ct = c_ref[idx]
                num_den = (jnp.dot(wqk.astype(BF16), v_ext, preferred_element_type=F32)
                           + a * jnp.dot(q, ct.astype(BF16), preferred_element_type=F32))
                hval = num_den[:, :B_HD] / jnp.maximum(jnp.abs(num_den[:, B_HD:]), jnp.exp(-m_t))
                out_ref[0, rows, h * B_HD:(h + 1) * B_HD] = hval.astype(BF16)
                g_row = b_last + a_row
                m_new = jnp.maximum(b_last + m_prev, jnp.max(g_row, axis=-1, keepdims=True))
                decay = jnp.exp(b_last + m_prev - m_new)
                kts = (k.astype(F32).T * jnp.exp(g_row - m_new)).astype(BF16)
                c_ref[idx] = decay * ct + jnp.dot(kts, v_ext, preferred_element_type=F32)
                m_ref[idx] = jnp.broadcast_to(m_new, (1, LANES))


def _mlstm(q, k, v, gt, ctx_len):
    b, tt, w = q.shape
    L = MLSTM_STEP_CHUNKS * B_CHUNK
    assert ctx_len % L == 0 and tt % L == 0
    nc = tt // L
    cc = ctx_len // L

    nl = nc - cc

    def fwd(s):
        return jnp.where(s < cc, nl + s, s - cc)

    def bwd(s):
        return nc - 1 - s

    def specs(f):
        tok = pl.BlockSpec((1, L, w), lambda bi, s: (bi, f(s), 0))
        return [tok, tok, tok, pl.BlockSpec((1, 16, L), lambda bi, s: (bi, 0, f(s)))]

    out = jax.ShapeDtypeStruct((b, tt, w), BF16)
    return pl.pallas_call(
        _mlstm_kernel,
        out_shape=[out, out],
        grid=(b, nc),
        in_specs=specs(fwd) + specs(bwd),
        out_specs=[pl.BlockSpec((1, L, w), lambda bi, s: (bi, fwd(s), 0)),
                   pl.BlockSpec((1, L, w), lambda bi, s: (bi, bwd(s), 0))],
        scratch_shapes=[pltpu.VMEM((2 * B_HEADS, B_HD, 2 * B_HD), F32),
                        pltpu.VMEM((2 * B_HEADS, 1, LANES), F32)],
        compiler_params=pltpu.CompilerParams(
            dimension_semantics=("parallel", "arbitrary"), vmem_limit_bytes=VMEM_LIMIT),
        name="mlstm",
    )(q, k, v, gt, q, k, v, gt)


def _softmax_update(q, k, v_ext, m_old, acc):
    s = _dot_nt(q, k)
    m_new = jnp.maximum(m_old, jnp.max(s, axis=-1, keepdims=True))
    p = jnp.exp2(s - jnp.tile(m_new, (1, s.shape[1] // LANES)))
    alpha = jnp.exp2(m_old - m_new)
    acc = jnp.tile(alpha, (1, 2)) * acc + jnp.dot(p.astype(BF16), v_ext, preferred_element_type=F32)
    return m_new, acc


def _attend(qs, stats, k_ref, v_ref):
    vals = [(m_ref[...], acc_ref[...]) for m_ref, acc_ref in stats]
    sub = min(FLASH_SUB, k_ref.shape[1])
    for c in range(k_ref.shape[1] // sub):
        k = k_ref[0, c * sub:(c + 1) * sub, :]
        v = v_ref[0, c * sub:(c + 1) * sub, :]
        v_ext = jnp.concatenate([v, jnp.ones_like(v)], axis=1)
        vals = [_softmax_update(q, k, v_ext, m, acc) for q, (m, acc) in zip(qs, vals)]
    for (m_ref, acc_ref), (m, acc) in zip(stats, vals):
        m_ref[...] = m
        acc_ref[...] = acc


def _flash_init(stats):
    for m_ref, acc_ref in stats:
        m_ref[...] = jnp.full_like(m_ref, NEG_BIG)
        acc_ref[...] = jnp.zeros_like(acc_ref)


def _normalized(acc_ref):
    return acc_ref[:, :LANES] / acc_ref[:, LANES:]


def _mla_queries(q):
    return [q]


def _mla_finish(stats):
    return _normalized(stats[0][1])


def _diff_queries(q):
    low = lax.broadcasted_iota(jnp.int32, q.shape, 1) < D_HD
    zero = jnp.zeros_like(q)
    return [jnp.where(low, q, zero), jnp.where(low, zero, q)]


def _diff_finish(stats, lam_ref, gsub_ref, *, lam_init):
    lq1, lk1, lq2, lk2 = (lam_ref[r:r + 1, :] for r in range(4))
    lam = (jnp.exp(jnp.sum(lq1 * lk1, axis=-1, keepdims=True))
           - jnp.exp(jnp.sum(lq2 * lk2, axis=-1, keepdims=True)) + lam_init)
    o = _normalized(stats[0][1]) - lam * _normalized(stats[1][1])
    return _rms(o, gsub_ref[...]) * (1.0 - lam_init)


def _flash_kernel(*refs, n_extra, n_maps, queries, finish, latent):
    extra, refs = refs[:n_extra], refs[n_extra:]
    if latent:
        q_ref, kc_ref, vc_ref, k_ref, v_ref, o_ref = refs[:6]
        scratch = refs[6:]
    else:
        q_ref, kc_ref, vc_ref, _, o_ref = refs[:5]
        scratch = refs[5:]
    stats = [(scratch[2 * i], scratch[2 * i + 1]) for i in range(n_maps)]
    qs = queries(q_ref[0])
    if not latent:
        _flash_init(stats)
        _attend(qs, stats, kc_ref, vc_ref)
        o_ref[0] = finish(stats, *extra).astype(BF16)
        return
    j = pl.program_id(3)

    @pl.when(j == 0)
    def _():
        _flash_init(stats)
        _attend(qs, stats, kc_ref, vc_ref)

    _attend(qs, stats, k_ref, v_ref)

    @pl.when(j == pl.num_programs(3) - 1)
    def _():
        o_ref[0] = finish(stats, *extra).astype(BF16)


def _full_attention(q, k, v, seq, key_block, heads, qk_width, n_maps, queries, finish, extra, ctx_queries, name):
    b, tt, _ = q.shape
    ctx_len = tt - seq
    cb = seq // ctx_len
    tq, tk = min(FLASH_TQ, seq), min(key_block, seq)
    assert seq % tq == 0 and seq % tk == 0 and seq % ctx_len == 0
    out_shape = jax.ShapeDtypeStruct((b, tt, heads * LANES), BF16)

    def scratch(rows):
        return [pltpu.VMEM((rows, LANES), F32), pltpu.VMEM((rows, 2 * LANES), F32)] * n_maps

    body = functools.partial(_flash_kernel, n_extra=len(extra), n_maps=n_maps, queries=queries, finish=finish)
    whole4 = [pl.BlockSpec(e.shape, lambda bi, h, i, j: (0, 0)) for e in extra]
    y = pl.pallas_call(
        functools.partial(body, latent=True),
        out_shape=out_shape,
        grid=(b, heads, seq // tq, seq // tk),
        in_specs=whole4 + [
            pl.BlockSpec((1, tq, qk_width), lambda bi, h, i, j: (bi, i, h)),
            pl.BlockSpec((1, ctx_len, qk_width), lambda bi, h, i, j: (bi, cb, h)),
            pl.BlockSpec((1, ctx_len, LANES), lambda bi, h, i, j: (bi, cb, h)),
            pl.BlockSpec((1, tk, qk_width), lambda bi, h, i, j: (bi, j, h)),
            pl.BlockSpec((1, tk, LANES), lambda bi, h, i, j: (bi, j, h))],
        out_specs=pl.BlockSpec((1, tq, LANES), lambda bi, h, i, j: (bi, i, h)),
        scratch_shapes=scratch(tq),
        compiler_params=pltpu.CompilerParams(
            dimension_semantics=("parallel", "parallel", "parallel", "arbitrary"),
            vmem_limit_bytes=VMEM_LIMIT),
        name=name,
    )(*extra, q, k, v, k, v)
    if not ctx_queries:
        return y
    whole2 = [pl.BlockSpec(e.shape, lambda bi, h: (0, 0)) for e in extra]
    return pl.pallas_call(
        functools.partial(body, latent=False),
        out_shape=out_shape,
        grid=(b, heads),
        in_specs=whole2 + [
            pl.BlockSpec((1, ctx_len, qk_width), lambda bi, h: (bi, cb, h)),
            pl.BlockSpec((1, ctx_len, qk_width), lambda bi, h: (bi, cb, h)),
            pl.BlockSpec((1, ctx_len, LANES), lambda bi, h: (bi, cb, h)),
            pl.BlockSpec(memory_space=pl.ANY)],
        out_specs=pl.BlockSpec((1, ctx_len, LANES), lambda bi, h: (bi, cb, h)),
        scratch_shapes=scratch(ctx_len),
        input_output_aliases={len(extra) + 3: 0},
        compiler_params=pltpu.CompilerParams(
            dimension_semantics=("parallel", "parallel"), vmem_limit_bytes=VMEM_LIMIT),
        name=name + "_ctx",
    )(*extra, q, k, v, y)


def _mla_attention(q, k, v, seq, ctx_queries):
    return _full_attention(q, k, v, seq, MLA_TK, C_HEADS, C_QK_PAD, 1, _mla_queries, _mla_finish, [],
                           ctx_queries, "mla_attention")


def _diff_attention(q, k, v, lam_vecs, gsub, lam_init, seq, ctx_queries):
    return _full_attention(q, k, v, seq, DIFF_TK, D_HEADS, 2 * D_HD, 2, _diff_queries,
                           functools.partial(_diff_finish, lam_init=lam_init), [lam_vecs, gsub],
                           ctx_queries, "diff_attention")


def _merge_ffn_kernel(*refs, n_stream, lat_tiles):
    (ya_ref, hf_ref, hb_ref, bo_ref, yc_ref, yd_ref, gate_ref, wbr_ref, wout_ref,
     mod_ref, gpost_ref, fpre_ref, fpost_ref, wg_ref, wu_ref, wd_ref, o_ref) = refs[n_stream:]
    d = o_ref.shape[2]
    yb = (_sigmoid(bo_ref[0].astype(F32))
          * (hf_ref[0].astype(F32) + hb_ref[0].astype(F32))).astype(BF16)
    ys = (ya_ref[0], yb, yc_ref[0], yd_ref[0])
    merged = None
    for i, y in enumerate(ys):
        g = _sigmoid(gate_ref[0, :, i * d:(i + 1) * d].astype(F32))
        term = g * jnp.dot(y, wbr_ref[i], preferred_element_type=F32)
        merged = term if merged is None else merged + term
    out = jnp.dot(merged.astype(BF16), wout_ref[...], preferred_element_type=F32)
    x = _stream_tile(refs[:n_stream], lat_tiles) + mod_ref[0, 2:3, :] * _rms(out, gpost_ref[...])
    u = (_rms(x, fpre_ref[...]) * (1.0 + mod_ref[0, 4:5, :]) + mod_ref[0, 3:4, :]).astype(BF16)
    gate = jnp.dot(u, wg_ref[...], preferred_element_type=F32)
    up = jnp.dot(u, wu_ref[...], preferred_element_type=F32)
    act = (gate * _sigmoid(gate) * up).astype(BF16)
    out = jnp.dot(act, wd_ref[...], preferred_element_type=F32)
    o_ref[0] = x + mod_ref[0, 5:6, :] * _rms(out, fpost_ref[...])


def _merge_ffn(p, ya, hf, hb, yc, yd, wbr, wout, stream, modall, gpost, fpre, fpost, wg, wu, wd, seq, rows):
    b, _, d = stream[0].shape
    tp = TOKEN_TILE
    tok = lambda width: pl.BlockSpec((1, tp, width), lambda bi, t: (bi, t, 0))
    vec = _resident((1, d))
    return pl.pallas_call(
        functools.partial(_merge_ffn_kernel, n_stream=len(stream), lat_tiles=seq // tp),
        out_shape=jax.ShapeDtypeStruct((b, rows, d), F32),
        grid=(b, rows // tp),
        in_specs=_stream_specs(stream, tp) + [tok(BRANCH_W)] * 6 + [
            tok(N_BRANCH * d), _resident(wbr.shape), _resident(wout.shape),
            _mod_spec(d, seq // tp), vec, vec, vec,
            _resident(wg.shape), _resident(wu.shape), _resident(wd.shape)],
        out_specs=tok(d),
        compiler_params=pltpu.CompilerParams(
            dimension_semantics=("parallel", "parallel"), vmem_limit_bytes=VMEM_LIMIT),
        name="merge_ffn",
    )(*stream, ya, hf, hb, p['bo'], yc, yd, p['gate'], wbr, wout, modall, gpost, fpre, fpost, wg, wu, wd)


def _rope_tables(n_rows, ctx_len):
    row = jnp.repeat(jnp.arange(n_rows, dtype=F32), GRID_W)
    col = jnp.tile(jnp.arange(GRID_W, dtype=F32), n_rows)
    quarter = ROPE_DIM // 4
    inv_freq = ROPE_BASE ** (-jnp.arange(quarter, dtype=F32) / quarter)
    ang_r, ang_c = row[:, None] * inv_freq, col[:, None] * inv_freq
    ang = jnp.concatenate([ang_r, ang_r, ang_c, ang_c], axis=-1)
    cos, sin = jnp.cos(ang), jnp.sin(ang)
    cos = jnp.concatenate([cos, jnp.ones((ctx_len, ROPE_DIM), F32)], axis=0)
    sin = jnp.concatenate([sin, jnp.zeros((ctx_len, ROPE_DIM), F32)], axis=0)
    cos, sin = jnp.tile(cos, (1, LANES // ROPE_DIM)), jnp.tile(sin, (1, LANES // ROPE_DIM))
    first = (jnp.arange(LANES) % (ROPE_DIM // 2)) < quarter
    return cos, jnp.where(first, -sin, 0.0), jnp.where(first, 0.0, sin)


def _a_head_perm():
    idx = []
    for j in range(A_HEADS // 2):
        for half in range(2):
            head = j + (A_HEADS // 2) * half
            idx.extend(range(head * A_HD, (head + 1) * A_HD))
    return np.asarray(idx)


def _prep_layer(w_in, b_gate_bias, c_g_q, c_g_kv, c_w_uq, c_w_uk, c_w_uv, w_branch):
    d = w_in.shape[0]
    splits = (512, 128, 128, 512, 512, 512, 16, 512, 512, 256, 64, 512, 512, 512, N_BRANCH * d)
    offs = np.concatenate([[0], np.cumsum(splits)])
    part = [w_in[:, offs[i]:offs[i + 1]] for i in range(len(splits))]
    (a_q, a_k, a_v, b_q, b_k, b_v, b_g, b_o, c_q, c_kv, c_kr, d_q, d_k, d_v, gate) = part
    perm = _a_head_perm()
    zeros = lambda n: jnp.zeros((d, n), w_in.dtype)
    uq = c_w_uq.reshape(C_Q_LORA, C_HEADS, C_NOPE + C_ROPE)
    uq = jnp.pad(uq, ((0, 0), (0, 0), (0, C_QK_PAD - C_NOPE - C_ROPE))).reshape(C_Q_LORA, C_HEADS * C_QK_PAD)
    w = {
        'wa': jnp.concatenate([a_q[:, perm], a_k, a_v], axis=1).astype(BF16),
        'wb': jnp.concatenate([b_q, b_k, b_v, b_o], axis=1).astype(BF16),
        'wgt': b_g.T.astype(BF16),
        'wc': jnp.concatenate([c_q, c_kv, c_kr, zeros(LANES - C_ROPE)], axis=1).astype(BF16),
        'wd': jnp.concatenate([d_q, d_k, d_v], axis=1).astype(BF16),
        'wgate': gate.astype(BF16),
        'gbiast': b_gate_bias.reshape(16, 1),
        'cgq': c_g_q.reshape(1, -1),
        'cgkv': c_g_kv.reshape(1, -1),
        'wuq': uq.astype(BF16),
        'wukv': jnp.concatenate([c_w_uk, c_w_uv], axis=1).astype(BF16),
    }
    wbr = jnp.concatenate([w_branch[0:1, perm], w_branch[1:]], axis=0).astype(BF16)
    return w, wbr


def kernel(x, c, ctx, c_ctx, w_mod, b_mod, g_mix_pre, g_mix_post, g_ffn_pre, g_ffn_post, w_in, a_sink,
           b_gate_bias, c_g_q, c_g_kv, c_w_uq, c_w_uk, c_w_uv, d_lam_q1, d_lam_k1, d_lam_q2, d_lam_k2,
           d_g_sub, w_branch, w_out, w_gate_up, w_down):
    b, seq, d = x.shape
    ctx_len = ctx.shape[1]
    depth = w_in.shape[0]
    tables = _rope_tables(seq // GRID_W, ctx_len)

    rows = jnp.concatenate([c, c_ctx[None, :], jnp.zeros((8 - b - 1, d), F32)], axis=0)
    mods = _modulation(rows, w_mod, b_mod).reshape(depth, 8, 6, d)

    assert ctx_len == TOKEN_TILE == WINDOW_TILE, "the context must be exactly one token tile"
    stream = (x, ctx)
    for l in range(depth):
        with_ctx = l < depth - 1
        modall = jnp.stack([jnp.broadcast_to(mods[l, b], (b, 6, d)), mods[l, :b]], axis=1).reshape(2 * b, 6, d)
        w, wbr = _prep_layer(w_in[l], b_gate_bias[l], c_g_q[l], c_g_kv[l], c_w_uq[l], c_w_uk[l],
                             c_w_uv[l], w_branch[l])
        lam_init = 0.8 - 0.6 * math.exp(-0.3 * l)
        lam_vecs = jnp.stack([d_lam_q1[l], d_lam_k1[l], d_lam_q2[l], d_lam_k2[l]], axis=0)
        hidden = w_down.shape[1]

        p = _project(stream, modall, g_mix_pre[l].reshape(1, d), tables, w)
        ya = _window_attention(p['aq'], p['ak'], p['av'], a_sink[l], ctx_len)
        hf, hb = _mlstm(p['bq'], p['bk'], p['bv'], p['bgt'], ctx_len)
        yc = _mla_attention(p['cq'], p['ck'], p['cv'], seq, with_ctx)
        yd = _diff_attention(p['dq'], p['dk'], p['dv'], lam_vecs, d_g_sub[l].reshape(1, -1), lam_init,
                             seq, with_ctx)
        rows = seq + ctx_len if with_ctx else seq
        hidden_out = _merge_ffn(
            p, ya, hf, hb, yc, yd, wbr, w_out[l].astype(BF16), stream, modall,
            g_mix_post[l].reshape(1, d), g_ffn_pre[l].reshape(1, d), g_ffn_post[l].reshape(1, d),
            w_gate_up[l, :, :hidden].astype(BF16), w_gate_up[l, :, hidden:].astype(BF16),
            w_down[l].astype(BF16), seq, rows)
        stream = (hidden_out,)
    return stream[0]
```

```python
import functools
import math

import numpy as np
import jax
import jax.numpy as jnp
from jax import lax
from jax.experimental import pallas as pl
from jax.experimental.pallas import tpu as pltpu

F32 = jnp.float32
BF16 = jnp.bfloat16

GRID_W = 64
ROPE_BASE = 10000.0
NORM_EPS = 1e-6
A_HEADS, A_KV_HEADS, A_HD, A_WINDOW = 8, 2, 64, 128
B_HEADS, B_HD, B_CHUNK = 4, 128, 128
C_HEADS, C_Q_LORA, C_KV_LORA, C_NOPE, C_ROPE, C_VD = 4, 512, 256, 128, 64, 128
D_HEADS, D_HD, D_VD = 4, 64, 128
N_BRANCH = 4
BRANCH_W = 512
ROPE_DIM = 64
C_QK_PAD = 256

LANES = 128
NEG_BIG = -1e30
VMEM_LIMIT = 56 * 1024 * 1024

TOKEN_TILE = 256
WINDOW_TILE = 256
FLASH_TQ = 2048
MLA_TK = 8192
DIFF_TK = 4096
FLASH_SUB = 256
MLSTM_STEP_CHUNKS = 2
LOG2E = math.log2(math.e)


def _resident(shape):
    nd = len(shape)
    return pl.BlockSpec(shape, lambda *_: (0,) * nd, pipeline_mode=pl.Buffered(1))


def _mod_spec(d, ctx_tile):
    return pl.BlockSpec((1, 6, d), lambda bi, t: (bi * 2 + jnp.where(t == ctx_tile, 0, 1), 0, 0))


def _sigmoid(x):
    return 1.0 / (1.0 + jnp.exp(-x))


def _log_sigmoid(x):
    return jnp.minimum(x, 0.0) - jnp.log(1.0 + jnp.exp(-jnp.abs(x)))


def _rms(x, g):
    return x * lax.rsqrt(jnp.mean(x * x, axis=-1, keepdims=True) + NORM_EPS) * g


def _rope(x, cos, sin_lo, sin_hi):
    return x * cos + pltpu.roll(x, LANES - 16, 1) * sin_lo + pltpu.roll(x, 16, 1) * sin_hi


def _dot_nt(a, b):
    return lax.dot_general(a, b, (((1,), (1,)), ((), ())), preferred_element_type=F32)


def _dot_tn(a, b):
    return lax.dot_general(a, b, (((0,), (0,)), ((), ())), preferred_element_type=F32)


def _mod_kernel(c_ref, w_ref, b_ref, o_ref):
    c = c_ref[...]
    s = c * _sigmoid(c)
    o_ref[0] = jnp.dot(s, w_ref[0], preferred_element_type=F32,
                       precision=lax.Precision.HIGHEST) + b_ref[0]


def _modulation(rows, w_mod, b_mod):
    depth, d, n = w_mod.shape
    tn = 1024
    return pl.pallas_call(
        _mod_kernel,
        out_shape=jax.ShapeDtypeStruct((depth, rows.shape[0], n), F32),
        grid=(depth, n // tn),
        in_specs=[pl.BlockSpec(rows.shape, lambda l, j: (0, 0)),
                  pl.BlockSpec((1, d, tn), lambda l, j: (l, 0, j)),
                  pl.BlockSpec((1, 1, tn), lambda l, j: (l, 0, j))],
        out_specs=pl.BlockSpec((1, rows.shape[0], tn), lambda l, j: (l, 0, j)),
        name="modulation",
    )(rows, w_mod, b_mod.reshape(depth, 1, n))


def _stream_specs(stream, tp):
    d = stream[0].shape[2]
    if len(stream) == 1:
        return [pl.BlockSpec((1, tp, d), lambda bi, t: (bi, t, 0))]
    last = stream[0].shape[1] // tp - 1
    return [pl.BlockSpec((1, tp, d), lambda bi, t: (bi, jnp.minimum(t, last), 0)),
            pl.BlockSpec((1, tp, d), lambda bi, t: (bi, 0, 0))]


def _stream_tile(refs, lat_tiles):
    if len(refs) == 1:
        return refs[0][0]
    return jnp.where(pl.program_id(1) == lat_tiles, refs[1][0], refs[0][0])


def _proj_kernel(*refs, n_stream, lat_tiles):
    (mod_ref, gpre_ref, cos_ref, slo_ref, shi_ref,
     wa_ref, wb_ref, wgt_ref, wc_ref, wd_ref, wgate_ref,
     gbiast_ref, cgq_ref, cgkv_ref, wuq_ref, wukv_ref,
     aq_ref, ak_ref, av_ref, bq_ref, bk_ref, bv_ref, bo_ref, bgt_ref,
     cq_ref, ck_ref, cv_ref, dq_ref, dk_ref, dv_ref, gate_ref) = refs[n_stream:]
    x = _stream_tile(refs[:n_stream], lat_tiles)
    y = _rms(x, gpre_ref[...])
    u = (y * (1.0 + mod_ref[0, 1:2, :]) + mod_ref[0, 0:1, :]).astype(BF16)
    cos, slo, shi = cos_ref[...], slo_ref[...], shi_ref[...]

    def rope(v):
        return _rope(v, cos, slo, shi)

    def chunks(v, n):
        return [v[:, i * LANES:(i + 1) * LANES] for i in range(n)]

    pa = jnp.dot(u, wa_ref[...], preferred_element_type=F32)
    for i, v in enumerate(chunks(pa, 4)):
        aq_ref[0, :, i * LANES:(i + 1) * LANES] = (rope(v) * (A_HD ** -0.5 * LOG2E)).astype(BF16)
    ak_ref[0] = rope(pa[:, 512:640]).astype(BF16)
    av_ref[0] = pa[:, 640:768].astype(BF16)

    pb = jnp.dot(u, wb_ref[...], preferred_element_type=F32)
    bq_ref[0] = pb[:, 0:512].astype(BF16)
    bv_ref[0] = pb[:, 512:1024].astype(BF16)
    bo_ref[0] = pb[:, 1024:1536].astype(BF16)
    kg = _dot_nt(wgt_ref[...], u)
    bk_ref[0] = (kg[0:512] * (B_HD ** -0.5)).astype(BF16)
    bgt_ref[0] = kg[512:528] + gbiast_ref[...]

    pc = jnp.dot(u, wc_ref[...], preferred_element_type=F32)
    cqn = _rms(pc[:, 0:C_Q_LORA], cgq_ref[...]).astype(BF16)
    ckvn = _rms(pc[:, C_Q_LORA:C_Q_LORA + C_KV_LORA], cgkv_ref[...]).astype(BF16)
    kr = rope(pc[:, 768:896]).astype(BF16)
    cq = jnp.dot(cqn, wuq_ref[...], preferred_element_type=F32)
    ckv = jnp.dot(ckvn, wukv_ref[...], preferred_element_type=F32)
    c_scale = (C_NOPE + C_ROPE) ** -0.5 * LOG2E
    for h in range(C_HEADS):
        o = h * C_QK_PAD
        cq_ref[0, :, o:o + LANES] = (cq[:, o:o + LANES] * c_scale).astype(BF16)
        cq_ref[0, :, o + LANES:o + 2 * LANES] = (rope(cq[:, o + LANES:o + 2 * LANES]) * c_scale).astype(BF16)
        ck_ref[0, :, o:o + LANES] = ckv[:, h * LANES:(h + 1) * LANES].astype(BF16)
        ck_ref[0, :, o + LANES:o + 2 * LANES] = kr
    cv_ref[0] = ckv[:, 512:1024].astype(BF16)

    pd = jnp.dot(u, wd_ref[...], preferred_element_type=F32)
    for i in range(4):
        sl = slice(i * LANES, (i + 1) * LANES)
        dq_ref[0, :, sl] = (rope(pd[:, sl]) * (D_HD ** -0.5 * LOG2E)).astype(BF16)
        dk_ref[0, :, sl] = rope(pd[:, 512 + i * LANES:512 + (i + 1) * LANES]).astype(BF16)
    dv_ref[0] = pd[:, 1024:1536].astype(BF16)

    for i in range(N_BRANCH):
        sl = slice(i * 1024, (i + 1) * 1024)
        gate_ref[0, :, sl] = jnp.dot(u, wgate_ref[:, sl], preferred_element_type=F32).astype(BF16)


def _project(stream, modall, gpre, tables, w):
    b, _, d = stream[0].shape
    tt = sum(s.shape[1] for s in stream)
    tp = TOKEN_TILE
    nt = tt // tp
    cos, slo, shi = tables

    def tok(width):
        return pl.BlockSpec((1, tp, width), lambda bi, t: (bi, t, 0))

    tab = pl.BlockSpec((tp, LANES), lambda bi, t: (t, 0))
    in_specs = _stream_specs(stream, tp) + [
        _mod_spec(d, nt - 1),
        _resident((1, d)), tab, tab, tab,
        _resident(w['wa'].shape), _resident(w['wb'].shape),
        _resident(w['wgt'].shape), _resident(w['wc'].shape), _resident(w['wd'].shape),
        _resident(w['wgate'].shape), _resident(w['gbiast'].shape),
        _resident(w['cgq'].shape), _resident(w['cgkv'].shape), _resident(w['wuq'].shape),
        _resident(w['wukv'].shape),
    ]
    widths = dict(aq=512, ak=128, av=128, bq=512, bv=512, bo=512,
                  cq=C_HEADS * C_QK_PAD, ck=C_HEADS * C_QK_PAD, cv=512,
                  dq=512, dk=512, dv=512, gate=N_BRANCH * d)
    tokens_on_lanes = dict(bk=(512, BF16), bgt=(16, F32))
    names = ['aq', 'ak', 'av', 'bq', 'bk', 'bv', 'bo', 'bgt',
             'cq', 'ck', 'cv', 'dq', 'dk', 'dv', 'gate']
    out_shape, out_specs = [], []
    for n in names:
        if n in tokens_on_lanes:
            feat, dtype = tokens_on_lanes[n]
            out_shape.append(jax.ShapeDtypeStruct((b, feat, tt), dtype))
            out_specs.append(pl.BlockSpec((1, feat, tp), lambda bi, t: (bi, 0, t)))
        else:
            out_shape.append(jax.ShapeDtypeStruct((b, tt, widths[n]), BF16))
            out_specs.append(tok(widths[n]))
    outs = pl.pallas_call(
        functools.partial(_proj_kernel, n_stream=len(stream), lat_tiles=nt - 1),
        out_shape=out_shape,
        grid=(b, nt),
        in_specs=in_specs,
        out_specs=out_specs,
        compiler_params=pltpu.CompilerParams(
            dimension_semantics=("parallel", "parallel"), vmem_limit_bytes=VMEM_LIMIT),
        name="project",
    )(*stream, modall, gpre, cos, slo, shi, w['wa'], w['wb'], w['wgt'], w['wc'], w['wd'],
      w['wgate'], w['gbiast'], w['cgq'], w['cgkv'], w['wuq'], w['wukv'])
    return dict(zip(names, outs))


def _window_kernel(sink_ref, q_ref, kc_ref, kp_ref, kk_ref, kn_ref,
                   vc_ref, vp_ref, vk_ref, vn_ref, o_ref, *, lat_tiles):
    t = pl.program_id(1)
    hb = WINDOW_TILE // 2
    n_ctx = kc_ref.shape[1]
    group = A_HEADS // A_KV_HEADS
    width = n_ctx + 3 * hb
    kwin = jnp.concatenate([kp_ref[0], kk_ref[0], kn_ref[0]], axis=0)
    vwin = jnp.concatenate([vp_ref[0], vk_ref[0], vn_ref[0]], axis=0)
    prev_ok = jnp.logical_and(t >= 1, t < lat_tiles).astype(jnp.int32)
    cur_ok = (t < lat_tiles).astype(jnp.int32)
    next_ok = (t < lat_tiles - 1).astype(jnp.int32)
    row = lax.broadcasted_iota(jnp.int32, (group * hb, width), 0) & (hb - 1)
    col = lax.broadcasted_iota(jnp.int32, (group * hb, width), 1)
    wcol = col - n_ctx
    near = jnp.abs(wcol - hb - row) <= A_WINDOW
    low = lax.broadcasted_iota(jnp.int32, (hb, LANES), 1) < A_HD
    for blk in range(2):
        k0 = hb * (1 + blk)
        kb = jnp.concatenate([kc_ref[0], kwin[k0:k0 + 3 * hb]], axis=0)
        vb = jnp.concatenate([vc_ref[0], vwin[k0:k0 + 3 * hb]], axis=0)
        vb_ext = jnp.concatenate([vb, jnp.ones_like(vb)], axis=1)
        if blk == 0:
            seg_ok = jnp.where(wcol < hb, prev_ok, cur_ok)
        else:
            seg_ok = jnp.where(wcol < 2 * hb, cur_ok, next_ok)
        valid = jnp.logical_or(col < n_ctx, jnp.logical_and(near, seg_ok > 0))
        rows = slice(blk * hb, (blk + 1) * hb)
        outs = []
        for g in range(A_KV_HEADS):
            keep = low if g == 0 else jnp.logical_not(low)
            qs, sinks = [], []
            for j in range(group):
                qb = q_ref[0, rows, j * LANES:(j + 1) * LANES]
                qs.append(jnp.where(keep, qb, jnp.zeros_like(qb)))
                sinks.append(jnp.full((hb, LANES), sink_ref[j + group * g] * LOG2E, F32))
            sink = jnp.concatenate(sinks, axis=0)
            s = jnp.where(valid, _dot_nt(jnp.concatenate(qs, axis=0), kb), NEG_BIG)
            m = jnp.maximum(jnp.max(s, axis=-1, keepdims=True), sink)
            e = jnp.exp2(s - jnp.tile(m, (1, width // LANES)))
            pv = jnp.dot(e.astype(BF16), vb_ext, preferred_element_type=F32)
            outs.append(pv[:, :LANES] / (pv[:, LANES:] + jnp.exp2(sink - m)))
        for j in range(group):
            o_ref[0, rows, j * LANES:(j + 1) * LANES] = jnp.where(
                low, outs[0][j * hb:(j + 1) * hb], outs[1][j * hb:(j + 1) * hb]).astype(BF16)


def _window_attention(q, k, v, sink, ctx_len):
    b, tt, _ = q.shape
    tw = WINDOW_TILE
    assert ctx_len == tw == 2 * A_WINDOW, "context is one key tile; a query block is one window radius"
    nt = tt // tw
    nl = nt - 1
    kv = lambda f: pl.BlockSpec((1, tw, LANES), lambda bi, t: (bi, f(t), 0))
    specs = [kv(lambda t: nl), kv(lambda t: jnp.maximum(t - 1, 0)), kv(lambda t: t),
             kv(lambda t: jnp.minimum(t + 1, nl - 1))]
    return pl.pallas_call(
        functools.partial(_window_kernel, lat_tiles=nl),
        out_shape=jax.ShapeDtypeStruct((b, tt, 512), BF16),
        grid=(b, nt),
        in_specs=[pl.BlockSpec(memory_space=pltpu.SMEM),
                  pl.BlockSpec((1, tw, 512), lambda bi, t: (bi, t, 0))] + specs + specs,
        out_specs=pl.BlockSpec((1, tw, 512), lambda bi, t: (bi, t, 0)),
        compiler_params=pltpu.CompilerParams(
            dimension_semantics=("parallel", "parallel"), vmem_limit_bytes=VMEM_LIMIT),
        name="window_attention",
    )(sink, q, k, k, k, k, v, v, v, v)


def _bf16_terms(x):
    hi = x.astype(BF16)
    r1 = x - hi.astype(F32)
    mid = r1.astype(BF16)
    return [hi, mid, (r1 - mid.astype(F32)).astype(BF16)]


def _block_diag(a, b):
    return jnp.concatenate([jnp.concatenate([a, jnp.zeros_like(b)], axis=1),
                            jnp.concatenate([jnp.zeros_like(a), b], axis=1)], axis=0)


def _mlstm_kernel(qf_ref, kf_ref, vf_ref, gtf_ref, qb_ref, kb_ref, vb_ref, gtb_ref,
                  hf_ref, hb_ref, c_ref, m_ref):
    step = pl.program_id(1)

    @pl.when(step == 0)
    def _():
        c_ref[...] = jnp.zeros_like(c_ref)
        m_ref[...] = jnp.zeros_like(m_ref)

    L = B_CHUNK
    row = lax.broadcasted_iota(jnp.int32, (L, L), 0)
    col = lax.broadcasted_iota(jnp.int32, (L, L), 1)
    dirs = ((qf_ref, kf_ref, vf_ref, gtf_ref, hf_ref), (qb_ref, kb_ref, vb_ref, gtb_ref, hb_ref))
    for d, (q_ref, kt_ref, v_ref, gt_ref, out_ref) in enumerate(dirs):
        before = (col <= row) if d == 0 else (col >= row)
        cum3 = jnp.tile(before.astype(BF16), (1, 3))
        cum_t3 = jnp.tile(((row <= col) if d == 0 else (row >= col)).astype(BF16), (3, 1))
        last = L - 1 if d == 0 else 0
        n_chunks = q_ref.shape[1] // L
        for c in (range(n_chunks) if d == 0 else reversed(range(n_chunks))):
            rows = slice(c * L, (c + 1) * L)
            gt = gt_ref[0, :, rows]
            lf_all = _log_sigmoid(gt)
            frow_all = jnp.dot(jnp.concatenate(_bf16_terms(lf_all), axis=1), cum_t3, preferred_element_type=F32)
            cf0 = (2 * d + 1) * B_HEADS
            lf_heads = jnp.concatenate([jnp.broadcast_to(lf_all[cf0 + h:cf0 + h + 1, :], (LANES, L))
                                        for h in range(B_HEADS)], axis=0)
            fcol_all = _dot_nt(cum3, jnp.concatenate(_bf16_terms(lf_heads), axis=1))
            for h0 in range(0, B_HEADS, 2):
                pair = (h0, h0 + 1)
                kts = [kt_ref[0, h * B_HD:(h + 1) * B_HD, rows] for h in pair]
                s_pair = jnp.dot(q_ref[0, rows, h0 * B_HD:(h0 + 2) * B_HD], _block_diag(*kts),
                                 preferred_element_type=F32)
                weighted, v_exts, decays, cts = [], [], [], []
                for j, h in enumerate(pair):
                    idx = d * B_HEADS + h
                    ci, cf = (2 * d) * B_HEADS + h, cf0 + h
                    i_row, f_row = gt[ci:ci + 1, :], frow_all[cf:cf + 1, :]
                    f_col = fcol_all[:, h * LANES:(h + 1) * LANES]
                    a_row = i_row - f_row
                    b_last = f_row[:, last:last + 1]
                    m_prev = m_ref[idx][:, 0:1]
                    q = q_ref[0, rows, h * B_HD:(h + 1) * B_HD]
                    v = v_ref[0, rows, h * B_HD:(h + 1) * B_HD]
                    v_ext = jnp.concatenate([v, jnp.ones_like(v)], axis=1)
                    dmat = jnp.where(before, f_col + a_row, -jnp.inf)
                    m_t = jnp.maximum(f_col + m_prev, jnp.max(dmat, axis=-1, keepdims=True))
                    wqk = jnp.exp(dmat - m_t) * s_pair[:, j * L:(j + 1) * L]
                    a = jnp.exp(f_col + m_prev - m_t)
                    ct = c_ref[idx]
                    num_den = jnp.dot(
                        jnp.concatenate([wqk.astype(BF16), (a * q.astype(F32)).astype(BF16)], axis=1),
                        jnp.concatenate([v_ext, ct.astype(BF16)], axis=0), preferred_element_type=F32)
                    hval = num_den[:, :B_HD] / jnp.maximum(jnp.abs(num_den[:, B_HD:]), jnp.exp(-m_t))
                    out_ref[0, rows, h * B_HD:(h + 1) * B_HD] = hval.astype(BF16)
                    g_row = b_last + a_row
                    m_new = jnp.maximum(b_last + m_prev, jnp.max(g_row, axis=-1, keepdims=True))
                    decays.append(jnp.exp(b_last + m_prev - m_new))
                    weighted.append((kts[j].astype(F32) * jnp.exp(g_row - m_new)).astype(BF16))
                    v_exts.append(v_ext)
                    cts.append(ct)
                    m_ref[idx] = jnp.broadcast_to(m_new, (1, LANES))
                update = jnp.dot(_block_diag(*weighted), jnp.concatenate(v_exts, axis=0),
                                 preferred_element_type=F32)
                for j, h in enumerate(pair):
                    c_ref[d * B_HEADS + h] = decays[j] * cts[j] + update[j * B_HD:(j + 1) * B_HD]


def _mlstm(q, k, v, gt, ctx_len):
    b, tt, w = q.shape
    L = MLSTM_STEP_CHUNKS * B_CHUNK
    assert ctx_len % L == 0 and tt % L == 0
    nc = tt // L
    cc = ctx_len // L

    nl = nc - cc

    def fwd(s):
        return jnp.where(s < cc, nl + s, s - cc)

    def bwd(s):
        return nc - 1 - s

    def specs(f):
        tok = pl.BlockSpec((1, L, w), lambda bi, s: (bi, f(s), 0))
        lanes = lambda feat: pl.BlockSpec((1, feat, L), lambda bi, s: (bi, 0, f(s)))
        return [tok, lanes(w), tok, lanes(16)]

    out = jax.ShapeDtypeStruct((b, tt, w), BF16)
    return pl.pallas_call(
        _mlstm_kernel,
        out_shape=[out, out],
        grid=(b, nc),
        in_specs=specs(fwd) + specs(bwd),
        out_specs=[pl.BlockSpec((1, L, w), lambda bi, s: (bi, fwd(s), 0)),
                   pl.BlockSpec((1, L, w), lambda bi, s: (bi, bwd(s), 0))],
        scratch_shapes=[pltpu.VMEM((2 * B_HEADS, B_HD, 2 * B_HD), F32),
                        pltpu.VMEM((2 * B_HEADS, 1, LANES), F32)],
        compiler_params=pltpu.CompilerParams(
            dimension_semantics=("parallel", "arbitrary"), vmem_limit_bytes=VMEM_LIMIT),
        name="mlstm",
    )(q, k, v, gt, q, k, v, gt)


def _softmax_update(q, k, v_ext, m_old, acc):
    s = _dot_nt(q, k)
    m_new = jnp.maximum(m_old, jnp.max(s, axis=-1, keepdims=True))
    p = jnp.exp2(s - jnp.tile(m_new, (1, s.shape[1] // LANES)))
    alpha = jnp.exp2(m_old - m_new)
    acc = jnp.tile(alpha, (1, 2)) * acc + jnp.dot(p.astype(BF16), v_ext, preferred_element_type=F32)
    return m_new, acc


def _attend(qs, stats, k_ref, v_ref):
    vals = [(m_ref[...], acc_ref[...]) for m_ref, acc_ref in stats]
    sub = min(FLASH_SUB, k_ref.shape[1])
    for c in range(k_ref.shape[1] // sub):
        k = k_ref[0, c * sub:(c + 1) * sub, :]
        v = v_ref[0, c * sub:(c + 1) * sub, :]
        v_ext = jnp.concatenate([v, jnp.ones_like(v)], axis=1)
        vals = [_softmax_update(q, k, v_ext, m, acc) for q, (m, acc) in zip(qs, vals)]
    for (m_ref, acc_ref), (m, acc) in zip(stats, vals):
        m_ref[...] = m
        acc_ref[...] = acc


def _flash_init(stats):
    for m_ref, acc_ref in stats:
        m_ref[...] = jnp.full_like(m_ref, NEG_BIG)
        acc_ref[...] = jnp.zeros_like(acc_ref)


def _normalized(acc_ref):
    return acc_ref[:, :LANES] / acc_ref[:, LANES:]


def _mla_queries(q):
    return [q]


def _mla_finish(stats):
    return _normalized(stats[0][1])


def _diff_queries(q):
    low = lax.broadcasted_iota(jnp.int32, q.shape, 1) < D_HD
    zero = jnp.zeros_like(q)
    return [jnp.where(low, q, zero), jnp.where(low, zero, q)]


def _diff_finish(stats, lam_ref, gsub_ref, *, lam_init):
    lq1, lk1, lq2, lk2 = (lam_ref[r:r + 1, :] for r in range(4))
    lam = (jnp.exp(jnp.sum(lq1 * lk1, axis=-1, keepdims=True))
           - jnp.exp(jnp.sum(lq2 * lk2, axis=-1, keepdims=True)) + lam_init)
    o = _normalized(stats[0][1]) - lam * _normalized(stats[1][1])
    return _rms(o, gsub_ref[...]) * (1.0 - lam_init)


def _flash_kernel(*refs, n_extra, n_maps, queries, finish, latent):
    extra, refs = refs[:n_extra], refs[n_extra:]
    if latent:
        q_ref, kc_ref, vc_ref, k_ref, v_ref, o_ref = refs[:6]
        scratch = refs[6:]
    else:
        q_ref, kc_ref, vc_ref, _, o_ref = refs[:5]
        scratch = refs[5:]
    stats = [(scratch[2 * i], scratch[2 * i + 1]) for i in range(n_maps)]
    qs = queries(q_ref[0])
    if not latent:
        _flash_init(stats)
        _attend(qs, stats, kc_ref, vc_ref)
        o_ref[0] = finish(stats, *extra).astype(BF16)
        return
    j = pl.program_id(3)

    @pl.when(j == 0)
    def _():
        _flash_init(stats)
        _attend(qs, stats, kc_ref, vc_ref)

    _attend(qs, stats, k_ref, v_ref)

    @pl.when(j == pl.num_programs(3) - 1)
    def _():
        o_ref[0] = finish(stats, *extra).astype(BF16)


def _full_attention(q, k, v, seq, key_block, heads, qk_width, n_maps, queries, finish, extra, ctx_queries, name):
    b, tt, _ = q.shape
    ctx_len = tt - seq
    cb = seq // ctx_len
    tq, tk = min(FLASH_TQ, seq), min(key_block, seq)
    assert seq % tq == 0 and seq % tk == 0 and seq % ctx_len == 0
    out_shape = jax.ShapeDtypeStruct((b, tt, heads * LANES), BF16)

    def scratch(rows):
        return [pltpu.VMEM((rows, LANES), F32), pltpu.VMEM((rows, 2 * LANES), F32)] * n_maps

    body = functools.partial(_flash_kernel, n_extra=len(extra), n_maps=n_maps, queries=queries, finish=finish)
    whole4 = [pl.BlockSpec(e.shape, lambda bi, h, i, j: (0, 0)) for e in extra]
    y = pl.pallas_call(
        functools.partial(body, latent=True),
        out_shape=out_shape,
        grid=(b, heads, seq // tq, seq // tk),
        in_specs=whole4 + [
            pl.BlockSpec((1, tq, qk_width), lambda bi, h, i, j: (bi, i, h)),
            pl.BlockSpec((1, ctx_len, qk_width), lambda bi, h, i, j: (bi, cb, h)),
            pl.BlockSpec((1, ctx_len, LANES), lambda bi, h, i, j: (bi, cb, h)),
            pl.BlockSpec((1, tk, qk_width), lambda bi, h, i, j: (bi, j, h)),
            pl.BlockSpec((1, tk, LANES), lambda bi, h, i, j: (bi, j, h))],
        out_specs=pl.BlockSpec((1, tq, LANES), lambda bi, h, i, j: (bi, i, h)),
        scratch_shapes=scratch(tq),
        compiler_params=pltpu.CompilerParams(
            dimension_semantics=("parallel", "parallel", "parallel", "arbitrary"),
            vmem_limit_bytes=VMEM_LIMIT),
        name=name,
    )(*extra, q, k, v, k, v)
    if not ctx_queries:
        return y
    whole2 = [pl.BlockSpec(e.shape, lambda bi, h: (0, 0)) for e in extra]
    return pl.pallas_call(
        functools.partial(body, latent=False),
        out_shape=out_shape,
        grid=(b, heads),
        in_specs=whole2 + [
            pl.BlockSpec((1, ctx_len, qk_width), lambda bi, h: (bi, cb, h)),
            pl.BlockSpec((1, ctx_len, qk_width), lambda bi, h: (bi, cb, h)),
            pl.BlockSpec((1, ctx_len, LANES), lambda bi, h: (bi, cb, h)),
            pl.BlockSpec(memory_space=pl.ANY)],
        out_specs=pl.BlockSpec((1, ctx_len, LANES), lambda bi, h: (bi, cb, h)),
        scratch_shapes=scratch(ctx_len),
        input_output_aliases={len(extra) + 3: 0},
        compiler_params=pltpu.CompilerParams(
            dimension_semantics=("parallel", "parallel"), vmem_limit_bytes=VMEM_LIMIT),
        name=name + "_ctx",
    )(*extra, q, k, v, y)


def _mla_attention(q, k, v, seq, ctx_queries):
    return _full_attention(q, k, v, seq, MLA_TK, C_HEADS, C_QK_PAD, 1, _mla_queries, _mla_finish, [],
                           ctx_queries, "mla_attention")


def _diff_attention(q, k, v, lam_vecs, gsub, lam_init, seq, ctx_queries):
    return _full_attention(q, k, v, seq, DIFF_TK, D_HEADS, 2 * D_HD, 2, _diff_queries,
                           functools.partial(_diff_finish, lam_init=lam_init), [lam_vecs, gsub],
                           ctx_queries, "diff_attention")


def _merge_ffn_kernel(*refs, n_stream, lat_tiles):
    (ya_ref, hf_ref, hb_ref, bo_ref, yc_ref, yd_ref, gate_ref, wbr_ref, wout_ref,
     mod_ref, gpost_ref, fpre_ref, fpost_ref, wg_ref, wu_ref, wd_ref, o_ref) = refs[n_stream:]
    d = o_ref.shape[2]
    yb = (_sigmoid(bo_ref[0].astype(F32))
          * (hf_ref[0].astype(F32) + hb_ref[0].astype(F32))).astype(BF16)
    ys = (ya_ref[0], yb, yc_ref[0], yd_ref[0])
    merged = None
    for i, y in enumerate(ys):
        g = _sigmoid(gate_ref[0, :, i * d:(i + 1) * d].astype(F32))
        term = g * jnp.dot(y, wbr_ref[i], preferred_element_type=F32)
        merged = term if merged is None else merged + term
    out = jnp.dot(merged.astype(BF16), wout_ref[...], preferred_element_type=F32)
    x = _stream_tile(refs[:n_stream], lat_tiles) + mod_ref[0, 2:3, :] * _rms(out, gpost_ref[...])
    u = (_rms(x, fpre_ref[...]) * (1.0 + mod_ref[0, 4:5, :]) + mod_ref[0, 3:4, :]).astype(BF16)
    gate = jnp.dot(u, wg_ref[...], preferred_element_type=F32)
    up = jnp.dot(u, wu_ref[...], preferred_element_type=F32)
    act = (gate * _sigmoid(gate) * up).astype(BF16)
    out = jnp.dot(act, wd_ref[...], preferred_element_type=F32)
    o_ref[0] = x + mod_ref[0, 5:6, :] * _rms(out, fpost_ref[...])


def _merge_ffn(p, ya, hf, hb, yc, yd, wbr, wout, stream, modall, gpost, fpre, fpost, wg, wu, wd, seq, rows):
    b, _, d = stream[0].shape
    tp = TOKEN_TILE
    tok = lambda width: pl.BlockSpec((1, tp, width), lambda bi, t: (bi, t, 0))
    vec = _resident((1, d))
    return pl.pallas_call(
        functools.partial(_merge_ffn_kernel, n_stream=len(stream), lat_tiles=seq // tp),
        out_shape=jax.ShapeDtypeStruct((b, rows, d), F32),
        grid=(b, rows // tp),
        in_specs=_stream_specs(stream, tp) + [tok(BRANCH_W)] * 6 + [
            tok(N_BRANCH * d), _resident(wbr.shape), _resident(wout.shape),
            _mod_spec(d, seq // tp), vec, vec, vec,
            _resident(wg.shape), _resident(wu.shape), _resident(wd.shape)],
        out_specs=tok(d),
        compiler_params=pltpu.CompilerParams(
            dimension_semantics=("parallel", "parallel"), vmem_limit_bytes=VMEM_LIMIT),
        name="merge_ffn",
    )(*stream, ya, hf, hb, p['bo'], yc, yd, p['gate'], wbr, wout, modall, gpost, fpre, fpost, wg, wu, wd)


def _rope_tables(n_rows, ctx_len):
    row = jnp.repeat(jnp.arange(n_rows, dtype=F32), GRID_W)
    col = jnp.tile(jnp.arange(GRID_W, dtype=F32), n_rows)
    quarter = ROPE_DIM // 4
    inv_freq = ROPE_BASE ** (-jnp.arange(quarter, dtype=F32) / quarter)
    ang_r, ang_c = row[:, None] * inv_freq, col[:, None] * inv_freq
    ang = jnp.concatenate([ang_r, ang_r, ang_c, ang_c], axis=-1)
    cos, sin = jnp.cos(ang), jnp.sin(ang)
    cos = jnp.concatenate([cos, jnp.ones((ctx_len, ROPE_DIM), F32)], axis=0)
    sin = jnp.concatenate([sin, jnp.zeros((ctx_len, ROPE_DIM), F32)], axis=0)
    cos, sin = jnp.tile(cos, (1, LANES // ROPE_DIM)), jnp.tile(sin, (1, LANES // ROPE_DIM))
    first = (jnp.arange(LANES) % (ROPE_DIM // 2)) < quarter
    return cos, jnp.where(first, -sin, 0.0), jnp.where(first, 0.0, sin)


def _a_head_perm():
    idx = []
    for j in range(A_HEADS // 2):
        for half in range(2):
            head = j + (A_HEADS // 2) * half
            idx.extend(range(head * A_HD, (head + 1) * A_HD))
    return np.asarray(idx)


def _prep_layer(w_in, b_gate_bias, c_g_q, c_g_kv, c_w_uq, c_w_uk, c_w_uv, w_branch):
    d = w_in.shape[0]
    splits = (512, 128, 128, 512, 512, 512, 16, 512, 512, 256, 64, 512, 512, 512, N_BRANCH * d)
    offs = np.concatenate([[0], np.cumsum(splits)])
    part = [w_in[:, offs[i]:offs[i + 1]] for i in range(len(splits))]
    (a_q, a_k, a_v, b_q, b_k, b_v, b_g, b_o, c_q, c_kv, c_kr, d_q, d_k, d_v, gate) = part
    perm = _a_head_perm()
    zeros = lambda n: jnp.zeros((d, n), w_in.dtype)
    uq = c_w_uq.reshape(C_Q_LORA, C_HEADS, C_NOPE + C_ROPE)
    uq = jnp.pad(uq, ((0, 0), (0, 0), (0, C_QK_PAD - C_NOPE - C_ROPE))).reshape(C_Q_LORA, C_HEADS * C_QK_PAD)
    w = {
        'wa': jnp.concatenate([a_q[:, perm], a_k, a_v], axis=1).astype(BF16),
        'wb': jnp.concatenate([b_q, b_v, b_o], axis=1).astype(BF16),
        'wgt': jnp.concatenate([b_k, b_g], axis=1).T.astype(BF16),
        'wc': jnp.concatenate([c_q, c_kv, c_kr, zeros(LANES - C_ROPE)], axis=1).astype(BF16),
        'wd': jnp.concatenate([d_q, d_k, d_v], axis=1).astype(BF16),
        'wgate': gate.astype(BF16),
        'gbiast': b_gate_bias.reshape(16, 1),
        'cgq': c_g_q.reshape(1, -1),
        'cgkv': c_g_kv.reshape(1, -1),
        'wuq': uq.astype(BF16),
        'wukv': jnp.concatenate([c_w_uk, c_w_uv], axis=1).astype(BF16),
    }
    wbr = jnp.concatenate([w_branch[0:1, perm], w_branch[1:]], axis=0).astype(BF16)
    return w, wbr


def kernel(x, c, ctx, c_ctx, w_mod, b_mod, g_mix_pre, g_mix_post, g_ffn_pre, g_ffn_post, w_in, a_sink,
           b_gate_bias, c_g_q, c_g_kv, c_w_uq, c_w_uk, c_w_uv, d_lam_q1, d_lam_k1, d_lam_q2, d_lam_k2,
           d_g_sub, w_branch, w_out, w_gate_up, w_down):
    b, seq, d = x.shape
    ctx_len = ctx.shape[1]
    depth = w_in.shape[0]
    tables = _rope_tables(seq // GRID_W, ctx_len)

    rows = jnp.concatenate([c, c_ctx[None, :], jnp.zeros((8 - b - 1, d), F32)], axis=0)
    mods = _modulation(rows, w_mod, b_mod).reshape(depth, 8, 6, d)

    assert ctx_len == TOKEN_TILE == WINDOW_TILE, "the context must be exactly one token tile"
    stream = (x, ctx)
    for l in range(depth):
        with_ctx = l < depth - 1
        modall = jnp.stack([jnp.broadcast_to(mods[l, b], (b, 6, d)), mods[l, :b]], axis=1).reshape(2 * b, 6, d)
        w, wbr = _prep_layer(w_in[l], b_gate_bias[l], c_g_q[l], c_g_kv[l], c_w_uq[l], c_w_uk[l],
                             c_w_uv[l], w_branch[l])
        lam_init = 0.8 - 0.6 * math.exp(-0.3 * l)
        lam_vecs = jnp.stack([d_lam_q1[l], d_lam_k1[l], d_lam_q2[l], d_lam_k2[l]], axis=0)
        hidden = w_down.shape[1]

        p = _project(stream, modall, g_mix_pre[l].reshape(1, d), tables, w)
        ya = _window_attention(p['aq'], p['ak'], p['av'], a_sink[l], ctx_len)
        hf, hb = _mlstm(p['bq'], p['bk'], p['bv'], p['bgt'], ctx_len)
        yc = _mla_attention(p['cq'], p['ck'], p['cv'], seq, with_ctx)
        yd = _diff_attention(p['dq'], p['dk'], p['dv'], lam_vecs, d_g_sub[l].reshape(1, -1), lam_init,
                             seq, with_ctx)
        rows = seq + ctx_len if with_ctx else seq
        hidden_out = _merge_ffn(
            p, ya, hf, hb, yc, yd, wbr, w_out[l].astype(BF16), stream, modall,
            g_mix_post[l].reshape(1, d), g_ffn_pre[l].reshape(1, d), g_ffn_post[l].reshape(1, d),
            w_gate_up[l, :, :hidden].astype(BF16), w_gate_up[l, :, hidden:].astype(BF16),
            w_down[l].astype(BF16), seq, rows)
        stream = (hidden_out,)
    return stream[0]
```

```python
import functools
import math

import numpy as np
import jax
import jax.numpy as jnp
from jax import lax
from jax.experimental import pallas as pl
from jax.experimental.pallas import tpu as pltpu

F32 = jnp.float32
BF16 = jnp.bfloat16

GRID_W = 64
ROPE_BASE = 10000.0
NORM_EPS = 1e-6
A_HEADS, A_KV_HEADS, A_HD, A_WINDOW = 8, 2, 64, 128
B_HEADS, B_HD, B_CHUNK = 4, 128, 128
C_HEADS, C_Q_LORA, C_KV_LORA, C_NOPE, C_ROPE, C_VD = 4, 512, 256, 128, 64, 128
D_HEADS, D_HD, D_VD = 4, 64, 128
N_BRANCH = 4
BRANCH_W = 512
ROPE_DIM = 64
C_QK_PAD = 256

LANES = 128
NEG_BIG = -1e30
VMEM_LIMIT = 56 * 1024 * 1024

TOKEN_TILE = 256
WINDOW_TILE = 256
FLASH_TQ = 2048
MLA_TK = 16384
DIFF_TK = 8192
FLASH_SUB = 256
MLSTM_STEP_CHUNKS = 2
LOG2E = math.log2(math.e)


def _resident(shape):
    nd = len(shape)
    return pl.BlockSpec(shape, lambda *_: (0,) * nd, pipeline_mode=pl.Buffered(1))


def _mod_spec(d, ctx_tile):
    return pl.BlockSpec((1, 6, d), lambda bi, t: (bi * 2 + jnp.where(t == ctx_tile, 0, 1), 0, 0))


def _sigmoid(x):
    return 1.0 / (1.0 + jnp.exp(-x))


def _log_sigmoid(x):
    return jnp.minimum(x, 0.0) - jnp.log(1.0 + jnp.exp(-jnp.abs(x)))


def _rms(x, g):
    return x * lax.rsqrt(jnp.mean(x * x, axis=-1, keepdims=True) + NORM_EPS) * g


def _rope(x, cos, sin_lo, sin_hi):
    return x * cos + pltpu.roll(x, LANES - 16, 1) * sin_lo + pltpu.roll(x, 16, 1) * sin_hi


def _dot_nt(a, b):
    return lax.dot_general(a, b, (((1,), (1,)), ((), ())), preferred_element_type=F32)


def _dot_tn(a, b):
    return lax.dot_general(a, b, (((0,), (0,)), ((), ())), preferred_element_type=F32)


def _mod_kernel(c_ref, w_ref, b_ref, o_ref):
    c = c_ref[...]
    s = c * _sigmoid(c)
    o_ref[0] = jnp.dot(s, w_ref[0], preferred_element_type=F32,
                       precision=lax.Precision.HIGHEST) + b_ref[0]


def _modulation(rows, w_mod, b_mod):
    depth, d, n = w_mod.shape
    tn = 1024
    return pl.pallas_call(
        _mod_kernel,
        out_shape=jax.ShapeDtypeStruct((depth, rows.shape[0], n), F32),
        grid=(depth, n // tn),
        in_specs=[pl.BlockSpec(rows.shape, lambda l, j: (0, 0)),
                  pl.BlockSpec((1, d, tn), lambda l, j: (l, 0, j)),
                  pl.BlockSpec((1, 1, tn), lambda l, j: (l, 0, j))],
        out_specs=pl.BlockSpec((1, rows.shape[0], tn), lambda l, j: (l, 0, j)),
        name="modulation",
    )(rows, w_mod, b_mod.reshape(depth, 1, n))


def _stream_specs(stream, tp):
    d = stream[0].shape[2]
    if len(stream) == 1:
        return [pl.BlockSpec((1, tp, d), lambda bi, t: (bi, t, 0))]
    last = stream[0].shape[1] // tp - 1
    return [pl.BlockSpec((1, tp, d), lambda bi, t: (bi, jnp.minimum(t, last), 0)),
            pl.BlockSpec((1, tp, d), lambda bi, t: (bi, 0, 0))]


def _stream_tile(refs, lat_tiles):
    if len(refs) == 1:
        return refs[0][0]
    return jnp.where(pl.program_id(1) == lat_tiles, refs[1][0], refs[0][0])


def _proj_kernel(*refs, n_stream, lat_tiles):
    (mod_ref, gpre_ref, cos_ref, slo_ref, shi_ref,
     wa_ref, wb_ref, wgt_ref, wc_ref, wd_ref, wgate_ref,
     gbiast_ref, cgq_ref, cgkv_ref, wuq_ref, wukv_ref,
     aq_ref, ak_ref, av_ref, bq_ref, bk_ref, bv_ref, bo_ref, bgt_ref,
     cq_ref, ck_ref, cv_ref, dq_ref, dk_ref, dv_ref, gate_ref) = refs[n_stream:]
    x = _stream_tile(refs[:n_stream], lat_tiles)
    y = _rms(x, gpre_ref[...])
    u = (y * (1.0 + mod_ref[0, 1:2, :]) + mod_ref[0, 0:1, :]).astype(BF16)
    cos, slo, shi = cos_ref[...], slo_ref[...], shi_ref[...]

    def rope(v):
        return _rope(v, cos, slo, shi)

    def chunks(v, n):
        return [v[:, i * LANES:(i + 1) * LANES] for i in range(n)]

    pa = jnp.dot(u, wa_ref[...], preferred_element_type=F32)
    for i, v in enumerate(chunks(pa, 4)):
        aq_ref[0, :, i * LANES:(i + 1) * LANES] = (rope(v) * (A_HD ** -0.5 * LOG2E)).astype(BF16)
    ak_ref[0] = rope(pa[:, 512:640]).astype(BF16)
    av_ref[0] = pa[:, 640:768].astype(BF16)

    pb = jnp.dot(u, wb_ref[...], preferred_element_type=F32)
    bq_ref[0] = pb[:, 0:512].astype(BF16)
    bv_ref[0] = pb[:, 512:1024].astype(BF16)
    bo_ref[0] = pb[:, 1024:1536].astype(BF16)
    kg = _dot_nt(wgt_ref[...], u)
    bk_ref[0] = (kg[0:512] * (B_HD ** -0.5)).astype(BF16)
    bgt_ref[0] = kg[512:528] + gbiast_ref[...]

    pc = jnp.dot(u, wc_ref[...], preferred_element_type=F32)
    cqn = _rms(pc[:, 0:C_Q_LORA], cgq_ref[...]).astype(BF16)
    ckvn = _rms(pc[:, C_Q_LORA:C_Q_LORA + C_KV_LORA], cgkv_ref[...]).astype(BF16)
    kr = rope(pc[:, 768:896]).astype(BF16)
    cq = jnp.dot(cqn, wuq_ref[...], preferred_element_type=F32)
    ckv = jnp.dot(ckvn, wukv_ref[...], preferred_element_type=F32)
    c_scale = (C_NOPE + C_ROPE) ** -0.5 * LOG2E
    for h in range(C_HEADS):
        o = h * C_QK_PAD
        cq_ref[0, :, o:o + LANES] = (cq[:, o:o + LANES] * c_scale).astype(BF16)
        cq_ref[0, :, o + LANES:o + 2 * LANES] = (rope(cq[:, o + LANES:o + 2 * LANES]) * c_scale).astype(BF16)
        ck_ref[0, :, o:o + LANES] = ckv[:, h * LANES:(h + 1) * LANES].astype(BF16)
        ck_ref[0, :, o + LANES:o + 2 * LANES] = kr
    cv_ref[0] = ckv[:, 512:1024].astype(BF16)

    pd = jnp.dot(u, wd_ref[...], preferred_element_type=F32)
    for i in range(4):
        sl = slice(i * LANES, (i + 1) * LANES)
        dq_ref[0, :, sl] = (rope(pd[:, sl]) * (D_HD ** -0.5 * LOG2E)).astype(BF16)
        dk_ref[0, :, sl] = rope(pd[:, 512 + i * LANES:512 + (i + 1) * LANES]).astype(BF16)
    dv_ref[0] = pd[:, 1024:1536].astype(BF16)

    for i in range(N_BRANCH):
        sl = slice(i * 1024, (i + 1) * 1024)
        gate_ref[0, :, sl] = jnp.dot(u, wgate_ref[:, sl], preferred_element_type=F32).astype(BF16)


def _project(stream, modall, gpre, tables, w):
    b, _, d = stream[0].shape
    tt = sum(s.shape[1] for s in stream)
    tp = TOKEN_TILE
    nt = tt // tp
    cos, slo, shi = tables

    def tok(width):
        return pl.BlockSpec((1, tp, width), lambda bi, t: (bi, t, 0))

    tab = pl.BlockSpec((tp, LANES), lambda bi, t: (t, 0))
    in_specs = _stream_specs(stream, tp) + [
        _mod_spec(d, nt - 1),
        _resident((1, d)), tab, tab, tab,
        _resident(w['wa'].shape), _resident(w['wb'].shape),
        _resident(w['wgt'].shape), _resident(w['wc'].shape), _resident(w['wd'].shape),
        _resident(w['wgate'].shape), _resident(w['gbiast'].shape),
        _resident(w['cgq'].shape), _resident(w['cgkv'].shape), _resident(w['wuq'].shape),
        _resident(w['wukv'].shape),
    ]
    widths = dict(aq=512, ak=128, av=128, bq=512, bv=512, bo=512,
                  cq=C_HEADS * C_QK_PAD, ck=C_HEADS * C_QK_PAD, cv=512,
                  dq=512, dk=512, dv=512, gate=N_BRANCH * d)
    tokens_on_lanes = dict(bk=(512, BF16), bgt=(16, F32))
    names = ['aq', 'ak', 'av', 'bq', 'bk', 'bv', 'bo', 'bgt',
             'cq', 'ck', 'cv', 'dq', 'dk', 'dv', 'gate']
    out_shape, out_specs = [], []
    for n in names:
        if n in tokens_on_lanes:
            feat, dtype = tokens_on_lanes[n]
            out_shape.append(jax.ShapeDtypeStruct((b, feat, tt), dtype))
            out_specs.append(pl.BlockSpec((1, feat, tp), lambda bi, t: (bi, 0, t)))
        else:
            out_shape.append(jax.ShapeDtypeStruct((b, tt, widths[n]), BF16))
            out_specs.append(tok(widths[n]))
    outs = pl.pallas_call(
        functools.partial(_proj_kernel, n_stream=len(stream), lat_tiles=nt - 1),
        out_shape=out_shape,
        grid=(b, nt),
        in_specs=in_specs,
        out_specs=out_specs,
        compiler_params=pltpu.CompilerParams(
            dimension_semantics=("parallel", "parallel"), vmem_limit_bytes=VMEM_LIMIT),
        name="project",
    )(*stream, modall, gpre, cos, slo, shi, w['wa'], w['wb'], w['wgt'], w['wc'], w['wd'],
      w['wgate'], w['gbiast'], w['cgq'], w['cgkv'], w['wuq'], w['wukv'])
    return dict(zip(names, outs))


def _window_kernel(sink_ref, q_ref, kc_ref, kp_ref, kk_ref, kn_ref,
                   vc_ref, vp_ref, vk_ref, vn_ref, o_ref, *, lat_tiles):
    t = pl.program_id(1)
    hb = WINDOW_TILE // 2
    n_ctx = kc_ref.shape[1]
    group = A_HEADS // A_KV_HEADS
    width = n_ctx + 3 * hb
    kwin = jnp.concatenate([kp_ref[0], kk_ref[0], kn_ref[0]], axis=0)
    vwin = jnp.concatenate([vp_ref[0], vk_ref[0], vn_ref[0]], axis=0)
    prev_ok = jnp.logical_and(t >= 1, t < lat_tiles).astype(jnp.int32)
    cur_ok = (t < lat_tiles).astype(jnp.int32)
    next_ok = (t < lat_tiles - 1).astype(jnp.int32)
    row = lax.broadcasted_iota(jnp.int32, (group * hb, width), 0) & (hb - 1)
    col = lax.broadcasted_iota(jnp.int32, (group * hb, width), 1)
    wcol = col - n_ctx
    near = jnp.abs(wcol - hb - row) <= A_WINDOW
    low = lax.broadcasted_iota(jnp.int32, (hb, LANES), 1) < A_HD
    for blk in range(2):
        k0 = hb * (1 + blk)
        kb = jnp.concatenate([kc_ref[0], kwin[k0:k0 + 3 * hb]], axis=0)
        vb = jnp.concatenate([vc_ref[0], vwin[k0:k0 + 3 * hb]], axis=0)
        vb_ext = jnp.concatenate([vb, jnp.ones_like(vb)], axis=1)
        if blk == 0:
            seg_ok = jnp.where(wcol < hb, prev_ok, cur_ok)
        else:
            seg_ok = jnp.where(wcol < 2 * hb, cur_ok, next_ok)
        valid = jnp.logical_or(col < n_ctx, jnp.logical_and(near, seg_ok > 0))
        rows = slice(blk * hb, (blk + 1) * hb)
        outs = []
        for g in range(A_KV_HEADS):
            keep = low if g == 0 else jnp.logical_not(low)
            qs, sinks = [], []
            for j in range(group):
                qb = q_ref[0, rows, j * LANES:(j + 1) * LANES]
                qs.append(jnp.where(keep, qb, jnp.zeros_like(qb)))
                sinks.append(jnp.full((hb, LANES), sink_ref[j + group * g] * LOG2E, F32))
            sink = jnp.concatenate(sinks, axis=0)
            s = jnp.where(valid, _dot_nt(jnp.concatenate(qs, axis=0), kb), NEG_BIG)
            m = jnp.maximum(jnp.max(s, axis=-1, keepdims=True), sink)
            e = jnp.exp2(s - jnp.tile(m, (1, width // LANES)))
            pv = jnp.dot(e.astype(BF16), vb_ext, preferred_element_type=F32)
            outs.append(pv[:, :LANES] / (pv[:, LANES:] + jnp.exp2(sink - m)))
        for j in range(group):
            o_ref[0, rows, j * LANES:(j + 1) * LANES] = jnp.where(
                low, outs[0][j * hb:(j + 1) * hb], outs[1][j * hb:(j + 1) * hb]).astype(BF16)


def _window_attention(q, k, v, sink, ctx_len):
    b, tt, _ = q.shape
    tw = WINDOW_TILE
    assert ctx_len == tw == 2 * A_WINDOW, "context is one key tile; a query block is one window radius"
    nt = tt // tw
    nl = nt - 1
    kv = lambda f: pl.BlockSpec((1, tw, LANES), lambda bi, t: (bi, f(t), 0))
    specs = [kv(lambda t: nl), kv(lambda t: jnp.maximum(t - 1, 0)), kv(lambda t: t),
             kv(lambda t: jnp.minimum(t + 1, nl - 1))]
    return pl.pallas_call(
        functools.partial(_window_kernel, lat_tiles=nl),
        out_shape=jax.ShapeDtypeStruct((b, tt, 512), BF16),
        grid=(b, nt),
        in_specs=[pl.BlockSpec(memory_space=pltpu.SMEM),
                  pl.BlockSpec((1, tw, 512), lambda bi, t: (bi, t, 0))] + specs + specs,
        out_specs=pl.BlockSpec((1, tw, 512), lambda bi, t: (bi, t, 0)),
        compiler_params=pltpu.CompilerParams(
            dimension_semantics=("parallel", "parallel"), vmem_limit_bytes=VMEM_LIMIT),
        name="window_attention",
    )(sink, q, k, k, k, k, v, v, v, v)


def _bf16_terms(x):
    hi = x.astype(BF16)
    r1 = x - hi.astype(F32)
    mid = r1.astype(BF16)
    return [hi, mid, (r1 - mid.astype(F32)).astype(BF16)]


def _block_diag(a, b):
    return jnp.concatenate([jnp.concatenate([a, jnp.zeros_like(b)], axis=1),
                            jnp.concatenate([jnp.zeros_like(a), b], axis=1)], axis=0)


def _mlstm_kernel(qf_ref, kf_ref, vf_ref, gtf_ref, qb_ref, kb_ref, vb_ref, gtb_ref,
                  hf_ref, hb_ref, c_ref, m_ref):
    step = pl.program_id(1)

    @pl.when(step == 0)
    def _():
        c_ref[...] = jnp.zeros_like(c_ref)
        m_ref[...] = jnp.zeros_like(m_ref)

    L = B_CHUNK
    row = lax.broadcasted_iota(jnp.int32, (L, L), 0)
    col = lax.broadcasted_iota(jnp.int32, (L, L), 1)
    dirs = ((qf_ref, kf_ref, vf_ref, gtf_ref, hf_ref), (qb_ref, kb_ref, vb_ref, gtb_ref, hb_ref))
    for d, (q_ref, kt_ref, v_ref, gt_ref, out_ref) in enumerate(dirs):
        before = (col <= row) if d == 0 else (col >= row)
        cum3 = jnp.tile(before.astype(BF16), (1, 3))
        cum_t3 = jnp.tile(((row <= col) if d == 0 else (row >= col)).astype(BF16), (3, 1))
        last = L - 1 if d == 0 else 0
        n_chunks = q_ref.shape[1] // L
        for c in (range(n_chunks) if d == 0 else reversed(range(n_chunks))):
            rows = slice(c * L, (c + 1) * L)
            gt = gt_ref[0, :, rows]
            lf_all = _log_sigmoid(gt)
            frow_all = jnp.dot(jnp.concatenate(_bf16_terms(lf_all), axis=1), cum_t3, preferred_element_type=F32)
            cf0 = (2 * d + 1) * B_HEADS
            lf_heads = jnp.concatenate([jnp.broadcast_to(lf_all[cf0 + h:cf0 + h + 1, :], (LANES, L))
                                        for h in range(B_HEADS)], axis=0)
            fcol_all = _dot_nt(cum3, jnp.concatenate(_bf16_terms(lf_heads), axis=1))
            for h0 in range(0, B_HEADS, 2):
                pair = (h0, h0 + 1)
                kts = [kt_ref[0, h * B_HD:(h + 1) * B_HD, rows] for h in pair]
                s_pair = jnp.dot(q_ref[0, rows, h0 * B_HD:(h0 + 2) * B_HD], _block_diag(*kts),
                                 preferred_element_type=F32)
                weighted, v_exts, decays, cts = [], [], [], []
                for j, h in enumerate(pair):
                    idx = d * B_HEADS + h
                    ci, cf = (2 * d) * B_HEADS + h, cf0 + h
                    i_row, f_row = gt[ci:ci + 1, :], frow_all[cf:cf + 1, :]
                    f_col = fcol_all[:, h * LANES:(h + 1) * LANES]
                    a_row = i_row - f_row
                    b_last = f_row[:, last:last + 1]
                    m_prev = m_ref[idx][:, 0:1]
                    q = q_ref[0, rows, h * B_HD:(h + 1) * B_HD]
                    v = v_ref[0, rows, h * B_HD:(h + 1) * B_HD]
                    v_ext = jnp.concatenate([v, jnp.ones_like(v)], axis=1)
                    dmat = jnp.where(before, f_col + a_row, -jnp.inf)
                    m_t = jnp.maximum(f_col + m_prev, jnp.max(dmat, axis=-1, keepdims=True))
                    wqk = jnp.exp(dmat - m_t) * s_pair[:, j * L:(j + 1) * L]
                    a = jnp.exp(f_col + m_prev - m_t)
                    ct = c_ref[idx]
                    num_den = jnp.dot(
                        jnp.concatenate([wqk.astype(BF16), (a * q.astype(F32)).astype(BF16)], axis=1),
                        jnp.concatenate([v_ext, ct.astype(BF16)], axis=0), preferred_element_type=F32)
                    hval = num_den[:, :B_HD] / jnp.maximum(jnp.abs(num_den[:, B_HD:]), jnp.exp(-m_t))
                    out_ref[0, rows, h * B_HD:(h + 1) * B_HD] = hval.astype(BF16)
                    g_row = b_last + a_row
                    m_new = jnp.maximum(b_last + m_prev, jnp.max(g_row, axis=-1, keepdims=True))
                    decays.append(jnp.exp(b_last + m_prev - m_new))
                    weighted.append((kts[j].astype(F32) * jnp.exp(g_row - m_new)).astype(BF16))
                    v_exts.append(v_ext)
                    cts.append(ct)
                    m_ref[idx] = jnp.broadcast_to(m_new, (1, LANES))
                update = jnp.dot(_block_diag(*weighted), jnp.concatenate(v_exts, axis=0),
                                 preferred_element_type=F32)
                for j, h in enumerate(pair):
                    c_ref[d * B_HEADS + h] = decays[j] * cts[j] + update[j * B_HD:(j + 1) * B_HD]


def _mlstm(q, k, v, gt, ctx_len):
    b, tt, w = q.shape
    L = MLSTM_STEP_CHUNKS * B_CHUNK
    assert ctx_len % L == 0 and tt % L == 0
    nc = tt // L
    cc = ctx_len // L

    nl = nc - cc

    def fwd(s):
        return jnp.where(s < cc, nl + s, s - cc)

    def bwd(s):
        return nc - 1 - s

    def specs(f):
        tok = pl.BlockSpec((1, L, w), lambda bi, s: (bi, f(s), 0))
        lanes = lambda feat: pl.BlockSpec((1, feat, L), lambda bi, s: (bi, 0, f(s)))
        return [tok, lanes(w), tok, lanes(16)]

    out = jax.ShapeDtypeStruct((b, tt, w), BF16)
    return pl.pallas_call(
        _mlstm_kernel,
        out_shape=[out, out],
        grid=(b, nc),
        in_specs=specs(fwd) + specs(bwd),
        out_specs=[pl.BlockSpec((1, L, w), lambda bi, s: (bi, fwd(s), 0)),
                   pl.BlockSpec((1, L, w), lambda bi, s: (bi, bwd(s), 0))],
        scratch_shapes=[pltpu.VMEM((2 * B_HEADS, B_HD, 2 * B_HD), F32),
                        pltpu.VMEM((2 * B_HEADS, 1, LANES), F32)],
        compiler_params=pltpu.CompilerParams(
            dimension_semantics=("parallel", "arbitrary"), vmem_limit_bytes=VMEM_LIMIT),
        name="mlstm",
    )(q, k, v, gt, q, k, v, gt)


def _softmax_update(q, k, v_ext, m_old, acc):
    s = _dot_nt(q, k)
    m_new = jnp.maximum(m_old, jnp.max(s, axis=-1, keepdims=True))
    p = jnp.exp2(s - jnp.tile(m_new, (1, s.shape[1] // LANES)))
    alpha = jnp.exp2(m_old - m_new)
    acc = jnp.tile(alpha, (1, 2)) * acc + jnp.dot(p.astype(BF16), v_ext, preferred_element_type=F32)
    return m_new, acc


def _attend(qs, stats, k_ref, v_ref):
    vals = [(m_ref[...], acc_ref[...]) for m_ref, acc_ref in stats]
    sub = min(FLASH_SUB, k_ref.shape[1])
    for c in range(k_ref.shape[1] // sub):
        k = k_ref[0, c * sub:(c + 1) * sub, :]
        v = v_ref[0, c * sub:(c + 1) * sub, :]
        v_ext = jnp.concatenate([v, jnp.ones_like(v)], axis=1)
        vals = [_softmax_update(q, k, v_ext, m, acc) for q, (m, acc) in zip(qs, vals)]
    for (m_ref, acc_ref), (m, acc) in zip(stats, vals):
        m_ref[...] = m
        acc_ref[...] = acc


def _flash_init(stats):
    for m_ref, acc_ref in stats:
        m_ref[...] = jnp.full_like(m_ref, NEG_BIG)
        acc_ref[...] = jnp.zeros_like(acc_ref)


def _normalized(acc_ref):
    return acc_ref[:, :LANES] / acc_ref[:, LANES:]


def _mla_queries(q):
    return [q]


def _mla_finish(stats):
    return _normalized(stats[0][1])


def _diff_queries(q):
    low = lax.broadcasted_iota(jnp.int32, q.shape, 1) < D_HD
    zero = jnp.zeros_like(q)
    return [jnp.where(low, q, zero), jnp.where(low, zero, q)]


def _diff_finish(stats, lam_ref, gsub_ref, *, lam_init):
    lq1, lk1, lq2, lk2 = (lam_ref[r:r + 1, :] for r in range(4))
    lam = (jnp.exp(jnp.sum(lq1 * lk1, axis=-1, keepdims=True))
           - jnp.exp(jnp.sum(lq2 * lk2, axis=-1, keepdims=True)) + lam_init)
    o = _normalized(stats[0][1]) - lam * _normalized(stats[1][1])
    return _rms(o, gsub_ref[...]) * (1.0 - lam_init)


def _flash_kernel(*refs, n_extra, n_maps, queries, finish, latent):
    extra, refs = refs[:n_extra], refs[n_extra:]
    if latent:
        q_ref, kc_ref, vc_ref, k_ref, v_ref, o_ref = refs[:6]
        scratch = refs[6:]
    else:
        q_ref, kc_ref, vc_ref, _, o_ref = refs[:5]
        scratch = refs[5:]
    stats = [(scratch[2 * i], scratch[2 * i + 1]) for i in range(n_maps)]
    qs = queries(q_ref[0])
    if not latent:
        _flash_init(stats)
        _attend(qs, stats, kc_ref, vc_ref)
        o_ref[0] = finish(stats, *extra).astype(BF16)
        return
    j = pl.program_id(3)

    @pl.when(j == 0)
    def _():
        _flash_init(stats)
        _attend(qs, stats, kc_ref, vc_ref)

    _attend(qs, stats, k_ref, v_ref)

    @pl.when(j == pl.num_programs(3) - 1)
    def _():
        o_ref[0] = finish(stats, *extra).astype(BF16)


def _full_attention(q, k, v, seq, key_block, heads, qk_width, n_maps, queries, finish, extra, ctx_queries, name):
    b, tt, _ = q.shape
    ctx_len = tt - seq
    cb = seq // ctx_len
    tq, tk = min(FLASH_TQ, seq), min(key_block, seq)
    assert seq % tq == 0 and seq % tk == 0 and seq % ctx_len == 0
    out_shape = jax.ShapeDtypeStruct((b, tt, heads * LANES), BF16)

    def scratch(rows):
        return [pltpu.VMEM((rows, LANES), F32), pltpu.VMEM((rows, 2 * LANES), F32)] * n_maps

    body = functools.partial(_flash_kernel, n_extra=len(extra), n_maps=n_maps, queries=queries, finish=finish)
    whole4 = [pl.BlockSpec(e.shape, lambda bi, h, i, j: (0, 0)) for e in extra]
    y = pl.pallas_call(
        functools.partial(body, latent=True),
        out_shape=out_shape,
        grid=(b, heads, seq // tq, seq // tk),
        in_specs=whole4 + [
            pl.BlockSpec((1, tq, qk_width), lambda bi, h, i, j: (bi, i, h)),
            pl.BlockSpec((1, ctx_len, qk_width), lambda bi, h, i, j: (bi, cb, h)),
            pl.BlockSpec((1, ctx_len, LANES), lambda bi, h, i, j: (bi, cb, h)),
            pl.BlockSpec((1, tk, qk_width), lambda bi, h, i, j: (bi, j, h)),
            pl.BlockSpec((1, tk, LANES), lambda bi, h, i, j: (bi, j, h))],
        out_specs=pl.BlockSpec((1, tq, LANES), lambda bi, h, i, j: (bi, i, h)),
        scratch_shapes=scratch(tq),
        compiler_params=pltpu.CompilerParams(
            dimension_semantics=("parallel", "parallel", "parallel", "arbitrary"),
            vmem_limit_bytes=VMEM_LIMIT),
        name=name,
    )(*extra, q, k, v, k, v)
    if not ctx_queries:
        return y
    whole2 = [pl.BlockSpec(e.shape, lambda bi, h: (0, 0)) for e in extra]
    return pl.pallas_call(
        functools.partial(body, latent=False),
        out_shape=out_shape,
        grid=(b, heads),
        in_specs=whole2 + [
            pl.BlockSpec((1, ctx_len, qk_width), lambda bi, h: (bi, cb, h)),
            pl.BlockSpec((1, ctx_len, qk_width), lambda bi, h: (bi, cb, h)),
            pl.BlockSpec((1, ctx_len, LANES), lambda bi, h: (bi, cb, h)),
            pl.BlockSpec(memory_space=pl.ANY)],
        out_specs=pl.BlockSpec((1, ctx_len, LANES), lambda bi, h: (bi, cb, h)),
        scratch_shapes=scratch(ctx_len),
        input_output_aliases={len(extra) + 3: 0},
        compiler_params=pltpu.CompilerParams(
            dimension_semantics=("parallel", "parallel"), vmem_limit_bytes=VMEM_LIMIT),
        name=name + "_ctx",
    )(*extra, q, k, v, y)


def _mla_attention(q, k, v, seq, ctx_queries):
    return _full_attention(q, k, v, seq, MLA_TK, C_HEADS, C_QK_PAD, 1, _mla_queries, _mla_finish, [],
                           ctx_queries, "mla_attention")


def _diff_attention(q, k, v, lam_vecs, gsub, lam_init, seq, ctx_queries):
    return _full_attention(q, k, v, seq, DIFF_TK, D_HEADS, 2 * D_HD, 2, _diff_queries,
                           functools.partial(_diff_finish, lam_init=lam_init), [lam_vecs, gsub],
                           ctx_queries, "diff_attention")


def _merge_ffn_kernel(*refs, n_stream, lat_tiles):
    (ya_ref, hf_ref, hb_ref, bo_ref, yc_ref, yd_ref, gate_ref, wbr_ref, wout_ref,
     mod_ref, gpost_ref, fpre_ref, fpost_ref, wg_ref, wu_ref, wd_ref, o_ref) = refs[n_stream:]
    d = o_ref.shape[2]
    yb = (_sigmoid(bo_ref[0].astype(F32))
          * (hf_ref[0].astype(F32) + hb_ref[0].astype(F32))).astype(BF16)
    ys = (ya_ref[0], yb, yc_ref[0], yd_ref[0])
    merged = None
    for i, y in enumerate(ys):
        g = _sigmoid(gate_ref[0, :, i * d:(i + 1) * d].astype(F32))
        term = g * jnp.dot(y, wbr_ref[i], preferred_element_type=F32)
        merged = term if merged is None else merged + term
    out = jnp.dot(merged.astype(BF16), wout_ref[...], preferred_element_type=F32)
    x = _stream_tile(refs[:n_stream], lat_tiles) + mod_ref[0, 2:3, :] * _rms(out, gpost_ref[...])
    u = (_rms(x, fpre_ref[...]) * (1.0 + mod_ref[0, 4:5, :]) + mod_ref[0, 3:4, :]).astype(BF16)
    gate = jnp.dot(u, wg_ref[...], preferred_element_type=F32)
    up = jnp.dot(u, wu_ref[...], preferred_element_type=F32)
    act = (gate * _sigmoid(gate) * up).astype(BF16)
    out = jnp.dot(act, wd_ref[...], preferred_element_type=F32)
    o_ref[0] = x + mod_ref[0, 5:6, :] * _rms(out, fpost_ref[...])


def _merge_ffn(p, ya, hf, hb, yc, yd, wbr, wout, stream, modall, gpost, fpre, fpost, wg, wu, wd, seq, rows):
    b, _, d = stream[0].shape
    tp = TOKEN_TILE
    tok = lambda width: pl.BlockSpec((1, tp, width), lambda bi, t: (bi, t, 0))
    vec = _resident((1, d))
    return pl.pallas_call(
        functools.partial(_merge_ffn_kernel, n_stream=len(stream), lat_tiles=seq // tp),
        out_shape=jax.ShapeDtypeStruct((b, rows, d), F32),
        grid=(b, rows // tp),
        in_specs=_stream_specs(stream, tp) + [tok(BRANCH_W)] * 6 + [
            tok(N_BRANCH * d), _resident(wbr.shape), _resident(wout.shape),
            _mod_spec(d, seq // tp), vec, vec, vec,
            _resident(wg.shape), _resident(wu.shape), _resident(wd.shape)],
        out_specs=tok(d),
        compiler_params=pltpu.CompilerParams(
            dimension_semantics=("parallel", "parallel"), vmem_limit_bytes=VMEM_LIMIT),
        name="merge_ffn",
    )(*stream, ya, hf, hb, p['bo'], yc, yd, p['gate'], wbr, wout, modall, gpost, fpre, fpost, wg, wu, wd)


def _rope_tables(n_rows, ctx_len):
    row = jnp.repeat(jnp.arange(n_rows, dtype=F32), GRID_W)
    col = jnp.tile(jnp.arange(GRID_W, dtype=F32), n_rows)
    quarter = ROPE_DIM // 4
    inv_freq = ROPE_BASE ** (-jnp.arange(quarter, dtype=F32) / quarter)
    ang_r, ang_c = row[:, None] * inv_freq, col[:, None] * inv_freq
    ang = jnp.concatenate([ang_r, ang_r, ang_c, ang_c], axis=-1)
    cos, sin = jnp.cos(ang), jnp.sin(ang)
    cos = jnp.concatenate([cos, jnp.ones((ctx_len, ROPE_DIM), F32)], axis=0)
    sin = jnp.concatenate([sin, jnp.zeros((ctx_len, ROPE_DIM), F32)], axis=0)
    cos, sin = jnp.tile(cos, (1, LANES // ROPE_DIM)), jnp.tile(sin, (1, LANES // ROPE_DIM))
    first = (jnp.arange(LANES) % (ROPE_DIM // 2)) < quarter
    return cos, jnp.where(first, -sin, 0.0), jnp.where(first, 0.0, sin)


def _a_head_perm():
    idx = []
    for j in range(A_HEADS // 2):
        for half in range(2):
            head = j + (A_HEADS // 2) * half
            idx.extend(range(head * A_HD, (head + 1) * A_HD))
    return np.asarray(idx)


def _prep_layer(w_in, b_gate_bias, c_g_q, c_g_kv, c_w_uq, c_w_uk, c_w_uv, w_branch):
    d = w_in.shape[0]
    splits = (512, 128, 128, 512, 512, 512, 16, 512, 512, 256, 64, 512, 512, 512, N_BRANCH * d)
    offs = np.concatenate([[0], np.cumsum(splits)])
    part = [w_in[:, offs[i]:offs[i + 1]] for i in range(len(splits))]
    (a_q, a_k, a_v, b_q, b_k, b_v, b_g, b_o, c_q, c_kv, c_kr, d_q, d_k, d_v, gate) = part
    perm = _a_head_perm()
    zeros = lambda n: jnp.zeros((d, n), w_in.dtype)
    uq = c_w_uq.reshape(C_Q_LORA, C_HEADS, C_NOPE + C_ROPE)
    uq = jnp.pad(uq, ((0, 0), (0, 0), (0, C_QK_PAD - C_NOPE - C_ROPE))).reshape(C_Q_LORA, C_HEADS * C_QK_PAD)
    w = {
        'wa': jnp.concatenate([a_q[:, perm], a_k, a_v], axis=1).astype(BF16),
        'wb': jnp.concatenate([b_q, b_v, b_o], axis=1).astype(BF16),
        'wgt': jnp.concatenate([b_k, b_g], axis=1).T.astype(BF16),
        'wc': jnp.concatenate([c_q, c_kv, c_kr, zeros(LANES - C_ROPE)], axis=1).astype(BF16),
        'wd': jnp.concatenate([d_q, d_k, d_v], axis=1).astype(BF16),
        'wgate': gate.astype(BF16),
        'gbiast': b_gate_bias.reshape(16, 1),
        'cgq': c_g_q.reshape(1, -1),
        'cgkv': c_g_kv.reshape(1, -1),
        'wuq': uq.astype(BF16),
        'wukv': jnp.concatenate([c_w_uk, c_w_uv], axis=1).astype(BF16),
    }
    wbr = jnp.concatenate([w_branch[0:1, perm], w_branch[1:]], axis=0).astype(BF16)
    return w, wbr


def kernel(x, c, ctx, c_ctx, w_mod, b_mod, g_mix_pre, g_mix_post, g_ffn_pre, g_ffn_post, w_in, a_sink,
           b_gate_bias, c_g_q, c_g_kv, c_w_uq, c_w_uk, c_w_uv, d_lam_q1, d_lam_k1, d_lam_q2, d_lam_k2,
           d_g_sub, w_branch, w_out, w_gate_up, w_down):
    b, seq, d = x.shape
    ctx_len = ctx.shape[1]
    depth = w_in.shape[0]
    tables = _rope_tables(seq // GRID_W, ctx_len)

    rows = jnp.concatenate([c, c_ctx[None, :], jnp.zeros((8 - b - 1, d), F32)], axis=0)
    mods = _modulation(rows, w_mod, b_mod).reshape(depth, 8, 6, d)

    assert ctx_len == TOKEN_TILE == WINDOW_TILE, "the context must be exactly one token tile"
    stream = (x, ctx)
    for l in range(depth):
        with_ctx = l < depth - 1
        modall = jnp.stack([jnp.broadcast_to(mods[l, b], (b, 6, d)), mods[l, :b]], axis=1).reshape(2 * b, 6, d)
        w, wbr = _prep_layer(w_in[l], b_gate_bias[l], c_g_q[l], c_g_kv[l], c_w_uq[l], c_w_uk[l],
                             c_w_uv[l], w_branch[l])
        lam_init = 0.8 - 0.6 * math.exp(-0.3 * l)
        lam_vecs = jnp.stack([d_lam_q1[l], d_lam_k1[l], d_lam_q2[l], d_lam_k2[l]], axis=0)
        hidden = w_down.shape[1]

        p = _project(stream, modall, g_mix_pre[l].reshape(1, d), tables, w)
        ya = _window_attention(p['aq'], p['ak'], p['av'], a_sink[l], ctx_len)
        hf, hb = _mlstm(p['bq'], p['bk'], p['bv'], p['bgt'], ctx_len)
        yc = _mla_attention(p['cq'], p['ck'], p['cv'], seq, with_ctx)
        yd = _diff_attention(p['dq'], p['dk'], p['dv'], lam_vecs, d_g_sub[l].reshape(1, -1), lam_init,
                             seq, with_ctx)
        rows = seq + ctx_len if with_ctx else seq
        hidden_out = _merge_ffn(
            p, ya, hf, hb, yc, yd, wbr, w_out[l].astype(BF16), stream, modall,
            g_mix_post[l].reshape(1, d), g_ffn_pre[l].reshape(1, d), g_ffn_post[l].reshape(1, d),
            w_gate_up[l, :, :hidden].astype(BF16), w_gate_up[l, :, hidden:].astype(BF16),
            w_down[l].astype(BF16), seq, rows)
        stream = (hidden_out,)
    return stream[0]
```

```python
import functools
import math

import numpy as np
import jax
import jax.numpy as jnp
from jax import lax
from jax.experimental import pallas as pl
from jax.experimental.pallas import tpu as pltpu

F32 = jnp.float32
BF16 = jnp.bfloat16

GRID_W = 64
ROPE_BASE = 10000.0
NORM_EPS = 1e-6
A_HEADS, A_KV_HEADS, A_HD, A_WINDOW = 8, 2, 64, 128
B_HEADS, B_HD, B_CHUNK = 4, 128, 128
C_HEADS, C_Q_LORA, C_KV_LORA, C_NOPE, C_ROPE, C_VD = 4, 512, 256, 128, 64, 128
D_HEADS, D_HD, D_VD = 4, 64, 128
N_BRANCH = 4
BRANCH_W = 512
ROPE_DIM = 64
C_QK_PAD = 256

LANES = 128
NEG_BIG = -1e30
VMEM_LIMIT = 56 * 1024 * 1024

TOKEN_TILE = 256
WINDOW_TILE = 256
FLASH_TQ = 2048
MLA_TK = 8192
DIFF_TK = 4096
FLASH_SUB = 256
MLSTM_STEP_CHUNKS = 2
LOG2E = math.log2(math.e)


def _resident(shape):
    nd = len(shape)
    return pl.BlockSpec(shape, lambda *_: (0,) * nd, pipeline_mode=pl.Buffered(1))


def _mod_spec(d, ctx_tile):
    return pl.BlockSpec((1, 6, d), lambda bi, t: (bi * 2 + jnp.where(t == ctx_tile, 0, 1), 0, 0))


def _sigmoid(x):
    return 1.0 / (1.0 + jnp.exp(-x))


def _log_sigmoid(x):
    return jnp.minimum(x, 0.0) - jnp.log(1.0 + jnp.exp(-jnp.abs(x)))


def _rms(x, g):
    return x * lax.rsqrt(jnp.mean(x * x, axis=-1, keepdims=True) + NORM_EPS) * g


def _rope(x, cos, sin_lo, sin_hi):
    return x * cos + pltpu.roll(x, LANES - 16, 1) * sin_lo + pltpu.roll(x, 16, 1) * sin_hi


def _dot_nt(a, b):
    return lax.dot_general(a, b, (((1,), (1,)), ((), ())), preferred_element_type=F32)


def _dot_tn(a, b):
    return lax.dot_general(a, b, (((0,), (0,)), ((), ())), preferred_element_type=F32)


def _mod_kernel(c_ref, w_ref, b_ref, o_ref):
    c = c_ref[...]
    s = c * _sigmoid(c)
    o_ref[0] = jnp.dot(s, w_ref[0], preferred_element_type=F32,
                       precision=lax.Precision.HIGHEST) + b_ref[0]


def _modulation(rows, w_mod, b_mod):
    depth, d, n = w_mod.shape
    tn = 1024
    return pl.pallas_call(
        _mod_kernel,
        out_shape=jax.ShapeDtypeStruct((depth, rows.shape[0], n), F32),
        grid=(depth, n // tn),
        in_specs=[pl.BlockSpec(rows.shape, lambda l, j: (0, 0)),
                  pl.BlockSpec((1, d, tn), lambda l, j: (l, 0, j)),
                  pl.BlockSpec((1, 1, tn), lambda l, j: (l, 0, j))],
        out_specs=pl.BlockSpec((1, rows.shape[0], tn), lambda l, j: (l, 0, j)),
        name="modulation",
    )(rows, w_mod, b_mod.reshape(depth, 1, n))


def _stream_specs(stream, tp):
    d = stream[0].shape[2]
    if len(stream) == 1:
        return [pl.BlockSpec((1, tp, d), lambda bi, t: (bi, t, 0))]
    last = stream[0].shape[1] // tp - 1
    return [pl.BlockSpec((1, tp, d), lambda bi, t: (bi, jnp.minimum(t, last), 0)),
            pl.BlockSpec((1, tp, d), lambda bi, t: (bi, 0, 0))]


def _stream_tile(refs, lat_tiles):
    if len(refs) == 1:
        return refs[0][0]
    return jnp.where(pl.program_id(1) == lat_tiles, refs[1][0], refs[0][0])


def _proj_kernel(*refs, n_stream, lat_tiles):
    (mod_ref, gpre_ref, cos_ref, slo_ref, shi_ref,
     wa_ref, wb_ref, wgt_ref, wc_ref, wd_ref, wgate_ref,
     gbiast_ref, cgq_ref, cgkv_ref, wuq_ref, wukv_ref,
     aq_ref, ak_ref, av_ref, bq_ref, bk_ref, bv_ref, bo_ref, bgt_ref,
     cq_ref, ck_ref, cv_ref, dq_ref, dk_ref, dv_ref, gate_ref) = refs[n_stream:]
    x = _stream_tile(refs[:n_stream], lat_tiles)
    y = _rms(x, gpre_ref[...])
    u = (y * (1.0 + mod_ref[0, 1:2, :]) + mod_ref[0, 0:1, :]).astype(BF16)
    cos, slo, shi = cos_ref[...], slo_ref[...], shi_ref[...]

    def rope(v):
        return _rope(v, cos, slo, shi)

    def chunks(v, n):
        return [v[:, i * LANES:(i + 1) * LANES] for i in range(n)]

    pa = jnp.dot(u, wa_ref[...], preferred_element_type=F32)
    for i, v in enumerate(chunks(pa, 4)):
        aq_ref[0, :, i * LANES:(i + 1) * LANES] = (rope(v) * (A_HD ** -0.5 * LOG2E)).astype(BF16)
    ak_ref[0] = rope(pa[:, 512:640]).astype(BF16)
    av_ref[0] = pa[:, 640:768].astype(BF16)

    pb = jnp.dot(u, wb_ref[...], preferred_element_type=F32)
    bq_ref[0] = pb[:, 0:512].astype(BF16)
    bv_ref[0] = pb[:, 512:1024].astype(BF16)
    bo_ref[0] = _sigmoid(pb[:, 1024:1536]).astype(BF16)
    kg = _dot_nt(wgt_ref[...], u)
    bk_ref[0] = (kg[0:512] * (B_HD ** -0.5)).astype(BF16)
    bgt_ref[0] = kg[512:528] + gbiast_ref[...]

    pc = jnp.dot(u, wc_ref[...], preferred_element_type=F32)
    cqn = _rms(pc[:, 0:C_Q_LORA], cgq_ref[...]).astype(BF16)
    ckvn = _rms(pc[:, C_Q_LORA:C_Q_LORA + C_KV_LORA], cgkv_ref[...]).astype(BF16)
    kr = rope(pc[:, 768:896]).astype(BF16)
    cq = jnp.dot(cqn, wuq_ref[...], preferred_element_type=F32)
    ckv = jnp.dot(ckvn, wukv_ref[...], preferred_element_type=F32)
    c_scale = (C_NOPE + C_ROPE) ** -0.5 * LOG2E
    for h in range(C_HEADS):
        o = h * C_QK_PAD
        cq_ref[0, :, o:o + LANES] = (cq[:, o:o + LANES] * c_scale).astype(BF16)
        cq_ref[0, :, o + LANES:o + 2 * LANES] = (rope(cq[:, o + LANES:o + 2 * LANES]) * c_scale).astype(BF16)
        ck_ref[0, :, o:o + LANES] = ckv[:, h * LANES:(h + 1) * LANES].astype(BF16)
        ck_ref[0, :, o + LANES:o + 2 * LANES] = kr
    cv_ref[0] = ckv[:, 512:1024].astype(BF16)

    pd = jnp.dot(u, wd_ref[...], preferred_element_type=F32)
    for i in range(4):
        sl = slice(i * LANES, (i + 1) * LANES)
        dq_ref[0, :, sl] = (rope(pd[:, sl]) * (D_HD ** -0.5 * LOG2E)).astype(BF16)
        dk_ref[0, :, sl] = rope(pd[:, 512 + i * LANES:512 + (i + 1) * LANES]).astype(BF16)
    dv_ref[0] = pd[:, 1024:1536].astype(BF16)

    for i in range(N_BRANCH):
        sl = slice(i * 1024, (i + 1) * 1024)
        gate_ref[0, :, sl] = _sigmoid(jnp.dot(u, wgate_ref[:, sl], preferred_element_type=F32)).astype(BF16)


def _project(stream, modall, gpre, tables, w):
    b, _, d = stream[0].shape
    tt = sum(s.shape[1] for s in stream)
    tp = TOKEN_TILE
    nt = tt // tp
    cos, slo, shi = tables

    def tok(width):
        return pl.BlockSpec((1, tp, width), lambda bi, t: (bi, t, 0))

    tab = pl.BlockSpec((tp, LANES), lambda bi, t: (t, 0))
    in_specs = _stream_specs(stream, tp) + [
        _mod_spec(d, nt - 1),
        _resident((1, d)), tab, tab, tab,
        _resident(w['wa'].shape), _resident(w['wb'].shape),
        _resident(w['wgt'].shape), _resident(w['wc'].shape), _resident(w['wd'].shape),
        _resident(w['wgate'].shape), _resident(w['gbiast'].shape),
        _resident(w['cgq'].shape), _resident(w['cgkv'].shape), _resident(w['wuq'].shape),
        _resident(w['wukv'].shape),
    ]
    widths = dict(aq=512, ak=128, av=128, bq=512, bv=512, bo=512,
                  cq=C_HEADS * C_QK_PAD, ck=C_HEADS * C_QK_PAD, cv=512,
                  dq=512, dk=512, dv=512, gate=N_BRANCH * d)
    tokens_on_lanes = dict(bk=(512, BF16), bgt=(16, F32))
    names = ['aq', 'ak', 'av', 'bq', 'bk', 'bv', 'bo', 'bgt',
             'cq', 'ck', 'cv', 'dq', 'dk', 'dv', 'gate']
    out_shape, out_specs = [], []
    for n in names:
        if n in tokens_on_lanes:
            feat, dtype = tokens_on_lanes[n]
            out_shape.append(jax.ShapeDtypeStruct((b, feat, tt), dtype))
            out_specs.append(pl.BlockSpec((1, feat, tp), lambda bi, t: (bi, 0, t)))
        else:
            out_shape.append(jax.ShapeDtypeStruct((b, tt, widths[n]), BF16))
            out_specs.append(tok(widths[n]))
    outs = pl.pallas_call(
        functools.partial(_proj_kernel, n_stream=len(stream), lat_tiles=nt - 1),
        out_shape=out_shape,
        grid=(b, nt),
        in_specs=in_specs,
        out_specs=out_specs,
        compiler_params=pltpu.CompilerParams(
            dimension_semantics=("parallel", "parallel"), vmem_limit_bytes=VMEM_LIMIT),
        name="project",
    )(*stream, modall, gpre, cos, slo, shi, w['wa'], w['wb'], w['wgt'], w['wc'], w['wd'],
      w['wgate'], w['gbiast'], w['cgq'], w['cgkv'], w['wuq'], w['wukv'])
    return dict(zip(names, outs))


def _window_kernel(sink_ref, q_ref, kc_ref, kp_ref, kk_ref, kn_ref,
                   vc_ref, vp_ref, vk_ref, vn_ref, o_ref, *, lat_tiles):
    t = pl.program_id(1)
    hb = WINDOW_TILE // 2
    n_ctx = kc_ref.shape[1]
    group = A_HEADS // A_KV_HEADS
    width = n_ctx + 3 * hb
    kwin = jnp.concatenate([kp_ref[0], kk_ref[0], kn_ref[0]], axis=0)
    vwin = jnp.concatenate([vp_ref[0], vk_ref[0], vn_ref[0]], axis=0)
    prev_ok = jnp.logical_and(t >= 1, t < lat_tiles).astype(jnp.int32)
    cur_ok = (t < lat_tiles).astype(jnp.int32)
    next_ok = (t < lat_tiles - 1).astype(jnp.int32)
    row = lax.broadcasted_iota(jnp.int32, (group * hb, width), 0) & (hb - 1)
    col = lax.broadcasted_iota(jnp.int32, (group * hb, width), 1)
    wcol = col - n_ctx
    near = jnp.abs(wcol - hb - row) <= A_WINDOW
    low = lax.broadcasted_iota(jnp.int32, (hb, LANES), 1) < A_HD
    for blk in range(2):
        k0 = hb * (1 + blk)
        kb = jnp.concatenate([kc_ref[0], kwin[k0:k0 + 3 * hb]], axis=0)
        vb = jnp.concatenate([vc_ref[0], vwin[k0:k0 + 3 * hb]], axis=0)
        vb_ext = jnp.concatenate([vb, jnp.ones_like(vb)], axis=1)
        if blk == 0:
            seg_ok = jnp.where(wcol < hb, prev_ok, cur_ok)
        else:
            seg_ok = jnp.where(wcol < 2 * hb, cur_ok, next_ok)
        valid = jnp.logical_or(col < n_ctx, jnp.logical_and(near, seg_ok > 0))
        rows = slice(blk * hb, (blk + 1) * hb)
        outs = []
        for g in range(A_KV_HEADS):
            keep = low if g == 0 else jnp.logical_not(low)
            qs, sinks = [], []
            for j in range(group):
                qb = q_ref[0, rows, j * LANES:(j + 1) * LANES]
                qs.append(jnp.where(keep, qb, jnp.zeros_like(qb)))
                sinks.append(jnp.full((hb, LANES), sink_ref[j + group * g] * LOG2E, F32))
            sink = jnp.concatenate(sinks, axis=0)
            s = jnp.where(valid, _dot_nt(jnp.concatenate(qs, axis=0), kb), NEG_BIG)
            m = jnp.maximum(jnp.max(s, axis=-1, keepdims=True), sink)
            e = jnp.exp2(s - jnp.tile(m, (1, width // LANES)))
            pv = jnp.dot(e.astype(BF16), vb_ext, preferred_element_type=F32)
            outs.append(pv[:, :LANES] / (pv[:, LANES:] + jnp.exp2(sink - m)))
        for j in range(group):
            o_ref[0, rows, j * LANES:(j + 1) * LANES] = jnp.where(
                low, outs[0][j * hb:(j + 1) * hb], outs[1][j * hb:(j + 1) * hb]).astype(BF16)


def _window_attention(q, k, v, sink, ctx_len):
    b, tt, _ = q.shape
    tw = WINDOW_TILE
    assert ctx_len == tw == 2 * A_WINDOW, "context is one key tile; a query block is one window radius"
    nt = tt // tw
    nl = nt - 1
    kv = lambda f: pl.BlockSpec((1, tw, LANES), lambda bi, t: (bi, f(t), 0))
    specs = [kv(lambda t: nl), kv(lambda t: jnp.maximum(t - 1, 0)), kv(lambda t: t),
             kv(lambda t: jnp.minimum(t + 1, nl - 1))]
    return pl.pallas_call(
        functools.partial(_window_kernel, lat_tiles=nl),
        out_shape=jax.ShapeDtypeStruct((b, tt, 512), BF16),
        grid=(b, nt),
        in_specs=[pl.BlockSpec(memory_space=pltpu.SMEM),
                  pl.BlockSpec((1, tw, 512), lambda bi, t: (bi, t, 0))] + specs + specs,
        out_specs=pl.BlockSpec((1, tw, 512), lambda bi, t: (bi, t, 0)),
        compiler_params=pltpu.CompilerParams(
            dimension_semantics=("parallel", "parallel"), vmem_limit_bytes=VMEM_LIMIT),
        name="window_attention",
    )(sink, q, k, k, k, k, v, v, v, v)


def _bf16_terms(x):
    hi = x.astype(BF16)
    r1 = x - hi.astype(F32)
    mid = r1.astype(BF16)
    return [hi, mid, (r1 - mid.astype(F32)).astype(BF16)]


def _block_diag(a, b):
    return jnp.concatenate([jnp.concatenate([a, jnp.zeros_like(b)], axis=1),
                            jnp.concatenate([jnp.zeros_like(a), b], axis=1)], axis=0)


def _mlstm_kernel(qf_ref, kf_ref, vf_ref, gtf_ref, qb_ref, kb_ref, vb_ref, gtb_ref,
                  hf_ref, hb_ref, c_ref, m_ref):
    step = pl.program_id(1)

    @pl.when(step == 0)
    def _():
        c_ref[...] = jnp.zeros_like(c_ref)
        m_ref[...] = jnp.zeros_like(m_ref)

    L = B_CHUNK
    row = lax.broadcasted_iota(jnp.int32, (L, L), 0)
    col = lax.broadcasted_iota(jnp.int32, (L, L), 1)
    dirs = ((qf_ref, kf_ref, vf_ref, gtf_ref, hf_ref), (qb_ref, kb_ref, vb_ref, gtb_ref, hb_ref))
    for d, (q_ref, kt_ref, v_ref, gt_ref, out_ref) in enumerate(dirs):
        before = (col <= row) if d == 0 else (col >= row)
        cum3 = jnp.tile(before.astype(BF16), (1, 3))
        cum_t3 = jnp.tile(((row <= col) if d == 0 else (row >= col)).astype(BF16), (3, 1))
        last = L - 1 if d == 0 else 0
        n_chunks = q_ref.shape[1] // L
        for c in (range(n_chunks) if d == 0 else reversed(range(n_chunks))):
            rows = slice(c * L, (c + 1) * L)
            gt = gt_ref[0, :, rows]
            lf_all = _log_sigmoid(gt)
            frow_all = jnp.dot(jnp.concatenate(_bf16_terms(lf_all), axis=1), cum_t3, preferred_element_type=F32)
            cf0 = (2 * d + 1) * B_HEADS
            lf_heads = jnp.concatenate([jnp.broadcast_to(lf_all[cf0 + h:cf0 + h + 1, :], (LANES, L))
                                        for h in range(B_HEADS)], axis=0)
            fcol_all = _dot_nt(cum3, jnp.concatenate(_bf16_terms(lf_heads), axis=1))
            for h0 in range(0, B_HEADS, 2):
                pair = (h0, h0 + 1)
                kts = [kt_ref[0, h * B_HD:(h + 1) * B_HD, rows] for h in pair]
                s_pair = jnp.dot(q_ref[0, rows, h0 * B_HD:(h0 + 2) * B_HD], _block_diag(*kts),
                                 preferred_element_type=F32)
                weighted, v_exts, decays, cts = [], [], [], []
                for j, h in enumerate(pair):
                    idx = d * B_HEADS + h
                    ci, cf = (2 * d) * B_HEADS + h, cf0 + h
                    i_row, f_row = gt[ci:ci + 1, :], frow_all[cf:cf + 1, :]
                    f_col = fcol_all[:, h * LANES:(h + 1) * LANES]
                    a_row = i_row - f_row
                    b_last = f_row[:, last:last + 1]
                    m_prev = m_ref[idx][:, 0:1]
                    q = q_ref[0, rows, h * B_HD:(h + 1) * B_HD]
                    v = v_ref[0, rows, h * B_HD:(h + 1) * B_HD]
                    v_ext = jnp.concatenate([v, jnp.ones_like(v)], axis=1)
                    dmat = jnp.where(before, f_col + a_row, -jnp.inf)
                    m_t = jnp.maximum(f_col + m_prev, jnp.max(dmat, axis=-1, keepdims=True))
                    wqk = jnp.exp(dmat - m_t) * s_pair[:, j * L:(j + 1) * L]
                    a = jnp.exp(f_col + m_prev - m_t)
                    ct = c_ref[idx]
                    num_den = jnp.dot(
                        jnp.concatenate([wqk.astype(BF16), (a * q.astype(F32)).astype(BF16)], axis=1),
                        jnp.concatenate([v_ext, ct.astype(BF16)], axis=0), preferred_element_type=F32)
                    hval = num_den[:, :B_HD] / jnp.maximum(jnp.abs(num_den[:, B_HD:]), jnp.exp(-m_t))
                    out_ref[0, rows, h * B_HD:(h + 1) * B_HD] = hval.astype(BF16)
                    g_row = b_last + a_row
                    m_new = jnp.maximum(b_last + m_prev, jnp.max(g_row, axis=-1, keepdims=True))
                    decays.append(jnp.exp(b_last + m_prev - m_new))
                    weighted.append((kts[j].astype(F32) * jnp.exp(g_row - m_new)).astype(BF16))
                    v_exts.append(v_ext)
                    cts.append(ct)
                    m_ref[idx] = jnp.broadcast_to(m_new, (1, LANES))
                update = jnp.dot(_block_diag(*weighted), jnp.concatenate(v_exts, axis=0),
                                 preferred_element_type=F32)
                for j, h in enumerate(pair):
                    c_ref[d * B_HEADS + h] = decays[j] * cts[j] + update[j * B_HD:(j + 1) * B_HD]


def _mlstm(q, k, v, gt, ctx_len):
    b, tt, w = q.shape
    L = MLSTM_STEP_CHUNKS * B_CHUNK
    assert ctx_len % L == 0 and tt % L == 0
    nc = tt // L
    cc = ctx_len // L

    nl = nc - cc

    def fwd(s):
        return jnp.where(s < cc, nl + s, s - cc)

    def bwd(s):
        return nc - 1 - s

    def specs(f):
        tok = pl.BlockSpec((1, L, w), lambda bi, s: (bi, f(s), 0))
        lanes = lambda feat: pl.BlockSpec((1, feat, L), lambda bi, s: (bi, 0, f(s)))
        return [tok, lanes(w), tok, lanes(16)]

    out = jax.ShapeDtypeStruct((b, tt, w), BF16)
    return pl.pallas_call(
        _mlstm_kernel,
        out_shape=[out, out],
        grid=(b, nc),
        in_specs=specs(fwd) + specs(bwd),
        out_specs=[pl.BlockSpec((1, L, w), lambda bi, s: (bi, fwd(s), 0)),
                   pl.BlockSpec((1, L, w), lambda bi, s: (bi, bwd(s), 0))],
        scratch_shapes=[pltpu.VMEM((2 * B_HEADS, B_HD, 2 * B_HD), F32),
                        pltpu.VMEM((2 * B_HEADS, 1, LANES), F32)],
        compiler_params=pltpu.CompilerParams(
            dimension_semantics=("parallel", "arbitrary"), vmem_limit_bytes=VMEM_LIMIT),
        name="mlstm",
    )(q, k, v, gt, q, k, v, gt)


def _softmax_update(q, k, v_ext, m_old, acc):
    s = _dot_nt(q, k)
    m_new = jnp.maximum(m_old, jnp.max(s, axis=-1, keepdims=True))
    p = jnp.exp2(s - jnp.tile(m_new, (1, s.shape[1] // LANES)))
    alpha = jnp.exp2(m_old - m_new)
    acc = jnp.tile(alpha, (1, 2)) * acc + jnp.dot(p.astype(BF16), v_ext, preferred_element_type=F32)
    return m_new, acc


def _attend(qs, stats, k_ref, v_ref):
    vals = [(m_ref[...], acc_ref[...]) for m_ref, acc_ref in stats]
    sub = min(FLASH_SUB, k_ref.shape[1])
    for c in range(k_ref.shape[1] // sub):
        k = k_ref[0, c * sub:(c + 1) * sub, :]
        v = v_ref[0, c * sub:(c + 1) * sub, :]
        v_ext = jnp.concatenate([v, jnp.ones_like(v)], axis=1)
        vals = [_softmax_update(q, k, v_ext, m, acc) for q, (m, acc) in zip(qs, vals)]
    for (m_ref, acc_ref), (m, acc) in zip(stats, vals):
        m_ref[...] = m
        acc_ref[...] = acc


def _flash_init(stats):
    for m_ref, acc_ref in stats:
        m_ref[...] = jnp.full_like(m_ref, NEG_BIG)
        acc_ref[...] = jnp.zeros_like(acc_ref)


def _normalized(acc_ref):
    return acc_ref[:, :LANES] / acc_ref[:, LANES:]


def _mla_queries(q):
    return [q]


def _mla_finish(stats):
    return _normalized(stats[0][1])


def _diff_queries(q):
    low = lax.broadcasted_iota(jnp.int32, q.shape, 1) < D_HD
    zero = jnp.zeros_like(q)
    return [jnp.where(low, q, zero), jnp.where(low, zero, q)]


def _diff_finish(stats, lam_ref, gsub_ref, *, lam_init):
    lq1, lk1, lq2, lk2 = (lam_ref[r:r + 1, :] for r in range(4))
    lam = (jnp.exp(jnp.sum(lq1 * lk1, axis=-1, keepdims=True))
           - jnp.exp(jnp.sum(lq2 * lk2, axis=-1, keepdims=True)) + lam_init)
    o = _normalized(stats[0][1]) - lam * _normalized(stats[1][1])
    return _rms(o, gsub_ref[...]) * (1.0 - lam_init)


def _flash_kernel(*refs, n_extra, n_maps, queries, finish, latent):
    extra, refs = refs[:n_extra], refs[n_extra:]
    if latent:
        q_ref, kc_ref, vc_ref, k_ref, v_ref, o_ref = refs[:6]
        scratch = refs[6:]
    else:
        q_ref, kc_ref, vc_ref, _, o_ref = refs[:5]
        scratch = refs[5:]
    stats = [(scratch[2 * i], scratch[2 * i + 1]) for i in range(n_maps)]
    qs = queries(q_ref[0])
    if not latent:
        _flash_init(stats)
        _attend(qs, stats, kc_ref, vc_ref)
        o_ref[0] = finish(stats, *extra).astype(BF16)
        return
    j = pl.program_id(3)

    @pl.when(j == 0)
    def _():
        _flash_init(stats)
        _attend(qs, stats, kc_ref, vc_ref)

    _attend(qs, stats, k_ref, v_ref)

    @pl.when(j == pl.num_programs(3) - 1)
    def _():
        o_ref[0] = finish(stats, *extra).astype(BF16)


def _full_attention(q, k, v, seq, key_block, heads, qk_width, n_maps, queries, finish, extra, ctx_queries, name):
    b, tt, _ = q.shape
    ctx_len = tt - seq
    cb = seq // ctx_len
    tq, tk = min(FLASH_TQ, seq), min(key_block, seq)
    assert seq % tq == 0 and seq % tk == 0 and seq % ctx_len == 0
    out_shape = jax.ShapeDtypeStruct((b, tt, heads * LANES), BF16)

    def scratch(rows):
        return [pltpu.VMEM((rows, LANES), F32), pltpu.VMEM((rows, 2 * LANES), F32)] * n_maps

    body = functools.partial(_flash_kernel, n_extra=len(extra), n_maps=n_maps, queries=queries, finish=finish)
    whole4 = [pl.BlockSpec(e.shape, lambda bi, h, i, j: (0, 0)) for e in extra]
    y = pl.pallas_call(
        functools.partial(body, latent=True),
        out_shape=out_shape,
        grid=(b, heads, seq // tq, seq // tk),
        in_specs=whole4 + [
            pl.BlockSpec((1, tq, qk_width), lambda bi, h, i, j: (bi, i, h)),
            pl.BlockSpec((1, ctx_len, qk_width), lambda bi, h, i, j: (bi, cb, h)),
            pl.BlockSpec((1, ctx_len, LANES), lambda bi, h, i, j: (bi, cb, h)),
            pl.BlockSpec((1, tk, qk_width), lambda bi, h, i, j: (bi, j, h)),
            pl.BlockSpec((1, tk, LANES), lambda bi, h, i, j: (bi, j, h))],
        out_specs=pl.BlockSpec((1, tq, LANES), lambda bi, h, i, j: (bi, i, h)),
        scratch_shapes=scratch(tq),
        compiler_params=pltpu.CompilerParams(
            dimension_semantics=("parallel", "parallel", "parallel", "arbitrary"),
            vmem_limit_bytes=VMEM_LIMIT),
        name=name,
    )(*extra, q, k, v, k, v)
    if not ctx_queries:
        return y
    whole2 = [pl.BlockSpec(e.shape, lambda bi, h: (0, 0)) for e in extra]
    return pl.pallas_call(
        functools.partial(body, latent=False),
        out_shape=out_shape,
        grid=(b, heads),
        in_specs=whole2 + [
            pl.BlockSpec((1, ctx_len, qk_width), lambda bi, h: (bi, cb, h)),
            pl.BlockSpec((1, ctx_len, qk_width), lambda bi, h: (bi, cb, h)),
            pl.BlockSpec((1, ctx_len, LANES), lambda bi, h: (bi, cb, h)),
            pl.BlockSpec(memory_space=pl.ANY)],
        out_specs=pl.BlockSpec((1, ctx_len, LANES), lambda bi, h: (bi, cb, h)),
        scratch_shapes=scratch(ctx_len),
        input_output_aliases={len(extra) + 3: 0},
        compiler_params=pltpu.CompilerParams(
            dimension_semantics=("parallel", "parallel"), vmem_limit_bytes=VMEM_LIMIT),
        name=name + "_ctx",
    )(*extra, q, k, v, y)


def _mla_attention(q, k, v, seq, ctx_queries):
    return _full_attention(q, k, v, seq, MLA_TK, C_HEADS, C_QK_PAD, 1, _mla_queries, _mla_finish, [],
                           ctx_queries, "mla_attention")


def _diff_attention(q, k, v, lam_vecs, gsub, lam_init, seq, ctx_queries):
    return _full_attention(q, k, v, seq, DIFF_TK, D_HEADS, 2 * D_HD, 2, _diff_queries,
                           functools.partial(_diff_finish, lam_init=lam_init), [lam_vecs, gsub],
                           ctx_queries, "diff_attention")


def _merge_ffn_kernel(*refs, n_stream, lat_tiles):
    (ya_ref, hf_ref, hb_ref, bo_ref, yc_ref, yd_ref, gate_ref, wbr_ref, wout_ref,
     mod_ref, gpost_ref, fpre_ref, fpost_ref, wg_ref, wu_ref, wd_ref, o_ref) = refs[n_stream:]
    d = o_ref.shape[2]
    yb = (bo_ref[0].astype(F32) * (hf_ref[0].astype(F32) + hb_ref[0].astype(F32))).astype(BF16)
    ys = (ya_ref[0], yb, yc_ref[0], yd_ref[0])
    merged = None
    for i, y in enumerate(ys):
        g = gate_ref[0, :, i * d:(i + 1) * d].astype(F32)
        term = g * jnp.dot(y, wbr_ref[i], preferred_element_type=F32)
        merged = term if merged is None else merged + term
    out = jnp.dot(merged.astype(BF16), wout_ref[...], preferred_element_type=F32)
    x = _stream_tile(refs[:n_stream], lat_tiles) + mod_ref[0, 2:3, :] * _rms(out, gpost_ref[...])
    u = (_rms(x, fpre_ref[...]) * (1.0 + mod_ref[0, 4:5, :]) + mod_ref[0, 3:4, :]).astype(BF16)
    gate = jnp.dot(u, wg_ref[...], preferred_element_type=F32)
    up = jnp.dot(u, wu_ref[...], preferred_element_type=F32)
    act = (gate * _sigmoid(gate) * up).astype(BF16)
    out = jnp.dot(act, wd_ref[...], preferred_element_type=F32)
    o_ref[0] = x + mod_ref[0, 5:6, :] * _rms(out, fpost_ref[...])


def _merge_ffn(p, ya, hf, hb, yc, yd, wbr, wout, stream, modall, gpost, fpre, fpost, wg, wu, wd, seq, rows):
    b, _, d = stream[0].shape
    tp = TOKEN_TILE
    tok = lambda width: pl.BlockSpec((1, tp, width), lambda bi, t: (bi, t, 0))
    vec = _resident((1, d))
    return pl.pallas_call(
        functools.partial(_merge_ffn_kernel, n_stream=len(stream), lat_tiles=seq // tp),
        out_shape=jax.ShapeDtypeStruct((b, rows, d), F32),
        grid=(b, rows // tp),
        in_specs=_stream_specs(stream, tp) + [tok(BRANCH_W)] * 6 + [
            tok(N_BRANCH * d), _resident(wbr.shape), _resident(wout.shape),
            _mod_spec(d, seq // tp), vec, vec, vec,
            _resident(wg.shape), _resident(wu.shape), _resident(wd.shape)],
        out_specs=tok(d),
        compiler_params=pltpu.CompilerParams(
            dimension_semantics=("parallel", "parallel"), vmem_limit_bytes=VMEM_LIMIT),
        name="merge_ffn",
    )(*stream, ya, hf, hb, p['bo'], yc, yd, p['gate'], wbr, wout, modall, gpost, fpre, fpost, wg, wu, wd)


def _rope_tables(n_rows, ctx_len):
    row = jnp.repeat(jnp.arange(n_rows, dtype=F32), GRID_W)
    col = jnp.tile(jnp.arange(GRID_W, dtype=F32), n_rows)
    quarter = ROPE_DIM // 4
    inv_freq = ROPE_BASE ** (-jnp.arange(quarter, dtype=F32) / quarter)
    ang_r, ang_c = row[:, None] * inv_freq, col[:, None] * inv_freq
    ang = jnp.concatenate([ang_r, ang_r, ang_c, ang_c], axis=-1)
    cos, sin = jnp.cos(ang), jnp.sin(ang)
    cos = jnp.concatenate([cos, jnp.ones((ctx_len, ROPE_DIM), F32)], axis=0)
    sin = jnp.concatenate([sin, jnp.zeros((ctx_len, ROPE_DIM), F32)], axis=0)
    cos, sin = jnp.tile(cos, (1, LANES // ROPE_DIM)), jnp.tile(sin, (1, LANES // ROPE_DIM))
    first = (jnp.arange(LANES) % (ROPE_DIM // 2)) < quarter
    return cos, jnp.where(first, -sin, 0.0), jnp.where(first, 0.0, sin)


def _a_head_perm():
    idx = []
    for j in range(A_HEADS // 2):
        for half in range(2):
            head = j + (A_HEADS // 2) * half
            idx.extend(range(head * A_HD, (head + 1) * A_HD))
    return np.asarray(idx)


def _prep_layer(w_in, b_gate_bias, c_g_q, c_g_kv, c_w_uq, c_w_uk, c_w_uv, w_branch):
    d = w_in.shape[0]
    splits = (512, 128, 128, 512, 512, 512, 16, 512, 512, 256, 64, 512, 512, 512, N_BRANCH * d)
    offs = np.concatenate([[0], np.cumsum(splits)])
    part = [w_in[:, offs[i]:offs[i + 1]] for i in range(len(splits))]
    (a_q, a_k, a_v, b_q, b_k, b_v, b_g, b_o, c_q, c_kv, c_kr, d_q, d_k, d_v, gate) = part
    perm = _a_head_perm()
    zeros = lambda n: jnp.zeros((d, n), w_in.dtype)
    uq = c_w_uq.reshape(C_Q_LORA, C_HEADS, C_NOPE + C_ROPE)
    uq = jnp.pad(uq, ((0, 0), (0, 0), (0, C_QK_PAD - C_NOPE - C_ROPE))).reshape(C_Q_LORA, C_HEADS * C_QK_PAD)
    w = {
        'wa': jnp.concatenate([a_q[:, perm], a_k, a_v], axis=1).astype(BF16),
        'wb': jnp.concatenate([b_q, b_v, b_o], axis=1).astype(BF16),
        'wgt': jnp.concatenate([b_k, b_g], axis=1).T.astype(BF16),
        'wc': jnp.concatenate([c_q, c_kv, c_kr, zeros(LANES - C_ROPE)], axis=1).astype(BF16),
        'wd': jnp.concatenate([d_q, d_k, d_v], axis=1).astype(BF16),
        'wgate': gate.astype(BF16),
        'gbiast': b_gate_bias.reshape(16, 1),
        'cgq': c_g_q.reshape(1, -1),
        'cgkv': c_g_kv.reshape(1, -1),
        'wuq': uq.astype(BF16),
        'wukv': jnp.concatenate([c_w_uk, c_w_uv], axis=1).astype(BF16),
    }
    wbr = jnp.concatenate([w_branch[0:1, perm], w_branch[1:]], axis=0).astype(BF16)
    return w, wbr


def kernel(x, c, ctx, c_ctx, w_mod, b_mod, g_mix_pre, g_mix_post, g_ffn_pre, g_ffn_post, w_in, a_sink,
           b_gate_bias, c_g_q, c_g_kv, c_w_uq, c_w_uk, c_w_uv, d_lam_q1, d_lam_k1, d_lam_q2, d_lam_k2,
           d_g_sub, w_branch, w_out, w_gate_up, w_down):
    b, seq, d = x.shape
    ctx_len = ctx.shape[1]
    depth = w_in.shape[0]
    tables = _rope_tables(seq // GRID_W, ctx_len)

    rows = jnp.concatenate([c, c_ctx[None, :], jnp.zeros((8 - b - 1, d), F32)], axis=0)
    mods = _modulation(rows, w_mod, b_mod).reshape(depth, 8, 6, d)

    assert ctx_len == TOKEN_TILE == WINDOW_TILE, "the context must be exactly one token tile"
    stream = (x, ctx)
    for l in range(depth):
        with_ctx = l < depth - 1
        modall = jnp.stack([jnp.broadcast_to(mods[l, b], (b, 6, d)), mods[l, :b]], axis=1).reshape(2 * b, 6, d)
        w, wbr = _prep_layer(w_in[l], b_gate_bias[l], c_g_q[l], c_g_kv[l], c_w_uq[l], c_w_uk[l],
                             c_w_uv[l], w_branch[l])
        lam_init = 0.8 - 0.6 * math.exp(-0.3 * l)
        lam_vecs = jnp.stack([d_lam_q1[l], d_lam_k1[l], d_lam_q2[l], d_lam_k2[l]], axis=0)
        hidden = w_down.shape[1]

        p = _project(stream, modall, g_mix_pre[l].reshape(1, d), tables, w)
        ya = _window_attention(p['aq'], p['ak'], p['av'], a_sink[l], ctx_len)
        hf, hb = _mlstm(p['bq'], p['bk'], p['bv'], p['bgt'], ctx_len)
        yc = _mla_attention(p['cq'], p['ck'], p['cv'], seq, with_ctx)
        yd = _diff_attention(p['dq'], p['dk'], p['dv'], lam_vecs, d_g_sub[l].reshape(1, -1), lam_init,
                             seq, with_ctx)
        rows = seq + ctx_len if with_ctx else seq
        hidden_out = _merge_ffn(
            p, ya, hf, hb, yc, yd, wbr, w_out[l].astype(BF16), stream, modall,
            g_mix_post[l].reshape(1, d), g_ffn_pre[l].reshape(1, d), g_ffn_post[l].reshape(1, d),
            w_gate_up[l, :, :hidden].astype(BF16), w_gate_up[l, :, hidden:].astype(BF16),
            w_down[l].astype(BF16), seq, rows)
        stream = (hidden_out,)
    return stream[0]
```

```python
import functools
import math

import numpy as np
import jax
import jax.numpy as jnp
from jax import lax
from jax.experimental import pallas as pl
from jax.experimental.pallas import tpu as pltpu

F32 = jnp.float32
BF16 = jnp.bfloat16

GRID_W = 64
ROPE_BASE = 10000.0
NORM_EPS = 1e-6
A_HEADS, A_KV_HEADS, A_HD, A_WINDOW = 8, 2, 64, 128
B_HEADS, B_HD, B_CHUNK = 4, 128, 128
C_HEADS, C_Q_LORA, C_KV_LORA, C_NOPE, C_ROPE, C_VD = 4, 512, 256, 128, 64, 128
D_HEADS, D_HD, D_VD = 4, 64, 128
N_BRANCH = 4
BRANCH_W = 512
ROPE_DIM = 64
A_Q_W, A_KV_W = A_HEADS * A_HD, A_KV_HEADS * A_HD
B_W, B_GATES = B_HEADS * B_HD, 4 * B_HEADS
C_KN_W, C_V_W = C_HEADS * C_NOPE, C_HEADS * C_VD
D_QK_W, D_V_W = D_HEADS * 2 * D_HD, D_HEADS * D_VD
C_QK_PAD = 256

LANES = 128
NEG_BIG = -1e30
VMEM_LIMIT = 56 * 1024 * 1024

TOKEN_TILE = 256
WINDOW_TILE = 256
FLASH_TQ = 2048
MLA_TK = 8192
DIFF_TK = 4096
FLASH_SUB = 256
MLSTM_STEP_CHUNKS = 2
LOG2E = math.log2(math.e)


def _resident(shape):
    nd = len(shape)
    return pl.BlockSpec(shape, lambda *_: (0,) * nd, pipeline_mode=pl.Buffered(1))


def _mod_spec(d, ctx_tile):
    return pl.BlockSpec((1, 6, d), lambda bi, t: (bi * 2 + jnp.where(t == ctx_tile, 0, 1), 0, 0))


def _sigmoid(x):
    return 1.0 / (1.0 + jnp.exp(-x))


def _log_sigmoid(x):
    return jnp.minimum(x, 0.0) - jnp.log(1.0 + jnp.exp(-jnp.abs(x)))


def _rms(x, g):
    return x * lax.rsqrt(jnp.mean(x * x, axis=-1, keepdims=True) + NORM_EPS) * g


def _rope(x, cos, sin_lo, sin_hi):
    quarter = ROPE_DIM // 4
    return x * cos + pltpu.roll(x, LANES - quarter, 1) * sin_lo + pltpu.roll(x, quarter, 1) * sin_hi


def _dot_nt(a, b):
    return lax.dot_general(a, b, (((1,), (1,)), ((), ())), preferred_element_type=F32)


def _dot_tn(a, b):
    return lax.dot_general(a, b, (((0,), (0,)), ((), ())), preferred_element_type=F32)


def _mod_kernel(c_ref, w_ref, b_ref, o_ref):
    c = c_ref[...]
    s = c * _sigmoid(c)
    o_ref[0] = jnp.dot(s, w_ref[0], preferred_element_type=F32,
                       precision=lax.Precision.HIGHEST) + b_ref[0]


def _modulation(rows, w_mod, b_mod):
    depth, d, n = w_mod.shape
    tn = d
    return pl.pallas_call(
        _mod_kernel,
        out_shape=jax.ShapeDtypeStruct((depth, rows.shape[0], n), F32),
        grid=(depth, n // tn),
        in_specs=[pl.BlockSpec(rows.shape, lambda l, j: (0, 0)),
                  pl.BlockSpec((1, d, tn), lambda l, j: (l, 0, j)),
                  pl.BlockSpec((1, 1, tn), lambda l, j: (l, 0, j))],
        out_specs=pl.BlockSpec((1, rows.shape[0], tn), lambda l, j: (l, 0, j)),
        name="modulation",
    )(rows, w_mod, b_mod.reshape(depth, 1, n))


def _stream_specs(stream, tp):
    d = stream[0].shape[2]
    if len(stream) == 1:
        return [pl.BlockSpec((1, tp, d), lambda bi, t: (bi, t, 0))]
    last = stream[0].shape[1] // tp - 1
    return [pl.BlockSpec((1, tp, d), lambda bi, t: (bi, jnp.minimum(t, last), 0)),
            pl.BlockSpec((1, tp, d), lambda bi, t: (bi, 0, 0))]


def _stream_tile(refs, lat_tiles):
    if len(refs) == 1:
        return refs[0][0]
    return jnp.where(pl.program_id(1) == lat_tiles, refs[1][0], refs[0][0])


def _proj_kernel(*refs, n_stream, lat_tiles):
    (mod_ref, gpre_ref, cos_ref, slo_ref, shi_ref,
     wa_ref, wb_ref, wgt_ref, wc_ref, wd_ref, wgate_ref,
     gbiast_ref, cgq_ref, cgkv_ref, wuq_ref, wukv_ref,
     aq_ref, ak_ref, av_ref, bq_ref, bk_ref, bv_ref, bo_ref, bgt_ref,
     cq_ref, ck_ref, cv_ref, dq_ref, dk_ref, dv_ref, gate_ref) = refs[n_stream:]
    x = _stream_tile(refs[:n_stream], lat_tiles)
    y = _rms(x, gpre_ref[...])
    u = (y * (1.0 + mod_ref[0, 1:2, :]) + mod_ref[0, 0:1, :]).astype(BF16)
    cos, slo, shi = cos_ref[...], slo_ref[...], shi_ref[...]

    def rope(v):
        return _rope(v, cos, slo, shi)

    def chunks(v, n):
        return [v[:, i * LANES:(i + 1) * LANES] for i in range(n)]

    pa = jnp.dot(u, wa_ref[...], preferred_element_type=F32)
    for i, v in enumerate(chunks(pa, A_Q_W // LANES)):
        aq_ref[0, :, i * LANES:(i + 1) * LANES] = (rope(v) * (A_HD ** -0.5 * LOG2E)).astype(BF16)
    ak_ref[0] = rope(pa[:, A_Q_W:A_Q_W + A_KV_W]).astype(BF16)
    av_ref[0] = pa[:, A_Q_W + A_KV_W:A_Q_W + 2 * A_KV_W].astype(BF16)

    pb = jnp.dot(u, wb_ref[...], preferred_element_type=F32)
    bq_ref[0] = pb[:, 0:B_W].astype(BF16)
    bv_ref[0] = pb[:, B_W:2 * B_W].astype(BF16)
    bo_ref[0] = _sigmoid(pb[:, 2 * B_W:3 * B_W]).astype(BF16)
    kg = _dot_nt(wgt_ref[...], u)
    bk_ref[0] = (kg[0:B_W] * (B_HD ** -0.5)).astype(BF16)
    bgt_ref[0] = kg[B_W:B_W + B_GATES] + gbiast_ref[...]

    pc = jnp.dot(u, wc_ref[...], preferred_element_type=F32)
    cqn = _rms(pc[:, 0:C_Q_LORA], cgq_ref[...]).astype(BF16)
    ckvn = _rms(pc[:, C_Q_LORA:C_Q_LORA + C_KV_LORA], cgkv_ref[...]).astype(BF16)
    kr = rope(pc[:, C_Q_LORA + C_KV_LORA:C_Q_LORA + C_KV_LORA + LANES]).astype(BF16)
    cq = jnp.dot(cqn, wuq_ref[...], preferred_element_type=F32)
    ckv = jnp.dot(ckvn, wukv_ref[...], preferred_element_type=F32)
    c_scale = (C_NOPE + C_ROPE) ** -0.5 * LOG2E
    for h in range(C_HEADS):
        o = h * C_QK_PAD
        cq_ref[0, :, o:o + LANES] = (cq[:, o:o + LANES] * c_scale).astype(BF16)
        cq_ref[0, :, o + LANES:o + 2 * LANES] = (rope(cq[:, o + LANES:o + 2 * LANES]) * c_scale).astype(BF16)
        ck_ref[0, :, o:o + LANES] = ckv[:, h * LANES:(h + 1) * LANES].astype(BF16)
        ck_ref[0, :, o + LANES:o + 2 * LANES] = kr
    cv_ref[0] = ckv[:, C_KN_W:C_KN_W + C_V_W].astype(BF16)

    pd = jnp.dot(u, wd_ref[...], preferred_element_type=F32)
    for i in range(D_QK_W // LANES):
        sl = slice(i * LANES, (i + 1) * LANES)
        dq_ref[0, :, sl] = (rope(pd[:, sl]) * (D_HD ** -0.5 * LOG2E)).astype(BF16)
        dk_ref[0, :, sl] = rope(pd[:, D_QK_W + i * LANES:D_QK_W + (i + 1) * LANES]).astype(BF16)
    dv_ref[0] = pd[:, 2 * D_QK_W:2 * D_QK_W + D_V_W].astype(BF16)

    for i in range(N_BRANCH):
        sl = slice(i * x.shape[1], (i + 1) * x.shape[1])
        gate_ref[0, :, sl] = _sigmoid(jnp.dot(u, wgate_ref[:, sl], preferred_element_type=F32)).astype(BF16)


def _project(stream, modall, gpre, tables, w):
    b, _, d = stream[0].shape
    tt = sum(s.shape[1] for s in stream)
    tp = TOKEN_TILE
    nt = tt // tp
    cos, slo, shi = tables

    def tok(width):
        return pl.BlockSpec((1, tp, width), lambda bi, t: (bi, t, 0))

    tab = pl.BlockSpec((tp, LANES), lambda bi, t: (t, 0))
    in_specs = _stream_specs(stream, tp) + [
        _mod_spec(d, nt - 1),
        _resident((1, d)), tab, tab, tab,
        _resident(w['wa'].shape), _resident(w['wb'].shape),
        _resident(w['wgt'].shape), _resident(w['wc'].shape), _resident(w['wd'].shape),
        _resident(w['wgate'].shape), _resident(w['gbiast'].shape),
        _resident(w['cgq'].shape), _resident(w['cgkv'].shape), _resident(w['wuq'].shape),
        _resident(w['wukv'].shape),
    ]
    widths = dict(aq=A_Q_W, ak=A_KV_W, av=A_KV_W, bq=B_W, bv=B_W, bo=B_W,
                  cq=C_HEADS * C_QK_PAD, ck=C_HEADS * C_QK_PAD, cv=C_V_W,
                  dq=D_QK_W, dk=D_QK_W, dv=D_V_W, gate=N_BRANCH * d)
    tokens_on_lanes = dict(bk=(B_W, BF16), bgt=(B_GATES, F32))
    names = ['aq', 'ak', 'av', 'bq', 'bk', 'bv', 'bo', 'bgt',
             'cq', 'ck', 'cv', 'dq', 'dk', 'dv', 'gate']
    out_shape, out_specs = [], []
    for n in names:
        if n in tokens_on_lanes:
            feat, dtype = tokens_on_lanes[n]
            out_shape.append(jax.ShapeDtypeStruct((b, feat, tt), dtype))
            out_specs.append(pl.BlockSpec((1, feat, tp), lambda bi, t: (bi, 0, t)))
        else:
            out_shape.append(jax.ShapeDtypeStruct((b, tt, widths[n]), BF16))
            out_specs.append(tok(widths[n]))
    outs = pl.pallas_call(
        functools.partial(_proj_kernel, n_stream=len(stream), lat_tiles=nt - 1),
        out_shape=out_shape,
        grid=(b, nt),
        in_specs=in_specs,
        out_specs=out_specs,
        compiler_params=pltpu.CompilerParams(
            dimension_semantics=("parallel", "parallel"), vmem_limit_bytes=VMEM_LIMIT),
        name="project",
    )(*stream, modall, gpre, cos, slo, shi, w['wa'], w['wb'], w['wgt'], w['wc'], w['wd'],
      w['wgate'], w['gbiast'], w['cgq'], w['cgkv'], w['wuq'], w['wukv'])
    return dict(zip(names, outs))


def _window_kernel(sink_ref, q_ref, kc_ref, kp_ref, kk_ref, kn_ref,
                   vc_ref, vp_ref, vk_ref, vn_ref, o_ref, *, lat_tiles):
    t = pl.program_id(1)
    hb = WINDOW_TILE // 2
    n_ctx = kc_ref.shape[1]
    group = A_HEADS // A_KV_HEADS
    width = n_ctx + 3 * hb
    kwin = jnp.concatenate([kp_ref[0], kk_ref[0], kn_ref[0]], axis=0)
    vwin = jnp.concatenate([vp_ref[0], vk_ref[0], vn_ref[0]], axis=0)
    prev_ok = jnp.logical_and(t >= 1, t < lat_tiles).astype(jnp.int32)
    cur_ok = (t < lat_tiles).astype(jnp.int32)
    next_ok = (t < lat_tiles - 1).astype(jnp.int32)
    row = lax.broadcasted_iota(jnp.int32, (group * hb, width), 0) & (hb - 1)
    col = lax.broadcasted_iota(jnp.int32, (group * hb, width), 1)
    wcol = col - n_ctx
    near = jnp.abs(wcol - hb - row) <= A_WINDOW
    low = lax.broadcasted_iota(jnp.int32, (hb, LANES), 1) < A_HD
    for blk in range(2):
        k0 = hb * (1 + blk)
        kb = jnp.concatenate([kc_ref[0], kwin[k0:k0 + 3 * hb]], axis=0)
        vb = jnp.concatenate([vc_ref[0], vwin[k0:k0 + 3 * hb]], axis=0)
        vb_ext = jnp.concatenate([vb, jnp.ones_like(vb)], axis=1)
        if blk == 0:
            seg_ok = jnp.where(wcol < hb, prev_ok, cur_ok)
        else:
            seg_ok = jnp.where(wcol < 2 * hb, cur_ok, next_ok)
        valid = jnp.logical_or(col < n_ctx, jnp.logical_and(near, seg_ok > 0))
        rows = slice(blk * hb, (blk + 1) * hb)
        outs = []
        for g in range(A_KV_HEADS):
            keep = low if g == 0 else jnp.logical_not(low)
            qs, sinks = [], []
            for j in range(group):
                qb = q_ref[0, rows, j * LANES:(j + 1) * LANES]
                qs.append(jnp.where(keep, qb, jnp.zeros_like(qb)))
                sinks.append(jnp.full((hb, LANES), sink_ref[j + group * g] * LOG2E, F32))
            sink = jnp.concatenate(sinks, axis=0)
            s = jnp.where(valid, _dot_nt(jnp.concatenate(qs, axis=0), kb), NEG_BIG)
            m = jnp.maximum(jnp.max(s, axis=-1, keepdims=True), sink)
            e = jnp.exp2(s - jnp.tile(m, (1, width // LANES)))
            pv = jnp.dot(e.astype(BF16), vb_ext, preferred_element_type=F32)
            outs.append(pv[:, :LANES] / (pv[:, LANES:] + jnp.exp2(sink - m)))
        for j in range(group):
            o_ref[0, rows, j * LANES:(j + 1) * LANES] = jnp.where(
                low, outs[0][j * hb:(j + 1) * hb], outs[1][j * hb:(j + 1) * hb]).astype(BF16)


def _window_attention(q, k, v, sink, ctx_len):
    b, tt, _ = q.shape
    tw = WINDOW_TILE
    assert ctx_len == tw == 2 * A_WINDOW, "context is one key tile; a query block is one window radius"
    nt = tt // tw
    nl = nt - 1
    kv = lambda f: pl.BlockSpec((1, tw, LANES), lambda bi, t: (bi, f(t), 0))
    specs = [kv(lambda t: nl), kv(lambda t: jnp.maximum(t - 1, 0)), kv(lambda t: t),
             kv(lambda t: jnp.minimum(t + 1, nl - 1))]
    return pl.pallas_call(
        functools.partial(_window_kernel, lat_tiles=nl),
        out_shape=jax.ShapeDtypeStruct((b, tt, A_Q_W), BF16),
        grid=(b, nt),
        in_specs=[pl.BlockSpec(memory_space=pltpu.SMEM),
                  pl.BlockSpec((1, tw, A_Q_W), lambda bi, t: (bi, t, 0))] + specs + specs,
        out_specs=pl.BlockSpec((1, tw, A_Q_W), lambda bi, t: (bi, t, 0)),
        compiler_params=pltpu.CompilerParams(
            dimension_semantics=("parallel", "parallel"), vmem_limit_bytes=VMEM_LIMIT),
        name="window_attention",
    )(sink, q, k, k, k, k, v, v, v, v)


def _bf16_terms(x):
    hi = x.astype(BF16)
    r1 = x - hi.astype(F32)
    mid = r1.astype(BF16)
    return [hi, mid, (r1 - mid.astype(F32)).astype(BF16)]


def _block_diag(a, b):
    return jnp.concatenate([jnp.concatenate([a, jnp.zeros_like(b)], axis=1),
                            jnp.concatenate([jnp.zeros_like(a), b], axis=1)], axis=0)


def _mlstm_kernel(qf_ref, kf_ref, vf_ref, gtf_ref, qb_ref, kb_ref, vb_ref, gtb_ref,
                  hf_ref, hb_ref, c_ref, m_ref):
    step = pl.program_id(1)

    @pl.when(step == 0)
    def _():
        c_ref[...] = jnp.zeros_like(c_ref)
        m_ref[...] = jnp.zeros_like(m_ref)

    L = B_CHUNK
    row = lax.broadcasted_iota(jnp.int32, (L, L), 0)
    col = lax.broadcasted_iota(jnp.int32, (L, L), 1)
    dirs = ((qf_ref, kf_ref, vf_ref, gtf_ref, hf_ref), (qb_ref, kb_ref, vb_ref, gtb_ref, hb_ref))
    for d, (q_ref, kt_ref, v_ref, gt_ref, out_ref) in enumerate(dirs):
        before = (col <= row) if d == 0 else (col >= row)
        cum3 = jnp.tile(before.astype(BF16), (1, 3))
        cum_t3 = jnp.tile(((row <= col) if d == 0 else (row >= col)).astype(BF16), (3, 1))
        last = L - 1 if d == 0 else 0
        n_chunks = q_ref.shape[1] // L
        for c in (range(n_chunks) if d == 0 else reversed(range(n_chunks))):
            rows = slice(c * L, (c + 1) * L)
            gt = gt_ref[0, :, rows]
            lf_all = _log_sigmoid(gt)
            frow_all = jnp.dot(jnp.concatenate(_bf16_terms(lf_all), axis=1), cum_t3, preferred_element_type=F32)
            cf0 = (2 * d + 1) * B_HEADS
            lf_heads = jnp.concatenate([jnp.broadcast_to(lf_all[cf0 + h:cf0 + h + 1, :], (LANES, L))
                                        for h in range(B_HEADS)], axis=0)
            fcol_all = _dot_nt(cum3, jnp.concatenate(_bf16_terms(lf_heads), axis=1))
            for h0 in range(0, B_HEADS, 2):
                pair = (h0, h0 + 1)
                kts = [kt_ref[0, h * B_HD:(h + 1) * B_HD, rows] for h in pair]
                s_pair = jnp.dot(q_ref[0, rows, h0 * B_HD:(h0 + 2) * B_HD], _block_diag(*kts),
                                 preferred_element_type=F32)
                weighted, v_exts, decays, cts = [], [], [], []
                for j, h in enumerate(pair):
                    idx = d * B_HEADS + h
                    ci, cf = (2 * d) * B_HEADS + h, cf0 + h
                    i_row, f_row = gt[ci:ci + 1, :], frow_all[cf:cf + 1, :]
                    f_col = fcol_all[:, h * LANES:(h + 1) * LANES]
                    a_row = i_row - f_row
                    b_last = f_row[:, last:last + 1]
                    m_prev = m_ref[idx][:, 0:1]
                    q = q_ref[0, rows, h * B_HD:(h + 1) * B_HD]
                    v = v_ref[0, rows, h * B_HD:(h + 1) * B_HD]
                    v_ext = jnp.concatenate([v, jnp.ones_like(v)], axis=1)
                    dmat = jnp.where(before, f_col + a_row, -jnp.inf)
                    m_t = jnp.maximum(f_col + m_prev, jnp.max(dmat, axis=-1, keepdims=True))
                    wqk = jnp.exp(dmat - m_t) * s_pair[:, j * L:(j + 1) * L]
                    a = jnp.exp(f_col + m_prev - m_t)
                    ct = c_ref[idx]
                    num_den = jnp.dot(
                        jnp.concatenate([wqk.astype(BF16), (a * q.astype(F32)).astype(BF16)], axis=1),
                        jnp.concatenate([v_ext, ct.astype(BF16)], axis=0), preferred_element_type=F32)
                    hval = num_den[:, :B_HD] / jnp.maximum(jnp.abs(num_den[:, B_HD:]), jnp.exp(-m_t))
                    out_ref[0, rows, h * B_HD:(h + 1) * B_HD] = hval.astype(BF16)
                    g_row = b_last + a_row
                    m_new = jnp.maximum(b_last + m_prev, jnp.max(g_row, axis=-1, keepdims=True))
                    decays.append(jnp.exp(b_last + m_prev - m_new))
                    weighted.append((kts[j].astype(F32) * jnp.exp(g_row - m_new)).astype(BF16))
                    v_exts.append(v_ext)
                    cts.append(ct)
                    m_ref[idx] = jnp.broadcast_to(m_new, (1, LANES))
                update = jnp.dot(_block_diag(*weighted), jnp.concatenate(v_exts, axis=0),
                                 preferred_element_type=F32)
                for j, h in enumerate(pair):
                    c_ref[d * B_HEADS + h] = decays[j] * cts[j] + update[j * B_HD:(j + 1) * B_HD]


def _mlstm(q, k, v, gt, ctx_len):
    b, tt, w = q.shape
    L = MLSTM_STEP_CHUNKS * B_CHUNK
    assert ctx_len % L == 0 and tt % L == 0
    nc = tt // L
    cc = ctx_len // L

    nl = nc - cc

    def fwd(s):
        return jnp.where(s < cc, nl + s, s - cc)

    def bwd(s):
        return nc - 1 - s

    def specs(f):
        tok = pl.BlockSpec((1, L, w), lambda bi, s: (bi, f(s), 0))
        lanes = lambda feat: pl.BlockSpec((1, feat, L), lambda bi, s: (bi, 0, f(s)))
        return [tok, lanes(w), tok, lanes(B_GATES)]

    out = jax.ShapeDtypeStruct((b, tt, w), BF16)
    return pl.pallas_call(
        _mlstm_kernel,
        out_shape=[out, out],
        grid=(b, nc),
        in_specs=specs(fwd) + specs(bwd),
        out_specs=[pl.BlockSpec((1, L, w), lambda bi, s: (bi, fwd(s), 0)),
                   pl.BlockSpec((1, L, w), lambda bi, s: (bi, bwd(s), 0))],
        scratch_shapes=[pltpu.VMEM((2 * B_HEADS, B_HD, 2 * B_HD), F32),
                        pltpu.VMEM((2 * B_HEADS, 1, LANES), F32)],
        compiler_params=pltpu.CompilerParams(
            dimension_semantics=("parallel", "arbitrary"), vmem_limit_bytes=VMEM_LIMIT),
        name="mlstm",
    )(q, k, v, gt, q, k, v, gt)


def _softmax_update(q, k, v_ext, m_old, acc):
    s = _dot_nt(q, k)
    m_new = jnp.maximum(m_old, jnp.max(s, axis=-1, keepdims=True))
    p = jnp.exp2(s - jnp.tile(m_new, (1, s.shape[1] // LANES)))
    alpha = jnp.exp2(m_old - m_new)
    acc = jnp.tile(alpha, (1, 2)) * acc + jnp.dot(p.astype(BF16), v_ext, preferred_element_type=F32)
    return m_new, acc


def _attend(qs, stats, k_ref, v_ref):
    vals = [(m_ref[...], acc_ref[...]) for m_ref, acc_ref in stats]
    sub = min(FLASH_SUB, k_ref.shape[1])
    for c in range(k_ref.shape[1] // sub):
        k = k_ref[0, c * sub:(c + 1) * sub, :]
        v = v_ref[0, c * sub:(c + 1) * sub, :]
        v_ext = jnp.concatenate([v, jnp.ones_like(v)], axis=1)
        vals = [_softmax_update(q, k, v_ext, m, acc) for q, (m, acc) in zip(qs, vals)]
    for (m_ref, acc_ref), (m, acc) in zip(stats, vals):
        m_ref[...] = m
        acc_ref[...] = acc


def _flash_init(stats):
    for m_ref, acc_ref in stats:
        m_ref[...] = jnp.full_like(m_ref, -jnp.inf)
        acc_ref[...] = jnp.zeros_like(acc_ref)


def _normalized(acc_ref):
    return acc_ref[:, :LANES] / acc_ref[:, LANES:]


def _mla_queries(q):
    return [q]


def _mla_finish(stats):
    return _normalized(stats[0][1])


def _diff_queries(q):
    low = lax.broadcasted_iota(jnp.int32, q.shape, 1) < D_HD
    zero = jnp.zeros_like(q)
    return [jnp.where(low, q, zero), jnp.where(low, zero, q)]


def _diff_finish(stats, lam_ref, gsub_ref, *, lam_init):
    lq1, lk1, lq2, lk2 = (lam_ref[r:r + 1, :] for r in range(4))
    lam = (jnp.exp(jnp.sum(lq1 * lk1, axis=-1, keepdims=True))
           - jnp.exp(jnp.sum(lq2 * lk2, axis=-1, keepdims=True)) + lam_init)
    o = _normalized(stats[0][1]) - lam * _normalized(stats[1][1])
    return _rms(o, gsub_ref[...]) * (1.0 - lam_init)


def _flash_kernel(*refs, n_extra, n_maps, queries, finish, latent):
    extra, refs = refs[:n_extra], refs[n_extra:]
    if latent:
        q_ref, kc_ref, vc_ref, k_ref, v_ref, o_ref = refs[:6]
        scratch = refs[6:]
    else:
        q_ref, kc_ref, vc_ref, _, o_ref = refs[:5]
        scratch = refs[5:]
    stats = [(scratch[2 * i], scratch[2 * i + 1]) for i in range(n_maps)]
    qs = queries(q_ref[0])
    if not latent:
        _flash_init(stats)
        _attend(qs, stats, kc_ref, vc_ref)
        o_ref[0] = finish(stats, *extra).astype(BF16)
        return
    j = pl.program_id(3)

    @pl.when(j == 0)
    def _():
        _flash_init(stats)
        _attend(qs, stats, kc_ref, vc_ref)

    _attend(qs, stats, k_ref, v_ref)

    @pl.when(j == pl.num_programs(3) - 1)
    def _():
        o_ref[0] = finish(stats, *extra).astype(BF16)


def _full_attention(q, k, v, seq, key_block, heads, qk_width, n_maps, queries, finish, extra, ctx_queries, name):
    b, tt, _ = q.shape
    ctx_len = tt - seq
    cb = seq // ctx_len
    tq, tk = min(FLASH_TQ, seq), min(key_block, seq)
    assert seq % tq == 0 and seq % tk == 0 and seq % ctx_len == 0
    out_shape = jax.ShapeDtypeStruct((b, tt, heads * LANES), BF16)

    def scratch(rows):
        return [pltpu.VMEM((rows, LANES), F32), pltpu.VMEM((rows, 2 * LANES), F32)] * n_maps

    body = functools.partial(_flash_kernel, n_extra=len(extra), n_maps=n_maps, queries=queries, finish=finish)
    whole4 = [pl.BlockSpec(e.shape, lambda bi, h, i, j: (0, 0)) for e in extra]
    y = pl.pallas_call(
        functools.partial(body, latent=True),
        out_shape=out_shape,
        grid=(b, heads, seq // tq, seq // tk),
        in_specs=whole4 + [
            pl.BlockSpec((1, tq, qk_width), lambda bi, h, i, j: (bi, i, h)),
            pl.BlockSpec((1, ctx_len, qk_width), lambda bi, h, i, j: (bi, cb, h)),
            pl.BlockSpec((1, ctx_len, LANES), lambda bi, h, i, j: (bi, cb, h)),
            pl.BlockSpec((1, tk, qk_width), lambda bi, h, i, j: (bi, j, h)),
            pl.BlockSpec((1, tk, LANES), lambda bi, h, i, j: (bi, j, h))],
        out_specs=pl.BlockSpec((1, tq, LANES), lambda bi, h, i, j: (bi, i, h)),
        scratch_shapes=scratch(tq),
        compiler_params=pltpu.CompilerParams(
            dimension_semantics=("parallel", "parallel", "parallel", "arbitrary"),
            vmem_limit_bytes=VMEM_LIMIT),
        name=name,
    )(*extra, q, k, v, k, v)
    if not ctx_queries:
        return y
    whole2 = [pl.BlockSpec(e.shape, lambda bi, h: (0, 0)) for e in extra]
    return pl.pallas_call(
        functools.partial(body, latent=False),
        out_shape=out_shape,
        grid=(b, heads),
        in_specs=whole2 + [
            pl.BlockSpec((1, ctx_len, qk_width), lambda bi, h: (bi, cb, h)),
            pl.BlockSpec((1, ctx_len, qk_width), lambda bi, h: (bi, cb, h)),
            pl.BlockSpec((1, ctx_len, LANES), lambda bi, h: (bi, cb, h)),
            pl.BlockSpec(memory_space=pl.ANY)],
        out_specs=pl.BlockSpec((1, ctx_len, LANES), lambda bi, h: (bi, cb, h)),
        scratch_shapes=scratch(ctx_len),
        input_output_aliases={len(extra) + 3: 0},
        compiler_params=pltpu.CompilerParams(
            dimension_semantics=("parallel", "parallel"), vmem_limit_bytes=VMEM_LIMIT),
        name=name + "_ctx",
    )(*extra, q, k, v, y)


def _mla_attention(q, k, v, seq, ctx_queries):
    return _full_attention(q, k, v, seq, MLA_TK, C_HEADS, C_QK_PAD, 1, _mla_queries, _mla_finish, [],
                           ctx_queries, "mla_attention")


def _diff_attention(q, k, v, lam_vecs, gsub, lam_init, seq, ctx_queries):
    return _full_attention(q, k, v, seq, DIFF_TK, D_HEADS, 2 * D_HD, 2, _diff_queries,
                           functools.partial(_diff_finish, lam_init=lam_init), [lam_vecs, gsub],
                           ctx_queries, "diff_attention")


def _merge_ffn_kernel(*refs, n_stream, lat_tiles):
    (ya_ref, hf_ref, hb_ref, bo_ref, yc_ref, yd_ref, gate_ref, wbr_ref, wout_ref,
     mod_ref, gpost_ref, fpre_ref, fpost_ref, wg_ref, wu_ref, wd_ref, o_ref) = refs[n_stream:]
    d = o_ref.shape[2]
    yb = (bo_ref[0].astype(F32) * (hf_ref[0].astype(F32) + hb_ref[0].astype(F32))).astype(BF16)
    ys = (ya_ref[0], yb, yc_ref[0], yd_ref[0])
    merged = None
    for i, y in enumerate(ys):
        g = gate_ref[0, :, i * d:(i + 1) * d].astype(F32)
        term = g * jnp.dot(y, wbr_ref[i], preferred_element_type=F32)
        merged = term if merged is None else merged + term
    out = jnp.dot(merged.astype(BF16), wout_ref[...], preferred_element_type=F32)
    x = _stream_tile(refs[:n_stream], lat_tiles) + mod_ref[0, 2:3, :] * _rms(out, gpost_ref[...])
    u = (_rms(x, fpre_ref[...]) * (1.0 + mod_ref[0, 4:5, :]) + mod_ref[0, 3:4, :]).astype(BF16)
    gate = jnp.dot(u, wg_ref[...], preferred_element_type=F32)
    up = jnp.dot(u, wu_ref[...], preferred_element_type=F32)
    act = (gate * _sigmoid(gate) * up).astype(BF16)
    out = jnp.dot(act, wd_ref[...], preferred_element_type=F32)
    o_ref[0] = x + mod_ref[0, 5:6, :] * _rms(out, fpost_ref[...])


def _merge_ffn(p, ya, hf, hb, yc, yd, wbr, wout, stream, modall, gpost, fpre, fpost, wg, wu, wd, seq, rows):
    b, _, d = stream[0].shape
    tp = TOKEN_TILE
    tok = lambda width: pl.BlockSpec((1, tp, width), lambda bi, t: (bi, t, 0))
    vec = _resident((1, d))
    return pl.pallas_call(
        functools.partial(_merge_ffn_kernel, n_stream=len(stream), lat_tiles=seq // tp),
        out_shape=jax.ShapeDtypeStruct((b, rows, d), F32),
        grid=(b, rows // tp),
        in_specs=_stream_specs(stream, tp) + [tok(BRANCH_W)] * 6 + [
            tok(N_BRANCH * d), _resident(wbr.shape), _resident(wout.shape),
            _mod_spec(d, seq // tp), vec, vec, vec,
            _resident(wg.shape), _resident(wu.shape), _resident(wd.shape)],
        out_specs=tok(d),
        compiler_params=pltpu.CompilerParams(
            dimension_semantics=("parallel", "parallel"), vmem_limit_bytes=VMEM_LIMIT),
        name="merge_ffn",
    )(*stream, ya, hf, hb, p['bo'], yc, yd, p['gate'], wbr, wout, modall, gpost, fpre, fpost, wg, wu, wd)


def _rope_tables(n_rows, ctx_len):
    row = jnp.repeat(jnp.arange(n_rows, dtype=F32), GRID_W)
    col = jnp.tile(jnp.arange(GRID_W, dtype=F32), n_rows)
    quarter = ROPE_DIM // 4
    inv_freq = ROPE_BASE ** (-jnp.arange(quarter, dtype=F32) / quarter)
    ang_r, ang_c = row[:, None] * inv_freq, col[:, None] * inv_freq
    ang = jnp.concatenate([ang_r, ang_r, ang_c, ang_c], axis=-1)
    cos, sin = jnp.cos(ang), jnp.sin(ang)
    cos = jnp.concatenate([cos, jnp.ones((ctx_len, ROPE_DIM), F32)], axis=0)
    sin = jnp.concatenate([sin, jnp.zeros((ctx_len, ROPE_DIM), F32)], axis=0)
    cos, sin = jnp.tile(cos, (1, LANES // ROPE_DIM)), jnp.tile(sin, (1, LANES // ROPE_DIM))
    first = (jnp.arange(LANES) % (ROPE_DIM // 2)) < quarter
    return cos, jnp.where(first, -sin, 0.0), jnp.where(first, 0.0, sin)


def _a_head_perm():
    idx = []
    for j in range(A_HEADS // 2):
        for half in range(2):
            head = j + (A_HEADS // 2) * half
            idx.extend(range(head * A_HD, (head + 1) * A_HD))
    return np.asarray(idx)


def _prep_layer(w_in, b_gate_bias, c_g_q, c_g_kv, c_w_uq, c_w_uk, c_w_uv, w_branch):
    d = w_in.shape[0]
    splits = (A_Q_W, A_KV_W, A_KV_W, B_W, B_W, B_W, B_GATES, B_W, C_Q_LORA, C_KV_LORA, C_ROPE,
              D_QK_W, D_QK_W, D_V_W, N_BRANCH * d)
    offs = np.concatenate([[0], np.cumsum(splits)])
    part = [w_in[:, offs[i]:offs[i + 1]] for i in range(len(splits))]
    (a_q, a_k, a_v, b_q, b_k, b_v, b_g, b_o, c_q, c_kv, c_kr, d_q, d_k, d_v, gate) = part
    perm = _a_head_perm()
    zeros = lambda n: jnp.zeros((d, n), w_in.dtype)
    uq = c_w_uq.reshape(C_Q_LORA, C_HEADS, C_NOPE + C_ROPE)
    uq = jnp.pad(uq, ((0, 0), (0, 0), (0, C_QK_PAD - C_NOPE - C_ROPE))).reshape(C_Q_LORA, C_HEADS * C_QK_PAD)
    w = {
        'wa': jnp.concatenate([a_q[:, perm], a_k, a_v], axis=1).astype(BF16),
        'wb': jnp.concatenate([b_q, b_v, b_o], axis=1).astype(BF16),
        'wgt': jnp.concatenate([b_k, b_g], axis=1).T.astype(BF16),
        'wc': jnp.concatenate([c_q, c_kv, c_kr, zeros(LANES - C_ROPE)], axis=1).astype(BF16),
        'wd': jnp.concatenate([d_q, d_k, d_v], axis=1).astype(BF16),
        'wgate': gate.astype(BF16),
        'gbiast': b_gate_bias.reshape(B_GATES, 1),
        'cgq': c_g_q.reshape(1, -1),
        'cgkv': c_g_kv.reshape(1, -1),
        'wuq': uq.astype(BF16),
        'wukv': jnp.concatenate([c_w_uk, c_w_uv], axis=1).astype(BF16),
    }
    wbr = jnp.concatenate([w_branch[0:1, perm], w_branch[1:]], axis=0).astype(BF16)
    return w, wbr


def kernel(x, c, ctx, c_ctx, w_mod, b_mod, g_mix_pre, g_mix_post, g_ffn_pre, g_ffn_post, w_in, a_sink,
           b_gate_bias, c_g_q, c_g_kv, c_w_uq, c_w_uk, c_w_uv, d_lam_q1, d_lam_k1, d_lam_q2, d_lam_k2,
           d_g_sub, w_branch, w_out, w_gate_up, w_down):
    b, seq, d = x.shape
    ctx_len = ctx.shape[1]
    depth = w_in.shape[0]
    tables = _rope_tables(seq // GRID_W, ctx_len)

    rows = jnp.concatenate([c, c_ctx[None, :], jnp.zeros((8 - b - 1, d), F32)], axis=0)
    mods = _modulation(rows, w_mod, b_mod).reshape(depth, 8, 6, d)

    assert ctx_len == TOKEN_TILE == WINDOW_TILE, "the context must be exactly one token tile"
    stream = (x, ctx)
    for l in range(depth):
        with_ctx = l < depth - 1
        modall = jnp.stack([jnp.broadcast_to(mods[l, b], (b, 6, d)), mods[l, :b]], axis=1).reshape(2 * b, 6, d)
        w, wbr = _prep_layer(w_in[l], b_gate_bias[l], c_g_q[l], c_g_kv[l], c_w_uq[l], c_w_uk[l],
                             c_w_uv[l], w_branch[l])
        lam_init = 0.8 - 0.6 * math.exp(-0.3 * l)
        lam_vecs = jnp.stack([d_lam_q1[l], d_lam_k1[l], d_lam_q2[l], d_lam_k2[l]], axis=0)
        hidden = w_down.shape[1]

        p = _project(stream, modall, g_mix_pre[l].reshape(1, d), tables, w)
        ya = _window_attention(p['aq'], p['ak'], p['av'], a_sink[l], ctx_len)
        hf, hb = _mlstm(p['bq'], p['bk'], p['bv'], p['bgt'], ctx_len)
        yc = _mla_attention(p['cq'], p['ck'], p['cv'], seq, with_ctx)
        yd = _diff_attention(p['dq'], p['dk'], p['dv'], lam_vecs, d_g_sub[l].reshape(1, -1), lam_init,
                             seq, with_ctx)
        rows = seq + ctx_len if with_ctx else seq
        hidden_out = _merge_ffn(
            p, ya, hf, hb, yc, yd, wbr, w_out[l].astype(BF16), stream, modall,
            g_mix_post[l].reshape(1, d), g_ffn_pre[l].reshape(1, d), g_ffn_post[l].reshape(1, d),
            w_gate_up[l, :, :hidden].astype(BF16), w_gate_up[l, :, hidden:].astype(BF16),
            w_down[l].astype(BF16), seq, rows)
        stream = (hidden_out,)
    return stream[0]
```

```python
import functools
import math

import numpy as np
import jax
import jax.numpy as jnp
from jax import lax
from jax.experimental import pallas as pl
from jax.experimental.pallas import tpu as pltpu

F32 = jnp.float32
BF16 = jnp.bfloat16

GRID_W = 64
ROPE_BASE = 10000.0
NORM_EPS = 1e-6
A_HEADS, A_KV_HEADS, A_HD, A_WINDOW = 8, 2, 64, 128
B_HEADS, B_HD, B_CHUNK = 4, 128, 128
C_HEADS, C_Q_LORA, C_KV_LORA, C_NOPE, C_ROPE, C_VD = 4, 512, 256, 128, 64, 128
D_HEADS, D_HD, D_VD = 4, 64, 128
N_BRANCH = 4
BRANCH_W = 512
ROPE_DIM = 64
A_Q_W, A_KV_W = A_HEADS * A_HD, A_KV_HEADS * A_HD
B_W, B_GATES = B_HEADS * B_HD, 4 * B_HEADS
C_KN_W, C_V_W = C_HEADS * C_NOPE, C_HEADS * C_VD
D_QK_W, D_V_W = D_HEADS * 2 * D_HD, D_HEADS * D_VD
C_QK_PAD = 256

LANES = 128
NEG_BIG = -1e30
VMEM_LIMIT = 56 * 1024 * 1024

TOKEN_TILE = 256
WINDOW_TILE = 256
FLASH_TQ = 2048
MLA_TK = 8192
DIFF_TK = 4096
FLASH_SUB = 256
MLSTM_STEP_CHUNKS = 2
LOG2E = math.log2(math.e)


def _resident(shape):
    nd = len(shape)
    return pl.BlockSpec(shape, lambda *_: (0,) * nd, pipeline_mode=pl.Buffered(1))


def _mod_spec(d, ctx_tile):
    return pl.BlockSpec((1, 6, d), lambda bi, t: (bi * 2 + jnp.where(t == ctx_tile, 0, 1), 0, 0))


def _sigmoid(x):
    return 1.0 / (1.0 + jnp.exp(-x))


def _log_sigmoid(x):
    return jnp.minimum(x, 0.0) - jnp.log(1.0 + jnp.exp(-jnp.abs(x)))


def _rms(x, g):
    return x * lax.rsqrt(jnp.mean(x * x, axis=-1, keepdims=True) + NORM_EPS) * g


def _rope(x, cos, sin_lo, sin_hi):
    quarter = ROPE_DIM // 4
    return x * cos + pltpu.roll(x, LANES - quarter, 1) * sin_lo + pltpu.roll(x, quarter, 1) * sin_hi


def _dot_nt(a, b):
    return lax.dot_general(a, b, (((1,), (1,)), ((), ())), preferred_element_type=F32)


def _dot_tn(a, b):
    return lax.dot_general(a, b, (((0,), (0,)), ((), ())), preferred_element_type=F32)


def _mod_kernel(c_ref, w_ref, b_ref, o_ref):
    c = c_ref[...]
    s = c * _sigmoid(c)
    o_ref[0] = jnp.dot(s, w_ref[0], preferred_element_type=F32,
                       precision=lax.Precision.HIGHEST) + b_ref[0]


def _modulation(rows, w_mod, b_mod):
    depth, d, n = w_mod.shape
    tn = d
    return pl.pallas_call(
        _mod_kernel,
        out_shape=jax.ShapeDtypeStruct((depth, rows.shape[0], n), F32),
        grid=(depth, n // tn),
        in_specs=[pl.BlockSpec(rows.shape, lambda l, j: (0, 0)),
                  pl.BlockSpec((1, d, tn), lambda l, j: (l, 0, j)),
                  pl.BlockSpec((1, 1, tn), lambda l, j: (l, 0, j))],
        out_specs=pl.BlockSpec((1, rows.shape[0], tn), lambda l, j: (l, 0, j)),
        name="modulation",
    )(rows, w_mod, b_mod.reshape(depth, 1, n))


def _stream_specs(stream, tp):
    d = stream[0].shape[2]
    if len(stream) == 1:
        return [pl.BlockSpec((1, tp, d), lambda bi, t: (bi, t, 0))]
    last = stream[0].shape[1] // tp - 1
    return [pl.BlockSpec((1, tp, d), lambda bi, t: (bi, jnp.minimum(t, last), 0)),
            pl.BlockSpec((1, tp, d), lambda bi, t: (bi, 0, 0))]


def _stream_tile(refs, lat_tiles):
    if len(refs) == 1:
        return refs[0][0]
    return jnp.where(pl.program_id(1) == lat_tiles, refs[1][0], refs[0][0])


def _proj_kernel(*refs, n_stream, lat_tiles):
    (mod_ref, gpre_ref, cos_ref, slo_ref, shi_ref,
     wa_ref, wb_ref, wgt_ref, wc_ref, wd_ref, wgate_ref,
     gbiast_ref, cgq_ref, cgkv_ref, wuq_ref, wukv_ref,
     aq_ref, ak_ref, av_ref, bq_ref, bk_ref, bv_ref, bo_ref, bgt_ref,
     cq_ref, ck_ref, cv_ref, dq_ref, dk_ref, dv_ref, gate_ref) = refs[n_stream:]
    x = _stream_tile(refs[:n_stream], lat_tiles)
    y = _rms(x, gpre_ref[...])
    u = (y * (1.0 + mod_ref[0, 1:2, :]) + mod_ref[0, 0:1, :]).astype(BF16)
    cos, slo, shi = cos_ref[...], slo_ref[...], shi_ref[...]

    def rope(v):
        return _rope(v, cos, slo, shi)

    def chunks(v, n):
        return [v[:, i * LANES:(i + 1) * LANES] for i in range(n)]

    pa = jnp.dot(u, wa_ref[...], preferred_element_type=F32)
    for i, v in enumerate(chunks(pa, A_Q_W // LANES)):
        aq_ref[0, :, i * LANES:(i + 1) * LANES] = (rope(v) * (A_HD ** -0.5 * LOG2E)).astype(BF16)
    ak_ref[0] = rope(pa[:, A_Q_W:A_Q_W + A_KV_W]).astype(BF16)
    av_ref[0] = pa[:, A_Q_W + A_KV_W:A_Q_W + 2 * A_KV_W].astype(BF16)

    pb = jnp.dot(u, wb_ref[...], preferred_element_type=F32)
    bq_ref[0] = pb[:, 0:B_W].astype(BF16)
    bv_ref[0] = pb[:, B_W:2 * B_W].astype(BF16)
    bo_ref[0] = _sigmoid(pb[:, 2 * B_W:3 * B_W]).astype(BF16)
    kg = _dot_nt(wgt_ref[...], u)
    bk_ref[0] = (kg[0:B_W] * (B_HD ** -0.5)).astype(BF16)
    bgt_ref[0] = kg[B_W:B_W + B_GATES] + gbiast_ref[...]

    pc = jnp.dot(u, wc_ref[...], preferred_element_type=F32)
    cqn = _rms(pc[:, 0:C_Q_LORA], cgq_ref[...]).astype(BF16)
    ckvn = _rms(pc[:, C_Q_LORA:C_Q_LORA + C_KV_LORA], cgkv_ref[...]).astype(BF16)
    kr = rope(pc[:, C_Q_LORA + C_KV_LORA:C_Q_LORA + C_KV_LORA + LANES]).astype(BF16)
    cq = jnp.dot(cqn, wuq_ref[...], preferred_element_type=F32)
    ckv = jnp.dot(ckvn, wukv_ref[...], preferred_element_type=F32)
    c_scale = (C_NOPE + C_ROPE) ** -0.5 * LOG2E
    for h in range(C_HEADS):
        o = h * C_QK_PAD
        cq_ref[0, :, o:o + LANES] = (cq[:, o:o + LANES] * c_scale).astype(BF16)
        cq_ref[0, :, o + LANES:o + 2 * LANES] = (rope(cq[:, o + LANES:o + 2 * LANES]) * c_scale).astype(BF16)
        ck_ref[0, :, o:o + LANES] = ckv[:, h * LANES:(h + 1) * LANES].astype(BF16)
        ck_ref[0, :, o + LANES:o + 2 * LANES] = kr
    cv_ref[0] = ckv[:, C_KN_W:C_KN_W + C_V_W].astype(BF16)

    pd = jnp.dot(u, wd_ref[...], preferred_element_type=F32)
    for i in range(D_QK_W // LANES):
        sl = slice(i * LANES, (i + 1) * LANES)
        dq_ref[0, :, sl] = (rope(pd[:, sl]) * (D_HD ** -0.5 * LOG2E)).astype(BF16)
        dk_ref[0, :, sl] = rope(pd[:, D_QK_W + i * LANES:D_QK_W + (i + 1) * LANES]).astype(BF16)
    dv_ref[0] = pd[:, 2 * D_QK_W:2 * D_QK_W + D_V_W].astype(BF16)

    for i in range(N_BRANCH):
        sl = slice(i * x.shape[1], (i + 1) * x.shape[1])
        gate_ref[0, :, sl] = _sigmoid(jnp.dot(u, wgate_ref[:, sl], preferred_element_type=F32)).astype(BF16)


def _project(stream, modall, gpre, tables, w):
    b, _, d = stream[0].shape
    tt = sum(s.shape[1] for s in stream)
    tp = TOKEN_TILE
    nt = tt // tp
    cos, slo, shi = tables

    def tok(width):
        return pl.BlockSpec((1, tp, width), lambda bi, t: (bi, t, 0))

    tab = pl.BlockSpec((tp, LANES), lambda bi, t: (t, 0))
    in_specs = _stream_specs(stream, tp) + [
        _mod_spec(d, nt - 1),
        _resident((1, d)), tab, tab, tab,
        _resident(w['wa'].shape), _resident(w['wb'].shape),
        _resident(w['wgt'].shape), _resident(w['wc'].shape), _resident(w['wd'].shape),
        _resident(w['wgate'].shape), _resident(w['gbiast'].shape),
        _resident(w['cgq'].shape), _resident(w['cgkv'].shape), _resident(w['wuq'].shape),
        _resident(w['wukv'].shape),
    ]
    widths = dict(aq=A_Q_W, ak=A_KV_W, av=A_KV_W, bq=B_W, bv=B_W, bo=B_W,
                  cq=C_HEADS * C_QK_PAD, ck=C_HEADS * C_QK_PAD, cv=C_V_W,
                  dq=D_QK_W, dk=D_QK_W, dv=D_V_W, gate=N_BRANCH * d)
    tokens_on_lanes = dict(bk=(B_W, BF16), bgt=(B_GATES, F32))
    names = ['aq', 'ak', 'av', 'bq', 'bk', 'bv', 'bo', 'bgt',
             'cq', 'ck', 'cv', 'dq', 'dk', 'dv', 'gate']
    out_shape, out_specs = [], []
    for n in names:
        if n in tokens_on_lanes:
            feat, dtype = tokens_on_lanes[n]
            out_shape.append(jax.ShapeDtypeStruct((b, feat, tt), dtype))
            out_specs.append(pl.BlockSpec((1, feat, tp), lambda bi, t: (bi, 0, t)))
        else:
            out_shape.append(jax.ShapeDtypeStruct((b, tt, widths[n]), BF16))
            out_specs.append(tok(widths[n]))
    outs = pl.pallas_call(
        functools.partial(_proj_kernel, n_stream=len(stream), lat_tiles=nt - 1),
        out_shape=out_shape,
        grid=(b, nt),
        in_specs=in_specs,
        out_specs=out_specs,
        compiler_params=pltpu.CompilerParams(
            dimension_semantics=("parallel", "parallel"), vmem_limit_bytes=VMEM_LIMIT),
        name="project",
    )(*stream, modall, gpre, cos, slo, shi, w['wa'], w['wb'], w['wgt'], w['wc'], w['wd'],
      w['wgate'], w['gbiast'], w['cgq'], w['cgkv'], w['wuq'], w['wukv'])
    return dict(zip(names, outs))


def _window_kernel(sink_ref, q_ref, kc_ref, kp_ref, kk_ref, kn_ref,
                   vc_ref, vp_ref, vk_ref, vn_ref, o_ref, *, lat_tiles):
    t = pl.program_id(1)
    hb = WINDOW_TILE // 2
    n_ctx = kc_ref.shape[1]
    group = A_HEADS // A_KV_HEADS
    width = n_ctx + 3 * hb
    kwin = jnp.concatenate([kp_ref[0], kk_ref[0], kn_ref[0]], axis=0)
    vwin = jnp.concatenate([vp_ref[0], vk_ref[0], vn_ref[0]], axis=0)
    prev_ok = jnp.logical_and(t >= 1, t < lat_tiles).astype(jnp.int32)
    cur_ok = (t < lat_tiles).astype(jnp.int32)
    next_ok = (t < lat_tiles - 1).astype(jnp.int32)
    row = lax.broadcasted_iota(jnp.int32, (group * hb, width), 0) & (hb - 1)
    col = lax.broadcasted_iota(jnp.int32, (group * hb, width), 1)
    wcol = col - n_ctx
    near = jnp.abs(wcol - hb - row) <= A_WINDOW
    low = lax.broadcasted_iota(jnp.int32, (hb, LANES), 1) < A_HD
    for blk in range(2):
        k0 = hb * (1 + blk)
        kb = jnp.concatenate([kc_ref[0], kwin[k0:k0 + 3 * hb]], axis=0)
        vb = jnp.concatenate([vc_ref[0], vwin[k0:k0 + 3 * hb]], axis=0)
        vb_ext = jnp.concatenate([vb, jnp.ones_like(vb)], axis=1)
        if blk == 0:
            seg_ok = jnp.where(wcol < hb, prev_ok, cur_ok)
        else:
            seg_ok = jnp.where(wcol < 2 * hb, cur_ok, next_ok)
        valid = jnp.logical_or(col < n_ctx, jnp.logical_and(near, seg_ok > 0))
        rows = slice(blk * hb, (blk + 1) * hb)
        outs = []
        for g in range(A_KV_HEADS):
            keep = low if g == 0 else jnp.logical_not(low)
            qs, sinks = [], []
            for j in range(group):
                qb = q_ref[0, rows, j * LANES:(j + 1) * LANES]
                qs.append(jnp.where(keep, qb, jnp.zeros_like(qb)))
                sinks.append(jnp.full((hb, LANES), sink_ref[j + group * g] * LOG2E, F32))
            sink = jnp.concatenate(sinks, axis=0)
            s = jnp.where(valid, _dot_nt(jnp.concatenate(qs, axis=0), kb), NEG_BIG)
            m = jnp.maximum(jnp.max(s, axis=-1, keepdims=True), sink)
            e = jnp.exp2(s - jnp.tile(m, (1, width // LANES)))
            pv = jnp.dot(e.astype(BF16), vb_ext, preferred_element_type=F32)
            outs.append(pv[:, :LANES] / (pv[:, LANES:] + jnp.exp2(sink - m)))
        for j in range(group):
            o_ref[0, rows, j * LANES:(j + 1) * LANES] = jnp.where(
                low, outs[0][j * hb:(j + 1) * hb], outs[1][j * hb:(j + 1) * hb]).astype(BF16)


def _window_attention(q, k, v, sink, ctx_len):
    b, tt, _ = q.shape
    tw = WINDOW_TILE
    assert ctx_len == tw == 2 * A_WINDOW, "context is one key tile; a query block is one window radius"
    nt = tt // tw
    nl = nt - 1
    kv = lambda f: pl.BlockSpec((1, tw, LANES), lambda bi, t: (bi, f(t), 0))
    specs = [kv(lambda t: nl), kv(lambda t: jnp.maximum(t - 1, 0)), kv(lambda t: t),
             kv(lambda t: jnp.minimum(t + 1, nl - 1))]
    return pl.pallas_call(
        functools.partial(_window_kernel, lat_tiles=nl),
        out_shape=jax.ShapeDtypeStruct((b, tt, A_Q_W), BF16),
        grid=(b, nt),
        in_specs=[pl.BlockSpec(memory_space=pltpu.SMEM),
                  pl.BlockSpec((1, tw, A_Q_W), lambda bi, t: (bi, t, 0))] + specs + specs,
        out_specs=pl.BlockSpec((1, tw, A_Q_W), lambda bi, t: (bi, t, 0)),
        compiler_params=pltpu.CompilerParams(
            dimension_semantics=("parallel", "parallel"), vmem_limit_bytes=VMEM_LIMIT),
        name="window_attention",
    )(sink, q, k, k, k, k, v, v, v, v)


def _bf16_terms(x):
    hi = x.astype(BF16)
    r1 = x - hi.astype(F32)
    mid = r1.astype(BF16)
    return [hi, mid, (r1 - mid.astype(F32)).astype(BF16)]


def _block_diag(a, b):
    return jnp.concatenate([jnp.concatenate([a, jnp.zeros_like(b)], axis=1),
                            jnp.concatenate([jnp.zeros_like(a), b], axis=1)], axis=0)


def _mlstm_kernel(qf_ref, kf_ref, vf_ref, gtf_ref, qb_ref, kb_ref, vb_ref, gtb_ref,
                  hf_ref, hb_ref, c_ref, m_ref):
    step = pl.program_id(1)

    @pl.when(step == 0)
    def _():
        c_ref[...] = jnp.zeros_like(c_ref)
        m_ref[...] = jnp.zeros_like(m_ref)

    L = B_CHUNK
    row = lax.broadcasted_iota(jnp.int32, (L, L), 0)
    col = lax.broadcasted_iota(jnp.int32, (L, L), 1)
    dirs = ((qf_ref, kf_ref, vf_ref, gtf_ref, hf_ref), (qb_ref, kb_ref, vb_ref, gtb_ref, hb_ref))
    for d, (q_ref, kt_ref, v_ref, gt_ref, out_ref) in enumerate(dirs):
        before = (col <= row) if d == 0 else (col >= row)
        cum3 = jnp.tile(before.astype(BF16), (1, 3))
        cum_t3 = jnp.tile(((row <= col) if d == 0 else (row >= col)).astype(BF16), (3, 1))
        last = L - 1 if d == 0 else 0
        n_chunks = q_ref.shape[1] // L
        for c in (range(n_chunks) if d == 0 else reversed(range(n_chunks))):
            rows = slice(c * L, (c + 1) * L)
            gt = gt_ref[0, :, rows]
            lf_all = _log_sigmoid(gt)
            frow_all = jnp.dot(jnp.concatenate(_bf16_terms(lf_all), axis=1), cum_t3, preferred_element_type=F32)
            cf0 = (2 * d + 1) * B_HEADS
            lf_heads = jnp.concatenate([jnp.broadcast_to(lf_all[cf0 + h:cf0 + h + 1, :], (LANES, L))
                                        for h in range(B_HEADS)], axis=0)
            fcol_all = _dot_nt(cum3, jnp.concatenate(_bf16_terms(lf_heads), axis=1))
            for h0 in range(0, B_HEADS, 2):
                pair = (h0, h0 + 1)
                kts = [kt_ref[0, h * B_HD:(h + 1) * B_HD, rows] for h in pair]
                s_pair = jnp.dot(q_ref[0, rows, h0 * B_HD:(h0 + 2) * B_HD], _block_diag(*kts),
                                 preferred_element_type=F32)
                weighted, v_exts, decays, cts = [], [], [], []
                for j, h in enumerate(pair):
                    idx = d * B_HEADS + h
                    ci, cf = (2 * d) * B_HEADS + h, cf0 + h
                    i_row, f_row = gt[ci:ci + 1, :], frow_all[cf:cf + 1, :]
                    f_col = fcol_all[:, h * LANES:(h + 1) * LANES]
                    a_row = i_row - f_row
                    b_last = f_row[:, last:last + 1]
                    m_prev = m_ref[idx][:, 0:1]
                    q = q_ref[0, rows, h * B_HD:(h + 1) * B_HD]
                    v = v_ref[0, rows, h * B_HD:(h + 1) * B_HD]
                    v_ext = jnp.concatenate([v, jnp.ones_like(v)], axis=1)
                    dmat = jnp.where(before, f_col + a_row, -jnp.inf)
                    m_t = jnp.maximum(f_col + m_prev, jnp.max(dmat, axis=-1, keepdims=True))
                    wqk = jnp.exp(dmat - m_t) * s_pair[:, j * L:(j + 1) * L]
                    a = jnp.exp(f_col + m_prev - m_t)
                    ct = c_ref[idx]
                    num_den = jnp.dot(
                        jnp.concatenate([wqk.astype(BF16), (a * q.astype(F32)).astype(BF16)], axis=1),
                        jnp.concatenate([v_ext, ct.astype(BF16)], axis=0), preferred_element_type=F32)
                    hval = num_den[:, :B_HD] / jnp.maximum(jnp.abs(num_den[:, B_HD:]), jnp.exp(-m_t))
                    out_ref[0, rows, h * B_HD:(h + 1) * B_HD] = hval.astype(BF16)
                    g_row = b_last + a_row
                    m_new = jnp.maximum(b_last + m_prev, jnp.max(g_row, axis=-1, keepdims=True))
                    decays.append(jnp.exp(b_last + m_prev - m_new))
                    weighted.append((kts[j].astype(F32) * jnp.exp(g_row - m_new)).astype(BF16))
                    v_exts.append(v_ext)
                    cts.append(ct)
                    m_ref[idx] = jnp.broadcast_to(m_new, (1, LANES))
                update = jnp.dot(_block_diag(*weighted), jnp.concatenate(v_exts, axis=0),
                                 preferred_element_type=F32)
                for j, h in enumerate(pair):
                    c_ref[d * B_HEADS + h] = decays[j] * cts[j] + update[j * B_HD:(j + 1) * B_HD]


def _mlstm(q, k, v, gt, ctx_len):
    b, tt, w = q.shape
    L = MLSTM_STEP_CHUNKS * B_CHUNK
    assert ctx_len % L == 0 and tt % L == 0
    nc = tt // L
    cc = ctx_len // L

    nl = nc - cc

    def fwd(s):
        return jnp.where(s < cc, nl + s, s - cc)

    def bwd(s):
        return nc - 1 - s

    def specs(f):
        tok = pl.BlockSpec((1, L, w), lambda bi, s: (bi, f(s), 0))
        lanes = lambda feat: pl.BlockSpec((1, feat, L), lambda bi, s: (bi, 0, f(s)))
        return [tok, lanes(w), tok, lanes(B_GATES)]

    out = jax.ShapeDtypeStruct((b, tt, w), BF16)
    return pl.pallas_call(
        _mlstm_kernel,
        out_shape=[out, out],
        grid=(b, nc),
        in_specs=specs(fwd) + specs(bwd),
        out_specs=[pl.BlockSpec((1, L, w), lambda bi, s: (bi, fwd(s), 0)),
                   pl.BlockSpec((1, L, w), lambda bi, s: (bi, bwd(s), 0))],
        scratch_shapes=[pltpu.VMEM((2 * B_HEADS, B_HD, 2 * B_HD), F32),
                        pltpu.VMEM((2 * B_HEADS, 1, LANES), F32)],
        compiler_params=pltpu.CompilerParams(
            dimension_semantics=("parallel", "arbitrary"), vmem_limit_bytes=VMEM_LIMIT),
        name="mlstm",
    )(q, k, v, gt, q, k, v, gt)


def _softmax_update(q, k, v_ext, m_old, acc):
    s = _dot_nt(q, k)
    m_new = jnp.maximum(m_old, jnp.max(s, axis=-1, keepdims=True))
    p = jnp.exp2(s - jnp.tile(m_new, (1, s.shape[1] // LANES)))
    alpha = jnp.exp2(m_old - m_new)
    acc = jnp.tile(alpha, (1, 2)) * acc + jnp.dot(p.astype(BF16), v_ext, preferred_element_type=F32)
    return m_new, acc


def _attend(qs, stats, k_ref, v_ref):
    vals = [(m_ref[...], acc_ref[...]) for m_ref, acc_ref in stats]
    sub = min(FLASH_SUB, k_ref.shape[1])
    for c in range(k_ref.shape[1] // sub):
        k = k_ref[0, c * sub:(c + 1) * sub, :]
        v = v_ref[0, c * sub:(c + 1) * sub, :]
        v_ext = jnp.concatenate([v, jnp.ones_like(v)], axis=1)
        vals = [_softmax_update(q, k, v_ext, m, acc) for q, (m, acc) in zip(qs, vals)]
    for (m_ref, acc_ref), (m, acc) in zip(stats, vals):
        m_ref[...] = m
        acc_ref[...] = acc


def _flash_init(stats):
    for m_ref, acc_ref in stats:
        m_ref[...] = jnp.full_like(m_ref, -jnp.inf)
        acc_ref[...] = jnp.zeros_like(acc_ref)


def _normalized(acc_ref):
    return acc_ref[:, :LANES] / acc_ref[:, LANES:]


def _mla_queries(q):
    return [q]


def _mla_finish(stats):
    return _normalized(stats[0][1])


def _diff_queries(q):
    low = lax.broadcasted_iota(jnp.int32, q.shape, 1) < D_HD
    zero = jnp.zeros_like(q)
    return [jnp.where(low, q, zero), jnp.where(low, zero, q)]


def _diff_finish(stats, lam_ref, gsub_ref, *, lam_init):
    lq1, lk1, lq2, lk2 = (lam_ref[r:r + 1, :] for r in range(4))
    lam = (jnp.exp(jnp.sum(lq1 * lk1, axis=-1, keepdims=True))
           - jnp.exp(jnp.sum(lq2 * lk2, axis=-1, keepdims=True)) + lam_init)
    o = _normalized(stats[0][1]) - lam * _normalized(stats[1][1])
    return _rms(o, gsub_ref[...]) * (1.0 - lam_init)


def _flash_kernel(*refs, n_extra, n_maps, queries, finish, latent):
    extra, refs = refs[:n_extra], refs[n_extra:]
    if latent:
        q_ref, kc_ref, vc_ref, k_ref, v_ref, o_ref = refs[:6]
        scratch = refs[6:]
    else:
        q_ref, kc_ref, vc_ref, o_ref = refs[:4]
        scratch = refs[4:]
    stats = [(scratch[2 * i], scratch[2 * i + 1]) for i in range(n_maps)]
    qs = queries(q_ref[0])
    if not latent:
        _flash_init(stats)
        _attend(qs, stats, kc_ref, vc_ref)
        o_ref[0] = finish(stats, *extra).astype(BF16)
        return
    j = pl.program_id(3)

    @pl.when(j == 0)
    def _():
        _flash_init(stats)
        _attend(qs, stats, kc_ref, vc_ref)

    _attend(qs, stats, k_ref, v_ref)

    @pl.when(j == pl.num_programs(3) - 1)
    def _():
        o_ref[0] = finish(stats, *extra).astype(BF16)


def _full_attention(q, k, v, seq, key_block, heads, qk_width, n_maps, queries, finish, extra, ctx_queries, name):
    b, tt, _ = q.shape
    ctx_len = tt - seq
    cb = seq // ctx_len
    tq, tk = min(FLASH_TQ, seq), min(key_block, seq)
    assert seq % tq == 0 and seq % tk == 0 and seq % ctx_len == 0

    def scratch(rows):
        return [pltpu.VMEM((rows, LANES), F32), pltpu.VMEM((rows, 2 * LANES), F32)] * n_maps

    body = functools.partial(_flash_kernel, n_extra=len(extra), n_maps=n_maps, queries=queries, finish=finish)
    whole4 = [pl.BlockSpec(e.shape, lambda bi, h, i, j: (0, 0)) for e in extra]
    y = pl.pallas_call(
        functools.partial(body, latent=True),
        out_shape=jax.ShapeDtypeStruct((b, seq, heads * LANES), BF16),
        grid=(b, heads, seq // tq, seq // tk),
        in_specs=whole4 + [
            pl.BlockSpec((1, tq, qk_width), lambda bi, h, i, j: (bi, i, h)),
            pl.BlockSpec((1, ctx_len, qk_width), lambda bi, h, i, j: (bi, cb, h)),
            pl.BlockSpec((1, ctx_len, LANES), lambda bi, h, i, j: (bi, cb, h)),
            pl.BlockSpec((1, tk, qk_width), lambda bi, h, i, j: (bi, j, h)),
            pl.BlockSpec((1, tk, LANES), lambda bi, h, i, j: (bi, j, h))],
        out_specs=pl.BlockSpec((1, tq, LANES), lambda bi, h, i, j: (bi, i, h)),
        scratch_shapes=scratch(tq),
        compiler_params=pltpu.CompilerParams(
            dimension_semantics=("parallel", "parallel", "parallel", "arbitrary"),
            vmem_limit_bytes=VMEM_LIMIT),
        name=name,
    )(*extra, q, k, v, k, v)
    if not ctx_queries:
        return (y,)
    whole2 = [pl.BlockSpec(e.shape, lambda bi, h: (0, 0)) for e in extra]
    y_ctx = pl.pallas_call(
        functools.partial(body, latent=False),
        out_shape=jax.ShapeDtypeStruct((b, ctx_len, heads * LANES), BF16),
        grid=(b, heads),
        in_specs=whole2 + [
            pl.BlockSpec((1, ctx_len, qk_width), lambda bi, h: (bi, cb, h)),
            pl.BlockSpec((1, ctx_len, qk_width), lambda bi, h: (bi, cb, h)),
            pl.BlockSpec((1, ctx_len, LANES), lambda bi, h: (bi, cb, h))],
        out_specs=pl.BlockSpec((1, ctx_len, LANES), lambda bi, h: (bi, 0, h)),
        scratch_shapes=scratch(ctx_len),
        compiler_params=pltpu.CompilerParams(
            dimension_semantics=("parallel", "parallel"), vmem_limit_bytes=VMEM_LIMIT),
        name=name + "_ctx",
    )(*extra, q, k, v)
    return (y, y_ctx)


def _mla_attention(q, k, v, seq, ctx_queries):
    return _full_attention(q, k, v, seq, MLA_TK, C_HEADS, C_QK_PAD, 1, _mla_queries, _mla_finish, [],
                           ctx_queries, "mla_attention")


def _diff_attention(q, k, v, lam_vecs, gsub, lam_init, seq, ctx_queries):
    return _full_attention(q, k, v, seq, DIFF_TK, D_HEADS, 2 * D_HD, 2, _diff_queries,
                           functools.partial(_diff_finish, lam_init=lam_init), [lam_vecs, gsub],
                           ctx_queries, "diff_attention")


def _merge_ffn_kernel(*refs, n_stream, n_split, lat_tiles):
    yc_refs = refs[n_stream:n_stream + n_split]
    yd_refs = refs[n_stream + n_split:n_stream + 2 * n_split]
    (ya_ref, hf_ref, hb_ref, bo_ref, gate_ref, wbr_ref, wout_ref,
     mod_ref, gpost_ref, fpre_ref, fpost_ref, wg_ref, wu_ref, wd_ref, o_ref) = refs[n_stream + 2 * n_split:]
    d = o_ref.shape[2]
    yb = (bo_ref[0].astype(F32) * (hf_ref[0].astype(F32) + hb_ref[0].astype(F32))).astype(BF16)
    ys = (ya_ref[0], yb, _stream_tile(yc_refs, lat_tiles), _stream_tile(yd_refs, lat_tiles))
    merged = None
    for i, y in enumerate(ys):
        g = gate_ref[0, :, i * d:(i + 1) * d].astype(F32)
        term = g * jnp.dot(y, wbr_ref[i], preferred_element_type=F32)
        merged = term if merged is None else merged + term
    out = jnp.dot(merged.astype(BF16), wout_ref[...], preferred_element_type=F32)
    x = _stream_tile(refs[:n_stream], lat_tiles) + mod_ref[0, 2:3, :] * _rms(out, gpost_ref[...])
    u = (_rms(x, fpre_ref[...]) * (1.0 + mod_ref[0, 4:5, :]) + mod_ref[0, 3:4, :]).astype(BF16)
    gate = jnp.dot(u, wg_ref[...], preferred_element_type=F32)
    up = jnp.dot(u, wu_ref[...], preferred_element_type=F32)
    act = (gate * _sigmoid(gate) * up).astype(BF16)
    out = jnp.dot(act, wd_ref[...], preferred_element_type=F32)
    o_ref[0] = x + mod_ref[0, 5:6, :] * _rms(out, fpost_ref[...])


def _merge_ffn(p, ya, hf, hb, yc, yd, wbr, wout, stream, modall, gpost, fpre, fpost, wg, wu, wd, seq, rows):
    b, _, d = stream[0].shape
    tp = TOKEN_TILE
    assert len(yc) == len(yd) and (len(yc) == 2 or rows == seq), "context rows need the mixers' context outputs"
    tok = lambda width: pl.BlockSpec((1, tp, width), lambda bi, t: (bi, t, 0))
    vec = _resident((1, d))
    return pl.pallas_call(
        functools.partial(_merge_ffn_kernel, n_stream=len(stream), n_split=len(yc), lat_tiles=seq // tp),
        out_shape=jax.ShapeDtypeStruct((b, rows, d), F32),
        grid=(b, rows // tp),
        in_specs=_stream_specs(stream, tp) + _stream_specs(yc, tp) + _stream_specs(yd, tp) + [tok(BRANCH_W)] * 4 + [
            tok(N_BRANCH * d), _resident(wbr.shape), _resident(wout.shape),
            _mod_spec(d, seq // tp), vec, vec, vec,
            _resident(wg.shape), _resident(wu.shape), _resident(wd.shape)],
        out_specs=tok(d),
        compiler_params=pltpu.CompilerParams(
            dimension_semantics=("parallel", "parallel"), vmem_limit_bytes=VMEM_LIMIT),
        name="merge_ffn",
    )(*stream, *yc, *yd, ya, hf, hb, p['bo'], p['gate'], wbr, wout, modall, gpost, fpre, fpost, wg, wu, wd)


def _rope_tables(n_rows, ctx_len):
    row = jnp.repeat(jnp.arange(n_rows, dtype=F32), GRID_W)
    col = jnp.tile(jnp.arange(GRID_W, dtype=F32), n_rows)
    quarter = ROPE_DIM // 4
    inv_freq = ROPE_BASE ** (-jnp.arange(quarter, dtype=F32) / quarter)
    ang_r, ang_c = row[:, None] * inv_freq, col[:, None] * inv_freq
    ang = jnp.concatenate([ang_r, ang_r, ang_c, ang_c], axis=-1)
    cos, sin = jnp.cos(ang), jnp.sin(ang)
    cos = jnp.concatenate([cos, jnp.ones((ctx_len, ROPE_DIM), F32)], axis=0)
    sin = jnp.concatenate([sin, jnp.zeros((ctx_len, ROPE_DIM), F32)], axis=0)
    cos, sin = jnp.tile(cos, (1, LANES // ROPE_DIM)), jnp.tile(sin, (1, LANES // ROPE_DIM))
    first = (jnp.arange(LANES) % (ROPE_DIM // 2)) < quarter
    return cos, jnp.where(first, -sin, 0.0), jnp.where(first, 0.0, sin)


def _a_head_perm():
    idx = []
    for j in range(A_HEADS // 2):
        for half in range(2):
            head = j + (A_HEADS // 2) * half
            idx.extend(range(head * A_HD, (head + 1) * A_HD))
    return np.asarray(idx)


def _prep_layer(w_in, b_gate_bias, c_g_q, c_g_kv, c_w_uq, c_w_uk, c_w_uv, w_branch):
    d = w_in.shape[0]
    splits = (A_Q_W, A_KV_W, A_KV_W, B_W, B_W, B_W, B_GATES, B_W, C_Q_LORA, C_KV_LORA, C_ROPE,
              D_QK_W, D_QK_W, D_V_W, N_BRANCH * d)
    offs = np.concatenate([[0], np.cumsum(splits)])
    part = [w_in[:, offs[i]:offs[i + 1]] for i in range(len(splits))]
    (a_q, a_k, a_v, b_q, b_k, b_v, b_g, b_o, c_q, c_kv, c_kr, d_q, d_k, d_v, gate) = part
    perm = _a_head_perm()
    zeros = lambda n: jnp.zeros((d, n), w_in.dtype)
    uq = c_w_uq.reshape(C_Q_LORA, C_HEADS, C_NOPE + C_ROPE)
    uq = jnp.pad(uq, ((0, 0), (0, 0), (0, C_QK_PAD - C_NOPE - C_ROPE))).reshape(C_Q_LORA, C_HEADS * C_QK_PAD)
    w = {
        'wa': jnp.concatenate([a_q[:, perm], a_k, a_v], axis=1).astype(BF16),
        'wb': jnp.concatenate([b_q, b_v, b_o], axis=1).astype(BF16),
        'wgt': jnp.concatenate([b_k, b_g], axis=1).T.astype(BF16),
        'wc': jnp.concatenate([c_q, c_kv, c_kr, zeros(LANES - C_ROPE)], axis=1).astype(BF16),
        'wd': jnp.concatenate([d_q, d_k, d_v], axis=1).astype(BF16),
        'wgate': gate.astype(BF16),
        'gbiast': b_gate_bias.reshape(B_GATES, 1),
        'cgq': c_g_q.reshape(1, -1),
        'cgkv': c_g_kv.reshape(1, -1),
        'wuq': uq.astype(BF16),
        'wukv': jnp.concatenate([c_w_uk, c_w_uv], axis=1).astype(BF16),
    }
    wbr = jnp.concatenate([w_branch[0:1, perm], w_branch[1:]], axis=0).astype(BF16)
    return w, wbr


def kernel(x, c, ctx, c_ctx, w_mod, b_mod, g_mix_pre, g_mix_post, g_ffn_pre, g_ffn_post, w_in, a_sink,
           b_gate_bias, c_g_q, c_g_kv, c_w_uq, c_w_uk, c_w_uv, d_lam_q1, d_lam_k1, d_lam_q2, d_lam_k2,
           d_g_sub, w_branch, w_out, w_gate_up, w_down):
    b, seq, d = x.shape
    ctx_len = ctx.shape[1]
    depth = w_in.shape[0]
    tables = _rope_tables(seq // GRID_W, ctx_len)

    rows = jnp.concatenate([c, c_ctx[None, :], jnp.zeros((8 - b - 1, d), F32)], axis=0)
    mods = _modulation(rows, w_mod, b_mod).reshape(depth, 8, 6, d)

    assert ctx_len == TOKEN_TILE == WINDOW_TILE, "the context must be exactly one token tile"
    stream = (x, ctx)
    for l in range(depth):
        with_ctx = l < depth - 1
        modall = jnp.stack([jnp.broadcast_to(mods[l, b], (b, 6, d)), mods[l, :b]], axis=1).reshape(2 * b, 6, d)
        w, wbr = _prep_layer(w_in[l], b_gate_bias[l], c_g_q[l], c_g_kv[l], c_w_uq[l], c_w_uk[l],
                             c_w_uv[l], w_branch[l])
        lam_init = 0.8 - 0.6 * math.exp(-0.3 * l)
        lam_vecs = jnp.stack([d_lam_q1[l], d_lam_k1[l], d_lam_q2[l], d_lam_k2[l]], axis=0)
        hidden = w_down.shape[1]

        p = _project(stream, modall, g_mix_pre[l].reshape(1, d), tables, w)
        ya = _window_attention(p['aq'], p['ak'], p['av'], a_sink[l], ctx_len)
        hf, hb = _mlstm(p['bq'], p['bk'], p['bv'], p['bgt'], ctx_len)
        yc = _mla_attention(p['cq'], p['ck'], p['cv'], seq, with_ctx)
        yd = _diff_attention(p['dq'], p['dk'], p['dv'], lam_vecs, d_g_sub[l].reshape(1, -1), lam_init,
                             seq, with_ctx)
        rows = seq + ctx_len if with_ctx else seq
        hidden_out = _merge_ffn(
            p, ya, hf, hb, yc, yd, wbr, w_out[l].astype(BF16), stream, modall,
            g_mix_post[l].reshape(1, d), g_ffn_pre[l].reshape(1, d), g_ffn_post[l].reshape(1, d),
            w_gate_up[l, :, :hidden].astype(BF16), w_gate_up[l, :, hidden:].astype(BF16),
            w_down[l].astype(BF16), seq, rows)
        stream = (hidden_out,)
    return stream[0]
```

```python
import functools
import math

import numpy as np
import jax
import jax.numpy as jnp
from jax import lax
from jax.experimental import pallas as pl
from jax.experimental.pallas import tpu as pltpu

F32 = jnp.float32
BF16 = jnp.bfloat16

GRID_W = 64
ROPE_BASE = 10000.0
NORM_EPS = 1e-6
A_HEADS, A_KV_HEADS, A_HD, A_WINDOW = 8, 2, 64, 128
B_HEADS, B_HD, B_CHUNK = 4, 128, 128
C_HEADS, C_Q_LORA, C_KV_LORA, C_NOPE, C_ROPE, C_VD = 4, 512, 256, 128, 64, 128
D_HEADS, D_HD, D_VD = 4, 64, 128
N_BRANCH = 4
BRANCH_W = 512
ROPE_DIM = 64
A_Q_W, A_KV_W = A_HEADS * A_HD, A_KV_HEADS * A_HD
B_W, B_GATES = B_HEADS * B_HD, 4 * B_HEADS
C_KN_W, C_V_W = C_HEADS * C_NOPE, C_HEADS * C_VD
D_QK_W, D_V_W = D_HEADS * 2 * D_HD, D_HEADS * D_VD
C_QK_PAD = 256

LANES = 128
V7X_VMEM_BYTES = 64 * 1024 * 1024
VMEM_LIMIT = V7X_VMEM_BYTES * 7 // 8
MASKED = -1e30

TOKEN_TILE = 256
WINDOW_TILE = 256
FLASH_TQ = 2048
MLA_TK = 8192
DIFF_TK = 4096
FLASH_SUB = 256
MLSTM_STEP_CHUNKS = 2
LOG2E = math.log2(math.e)


def _resident(shape):
    nd = len(shape)
    return pl.BlockSpec(shape, lambda *_: (0,) * nd, pipeline_mode=pl.Buffered(1))


def _mod_spec(d, ctx_tile):
    return pl.BlockSpec((1, 6, d), lambda bi, t: (bi * 2 + jnp.where(t == ctx_tile, 0, 1), 0, 0))


def _sigmoid(x):
    return 1.0 / (1.0 + jnp.exp(-x))


def _log_sigmoid(x):
    return jnp.minimum(x, 0.0) - jnp.log(1.0 + jnp.exp(-jnp.abs(x)))


def _rms(x, g):
    return x * lax.rsqrt(jnp.mean(x * x, axis=-1, keepdims=True) + NORM_EPS) * g


def _rope(x, cos, sin_lo, sin_hi):
    quarter = ROPE_DIM // 4
    return x * cos + pltpu.roll(x, LANES - quarter, 1) * sin_lo + pltpu.roll(x, quarter, 1) * sin_hi


def _dot_nt(a, b):
    return lax.dot_general(a, b, (((1,), (1,)), ((), ())), preferred_element_type=F32)


def _mod_kernel(c_ref, w_ref, b_ref, o_ref):
    c = c_ref[...]
    s = c * _sigmoid(c)
    o_ref[0] = jnp.dot(s, w_ref[0], preferred_element_type=F32,
                       precision=lax.Precision.HIGHEST) + b_ref[0]


def _modulation(rows, w_mod, b_mod):
    depth, d, n = w_mod.shape
    tn = d
    return pl.pallas_call(
        _mod_kernel,
        out_shape=jax.ShapeDtypeStruct((depth, rows.shape[0], n), F32),
        grid=(depth, n // tn),
        in_specs=[pl.BlockSpec(rows.shape, lambda l, j: (0, 0)),
                  pl.BlockSpec((1, d, tn), lambda l, j: (l, 0, j)),
                  pl.BlockSpec((1, 1, tn), lambda l, j: (l, 0, j))],
        out_specs=pl.BlockSpec((1, rows.shape[0], tn), lambda l, j: (l, 0, j)),
        name="modulation",
    )(rows, w_mod, b_mod.reshape(depth, 1, n))


def _stream_specs(stream, tp):
    d = stream[0].shape[2]
    if len(stream) == 1:
        return [pl.BlockSpec((1, tp, d), lambda bi, t: (bi, t, 0))]
    last = stream[0].shape[1] // tp - 1
    return [pl.BlockSpec((1, tp, d), lambda bi, t: (bi, jnp.minimum(t, last), 0)),
            pl.BlockSpec((1, tp, d), lambda bi, t: (bi, 0, 0))]


def _stream_tile(refs, lat_tiles):
    if len(refs) == 1:
        return refs[0][0]
    return jnp.where(pl.program_id(1) == lat_tiles, refs[1][0], refs[0][0])


def _proj_kernel(*refs, n_stream, lat_tiles):
    (mod_ref, gpre_ref, cos_ref, slo_ref, shi_ref,
     wa_ref, wb_ref, wgt_ref, wc_ref, wd_ref, wgate_ref,
     gbiast_ref, cgq_ref, cgkv_ref, wuq_ref, wukv_ref,
     aq_ref, ak_ref, av_ref, bq_ref, bk_ref, bv_ref, bo_ref, bgt_ref,
     cq_ref, ck_ref, cv_ref, dq_ref, dk_ref, dv_ref, gate_ref) = refs[n_stream:]
    x = _stream_tile(refs[:n_stream], lat_tiles)
    y = _rms(x, gpre_ref[...])
    u = (y * (1.0 + mod_ref[0, 1:2, :]) + mod_ref[0, 0:1, :]).astype(BF16)
    cos, slo, shi = cos_ref[...], slo_ref[...], shi_ref[...]

    def rope(v):
        return _rope(v, cos, slo, shi)

    def chunks(v, n):
        return [v[:, i * LANES:(i + 1) * LANES] for i in range(n)]

    pa = jnp.dot(u, wa_ref[...], preferred_element_type=F32)
    for i, v in enumerate(chunks(pa, A_Q_W // LANES)):
        aq_ref[0, :, i * LANES:(i + 1) * LANES] = (rope(v) * (A_HD ** -0.5 * LOG2E)).astype(BF16)
    ak_ref[0] = rope(pa[:, A_Q_W:A_Q_W + A_KV_W]).astype(BF16)
    av_ref[0] = pa[:, A_Q_W + A_KV_W:A_Q_W + 2 * A_KV_W].astype(BF16)

    pb = jnp.dot(u, wb_ref[...], preferred_element_type=F32)
    bq_ref[0] = pb[:, 0:B_W].astype(BF16)
    bv_ref[0] = pb[:, B_W:2 * B_W].astype(BF16)
    bo_ref[0] = _sigmoid(pb[:, 2 * B_W:3 * B_W]).astype(BF16)
    kg = _dot_nt(wgt_ref[...], u)
    bk_ref[0] = (kg[0:B_W] * (B_HD ** -0.5)).astype(BF16)
    bgt_ref[0] = kg[B_W:B_W + B_GATES] + gbiast_ref[...]

    pc = jnp.dot(u, wc_ref[...], preferred_element_type=F32)
    cqn = _rms(pc[:, 0:C_Q_LORA], cgq_ref[...]).astype(BF16)
    ckvn = _rms(pc[:, C_Q_LORA:C_Q_LORA + C_KV_LORA], cgkv_ref[...]).astype(BF16)
    kr = rope(pc[:, C_Q_LORA + C_KV_LORA:C_Q_LORA + C_KV_LORA + LANES]).astype(BF16)
    cq = jnp.dot(cqn, wuq_ref[...], preferred_element_type=F32)
    ckv = jnp.dot(ckvn, wukv_ref[...], preferred_element_type=F32)
    c_scale = (C_NOPE + C_ROPE) ** -0.5 * LOG2E
    for h in range(C_HEADS):
        o = h * C_QK_PAD
        cq_ref[0, :, o:o + LANES] = (cq[:, o:o + LANES] * c_scale).astype(BF16)
        cq_ref[0, :, o + LANES:o + 2 * LANES] = (rope(cq[:, o + LANES:o + 2 * LANES]) * c_scale).astype(BF16)
        ck_ref[0, :, o:o + LANES] = ckv[:, h * LANES:(h + 1) * LANES].astype(BF16)
        ck_ref[0, :, o + LANES:o + 2 * LANES] = kr
    cv_ref[0] = ckv[:, C_KN_W:C_KN_W + C_V_W].astype(BF16)

    pd = jnp.dot(u, wd_ref[...], preferred_element_type=F32)
    for i in range(D_QK_W // LANES):
        sl = slice(i * LANES, (i + 1) * LANES)
        dq_ref[0, :, sl] = (rope(pd[:, sl]) * (D_HD ** -0.5 * LOG2E)).astype(BF16)
        dk_ref[0, :, sl] = rope(pd[:, D_QK_W + i * LANES:D_QK_W + (i + 1) * LANES]).astype(BF16)
    dv_ref[0] = pd[:, 2 * D_QK_W:2 * D_QK_W + D_V_W].astype(BF16)

    for i in range(N_BRANCH):
        sl = slice(i * x.shape[1], (i + 1) * x.shape[1])
        gate_ref[0, :, sl] = _sigmoid(jnp.dot(u, wgate_ref[:, sl], preferred_element_type=F32)).astype(BF16)


def _project(stream, modall, gpre, tables, w):
    b, _, d = stream[0].shape
    tt = sum(s.shape[1] for s in stream)
    tp = TOKEN_TILE
    nt = tt // tp
    cos, slo, shi = tables

    def tok(width):
        return pl.BlockSpec((1, tp, width), lambda bi, t: (bi, t, 0))

    tab = pl.BlockSpec((tp, LANES), lambda bi, t: (t, 0))
    in_specs = _stream_specs(stream, tp) + [
        _mod_spec(d, nt - 1),
        _resident((1, d)), tab, tab, tab,
        _resident(w['wa'].shape), _resident(w['wb'].shape),
        _resident(w['wgt'].shape), _resident(w['wc'].shape), _resident(w['wd'].shape),
        _resident(w['wgate'].shape), _resident(w['gbiast'].shape),
        _resident(w['cgq'].shape), _resident(w['cgkv'].shape), _resident(w['wuq'].shape),
        _resident(w['wukv'].shape),
    ]
    widths = dict(aq=A_Q_W, ak=A_KV_W, av=A_KV_W, bq=B_W, bv=B_W, bo=B_W,
                  cq=C_HEADS * C_QK_PAD, ck=C_HEADS * C_QK_PAD, cv=C_V_W,
                  dq=D_QK_W, dk=D_QK_W, dv=D_V_W, gate=N_BRANCH * d)
    tokens_on_lanes = dict(bk=(B_W, BF16), bgt=(B_GATES, F32))
    names = ['aq', 'ak', 'av', 'bq', 'bk', 'bv', 'bo', 'bgt',
             'cq', 'ck', 'cv', 'dq', 'dk', 'dv', 'gate']
    out_shape, out_specs = [], []
    for n in names:
        if n in tokens_on_lanes:
            feat, dtype = tokens_on_lanes[n]
            out_shape.append(jax.ShapeDtypeStruct((b, feat, tt), dtype))
            out_specs.append(pl.BlockSpec((1, feat, tp), lambda bi, t: (bi, 0, t)))
        else:
            out_shape.append(jax.ShapeDtypeStruct((b, tt, widths[n]), BF16))
            out_specs.append(tok(widths[n]))
    outs = pl.pallas_call(
        functools.partial(_proj_kernel, n_stream=len(stream), lat_tiles=nt - 1),
        out_shape=out_shape,
        grid=(b, nt),
        in_specs=in_specs,
        out_specs=out_specs,
        compiler_params=pltpu.CompilerParams(
            dimension_semantics=("parallel", "parallel"), vmem_limit_bytes=VMEM_LIMIT),
        name="project",
    )(*stream, modall, gpre, cos, slo, shi, w['wa'], w['wb'], w['wgt'], w['wc'], w['wd'],
      w['wgate'], w['gbiast'], w['cgq'], w['cgkv'], w['wuq'], w['wukv'])
    return dict(zip(names, outs))


def _window_kernel(sink_ref, q_ref, kc_ref, kp_ref, kk_ref, kn_ref,
                   vc_ref, vp_ref, vk_ref, vn_ref, o_ref, *, lat_tiles):
    t = pl.program_id(1)
    hb = WINDOW_TILE // 2
    n_ctx = kc_ref.shape[1]
    group = A_HEADS // A_KV_HEADS
    width = n_ctx + 3 * hb
    kwin = jnp.concatenate([kp_ref[0], kk_ref[0], kn_ref[0]], axis=0)
    vwin = jnp.concatenate([vp_ref[0], vk_ref[0], vn_ref[0]], axis=0)
    prev_ok = jnp.logical_and(t >= 1, t < lat_tiles).astype(jnp.int32)
    cur_ok = (t < lat_tiles).astype(jnp.int32)
    next_ok = (t < lat_tiles - 1).astype(jnp.int32)
    row = lax.broadcasted_iota(jnp.int32, (hb, width), 0)
    col = lax.broadcasted_iota(jnp.int32, (hb, width), 1)
    wcol = col - n_ctx
    near = jnp.abs(wcol - hb - row) <= A_WINDOW
    low = lax.broadcasted_iota(jnp.int32, (hb, LANES), 1) < A_HD
    for blk in range(2):
        k0 = hb * (1 + blk)
        kb = jnp.concatenate([kc_ref[0], kwin[k0:k0 + 3 * hb]], axis=0)
        vb = jnp.concatenate([vc_ref[0], vwin[k0:k0 + 3 * hb]], axis=0)
        vb_ext = jnp.concatenate([vb, jnp.ones_like(vb)], axis=1)
        if blk == 0:
            seg_ok = jnp.where(wcol < hb, prev_ok, cur_ok)
        else:
            seg_ok = jnp.where(wcol < 2 * hb, cur_ok, next_ok)
        valid = jnp.logical_or(col < n_ctx, jnp.logical_and(near, seg_ok > 0))
        bias = jnp.tile(jnp.where(valid, 0.0, MASKED), (group, 1))
        rows = slice(blk * hb, (blk + 1) * hb)
        outs = []
        for g in range(A_KV_HEADS):
            keep = low if g == 0 else jnp.logical_not(low)
            qs, sinks = [], []
            for j in range(group):
                qb = q_ref[0, rows, j * LANES:(j + 1) * LANES]
                qs.append(jnp.where(keep, qb, jnp.zeros_like(qb)))
                sinks.append(jnp.full((hb, LANES), sink_ref[j + group * g] * LOG2E, F32))
            sink = jnp.concatenate(sinks, axis=0)
            s = _dot_nt(jnp.concatenate(qs, axis=0), kb) + bias
            m = jnp.maximum(jnp.max(s, axis=-1, keepdims=True), sink)
            e = jnp.exp2(s - jnp.tile(m, (1, width // LANES)))
            pv = jnp.dot(e.astype(BF16), vb_ext, preferred_element_type=F32)
            outs.append(pv[:, :LANES] / (pv[:, LANES:] + jnp.exp2(sink - m)))
        for j in range(group):
            o_ref[0, rows, j * LANES:(j + 1) * LANES] = jnp.where(
                low, outs[0][j * hb:(j + 1) * hb], outs[1][j * hb:(j + 1) * hb]).astype(BF16)


def _window_attention(q, k, v, sink, ctx_len):
    b, tt, _ = q.shape
    tw = WINDOW_TILE
    assert ctx_len == tw == 2 * A_WINDOW, "context is one key tile; a query block is one window radius"
    nt = tt // tw
    nl = nt - 1
    kv = lambda f: pl.BlockSpec((1, tw, LANES), lambda bi, t: (bi, f(t), 0))
    specs = [kv(lambda t: nl), kv(lambda t: jnp.maximum(t - 1, 0)), kv(lambda t: t),
             kv(lambda t: jnp.minimum(t + 1, nl - 1))]
    return pl.pallas_call(
        functools.partial(_window_kernel, lat_tiles=nl),
        out_shape=jax.ShapeDtypeStruct((b, tt, A_Q_W), BF16),
        grid=(b, nt),
        in_specs=[pl.BlockSpec(memory_space=pltpu.SMEM),
                  pl.BlockSpec((1, tw, A_Q_W), lambda bi, t: (bi, t, 0))] + specs + specs,
        out_specs=pl.BlockSpec((1, tw, A_Q_W), lambda bi, t: (bi, t, 0)),
        compiler_params=pltpu.CompilerParams(
            dimension_semantics=("parallel", "parallel"), vmem_limit_bytes=VMEM_LIMIT),
        name="window_attention",
    )(sink, q, k, k, k, k, v, v, v, v)


def _bf16_terms(x):
    hi = x.astype(BF16)
    r1 = x - hi.astype(F32)
    mid = r1.astype(BF16)
    return [hi, mid, (r1 - mid.astype(F32)).astype(BF16)]


def _block_diag(a, b):
    return jnp.concatenate([jnp.concatenate([a, jnp.zeros_like(b)], axis=1),
                            jnp.concatenate([jnp.zeros_like(a), b], axis=1)], axis=0)


def _mlstm_kernel(qf_ref, kf_ref, vf_ref, gtf_ref, qb_ref, kb_ref, vb_ref, gtb_ref,
                  hf_ref, hb_ref, c_ref, m_ref):
    step = pl.program_id(1)

    @pl.when(step == 0)
    def _():
        c_ref[...] = jnp.zeros_like(c_ref)
        m_ref[...] = jnp.zeros_like(m_ref)

    L = B_CHUNK
    row = lax.broadcasted_iota(jnp.int32, (L, L), 0)
    col = lax.broadcasted_iota(jnp.int32, (L, L), 1)
    dirs = ((qf_ref, kf_ref, vf_ref, gtf_ref, hf_ref), (qb_ref, kb_ref, vb_ref, gtb_ref, hb_ref))
    for d, (q_ref, kt_ref, v_ref, gt_ref, out_ref) in enumerate(dirs):
        before = (col <= row) if d == 0 else (col >= row)
        cum3 = jnp.tile(before.astype(BF16), (1, 3))
        cum_t3 = jnp.tile(((row <= col) if d == 0 else (row >= col)).astype(BF16), (3, 1))
        last = L - 1 if d == 0 else 0
        n_chunks = q_ref.shape[1] // L
        for c in (range(n_chunks) if d == 0 else reversed(range(n_chunks))):
            rows = slice(c * L, (c + 1) * L)
            gt = gt_ref[0, :, rows]
            lf_all = _log_sigmoid(gt)
            frow_all = jnp.dot(jnp.concatenate(_bf16_terms(lf_all), axis=1), cum_t3, preferred_element_type=F32)
            cf0 = (2 * d + 1) * B_HEADS
            lf_heads = jnp.concatenate([jnp.broadcast_to(lf_all[cf0 + h:cf0 + h + 1, :], (LANES, L))
                                        for h in range(B_HEADS)], axis=0)
            fcol_all = _dot_nt(cum3, jnp.concatenate(_bf16_terms(lf_heads), axis=1))
            for h0 in range(0, B_HEADS, 2):
                pair = (h0, h0 + 1)
                kts = [kt_ref[0, h * B_HD:(h + 1) * B_HD, rows] for h in pair]
                s_pair = jnp.dot(q_ref[0, rows, h0 * B_HD:(h0 + 2) * B_HD], _block_diag(*kts),
                                 preferred_element_type=F32)
                weighted, v_exts, decays, cts = [], [], [], []
                for j, h in enumerate(pair):
                    idx = d * B_HEADS + h
                    ci, cf = (2 * d) * B_HEADS + h, cf0 + h
                    i_row, f_row = gt[ci:ci + 1, :], frow_all[cf:cf + 1, :]
                    f_col = fcol_all[:, h * LANES:(h + 1) * LANES]
                    a_row = i_row - f_row
                    b_last = f_row[:, last:last + 1]
                    m_prev = m_ref[idx][:, 0:1]
                    q = q_ref[0, rows, h * B_HD:(h + 1) * B_HD]
                    v = v_ref[0, rows, h * B_HD:(h + 1) * B_HD]
                    v_ext = jnp.concatenate([v, jnp.ones_like(v)], axis=1)
                    dmat = jnp.where(before, f_col + a_row, -jnp.inf)
                    m_t = jnp.maximum(f_col + m_prev, jnp.max(dmat, axis=-1, keepdims=True))
                    wqk = jnp.exp(dmat - m_t) * s_pair[:, j * L:(j + 1) * L]
                    a = jnp.exp(f_col + m_prev - m_t)
                    ct = c_ref[idx]
                    num_den = jnp.dot(
                        jnp.concatenate([wqk.astype(BF16), (a * q.astype(F32)).astype(BF16)], axis=1),
                        jnp.concatenate([v_ext, ct.astype(BF16)], axis=0), preferred_element_type=F32)
                    hval = num_den[:, :B_HD] / jnp.maximum(jnp.abs(num_den[:, B_HD:]), jnp.exp(-m_t))
                    out_ref[0, rows, h * B_HD:(h + 1) * B_HD] = hval.astype(BF16)
                    g_row = b_last + a_row
                    m_new = jnp.maximum(b_last + m_prev, jnp.max(g_row, axis=-1, keepdims=True))
                    decays.append(jnp.exp(b_last + m_prev - m_new))
                    weighted.append((kts[j].astype(F32) * jnp.exp(g_row - m_new)).astype(BF16))
                    v_exts.append(v_ext)
                    cts.append(ct)
                    m_ref[idx] = jnp.broadcast_to(m_new, (1, LANES))
                update = jnp.dot(_block_diag(*weighted), jnp.concatenate(v_exts, axis=0),
                                 preferred_element_type=F32)
                for j, h in enumerate(pair):
                    c_ref[d * B_HEADS + h] = decays[j] * cts[j] + update[j * B_HD:(j + 1) * B_HD]


def _mlstm(q, k, v, gt, ctx_len):
    b, tt, w = q.shape
    L = MLSTM_STEP_CHUNKS * B_CHUNK
    assert ctx_len % L == 0 and tt % L == 0
    nc = tt // L
    cc = ctx_len // L

    nl = nc - cc

    def fwd(s):
        return jnp.where(s < cc, nl + s, s - cc)

    def bwd(s):
        return nc - 1 - s

    def specs(f):
        tok = pl.BlockSpec((1, L, w), lambda bi, s: (bi, f(s), 0))
        lanes = lambda feat: pl.BlockSpec((1, feat, L), lambda bi, s: (bi, 0, f(s)))
        return [tok, lanes(w), tok, lanes(B_GATES)]

    out = jax.ShapeDtypeStruct((b, tt, w), BF16)
    return pl.pallas_call(
        _mlstm_kernel,
        out_shape=[out, out],
        grid=(b, nc),
        in_specs=specs(fwd) + specs(bwd),
        out_specs=[pl.BlockSpec((1, L, w), lambda bi, s: (bi, fwd(s), 0)),
                   pl.BlockSpec((1, L, w), lambda bi, s: (bi, bwd(s), 0))],
        scratch_shapes=[pltpu.VMEM((2 * B_HEADS, B_HD, 2 * B_HD), F32),
                        pltpu.VMEM((2 * B_HEADS, 1, LANES), F32)],
        compiler_params=pltpu.CompilerParams(
            dimension_semantics=("parallel", "arbitrary"), vmem_limit_bytes=VMEM_LIMIT),
        name="mlstm",
    )(q, k, v, gt, q, k, v, gt)


def _softmax_update(q, k, v_ext, m_old, acc):
    s = _dot_nt(q, k)
    m_new = jnp.maximum(m_old, jnp.max(s, axis=-1, keepdims=True))
    p = jnp.exp2(s - jnp.tile(m_new, (1, s.shape[1] // LANES)))
    alpha = jnp.exp2(m_old - m_new)
    acc = jnp.tile(alpha, (1, 2)) * acc + jnp.dot(p.astype(BF16), v_ext, preferred_element_type=F32)
    return m_new, acc


def _attend(qs, stats, k_ref, v_ref):
    vals = [(m_ref[...], acc_ref[...]) for m_ref, acc_ref in stats]
    sub = min(FLASH_SUB, k_ref.shape[1])
    for c in range(k_ref.shape[1] // sub):
        k = k_ref[0, c * sub:(c + 1) * sub, :]
        v = v_ref[0, c * sub:(c + 1) * sub, :]
        v_ext = jnp.concatenate([v, jnp.ones_like(v)], axis=1)
        vals = [_softmax_update(q, k, v_ext, m, acc) for q, (m, acc) in zip(qs, vals)]
    for (m_ref, acc_ref), (m, acc) in zip(stats, vals):
        m_ref[...] = m
        acc_ref[...] = acc


def _flash_init(stats):
    for m_ref, acc_ref in stats:
        m_ref[...] = jnp.full_like(m_ref, -jnp.inf)
        acc_ref[...] = jnp.zeros_like(acc_ref)


def _normalized(acc_ref):
    return acc_ref[:, :LANES] / acc_ref[:, LANES:]


def _mla_queries(q):
    return [q]


def _mla_finish(stats):
    return _normalized(stats[0][1])


def _diff_queries(q):
    low = lax.broadcasted_iota(jnp.int32, q.shape, 1) < D_HD
    zero = jnp.zeros_like(q)
    return [jnp.where(low, q, zero), jnp.where(low, zero, q)]


def _diff_finish(stats, lam_ref, gsub_ref, *, lam_init):
    lq1, lk1, lq2, lk2 = (lam_ref[r:r + 1, :] for r in range(4))
    lam = (jnp.exp(jnp.sum(lq1 * lk1, axis=-1, keepdims=True))
           - jnp.exp(jnp.sum(lq2 * lk2, axis=-1, keepdims=True)) + lam_init)
    o = _normalized(stats[0][1]) - lam * _normalized(stats[1][1])
    return _rms(o, gsub_ref[...]) * (1.0 - lam_init)


def _flash_kernel(*refs, n_extra, n_maps, queries, finish, latent):
    extra, refs = refs[:n_extra], refs[n_extra:]
    if latent:
        q_ref, kc_ref, vc_ref, k_ref, v_ref, o_ref = refs[:6]
        scratch = refs[6:]
    else:
        q_ref, kc_ref, vc_ref, o_ref = refs[:4]
        scratch = refs[4:]
    stats = [(scratch[2 * i], scratch[2 * i + 1]) for i in range(n_maps)]
    qs = queries(q_ref[0])
    if not latent:
        _flash_init(stats)
        _attend(qs, stats, kc_ref, vc_ref)
        o_ref[0] = finish(stats, *extra).astype(BF16)
        return
    j = pl.program_id(3)

    @pl.when(j == 0)
    def _():
        _flash_init(stats)
        _attend(qs, stats, kc_ref, vc_ref)

    _attend(qs, stats, k_ref, v_ref)

    @pl.when(j == pl.num_programs(3) - 1)
    def _():
        o_ref[0] = finish(stats, *extra).astype(BF16)


def _full_attention(q, k, v, seq, key_block, heads, qk_width, n_maps, queries, finish, extra, ctx_queries, name):
    b, tt, _ = q.shape
    ctx_len = tt - seq
    cb = seq // ctx_len
    tq, tk = min(FLASH_TQ, seq), min(key_block, seq)
    assert seq % tq == 0 and seq % tk == 0 and seq % ctx_len == 0

    def scratch(rows):
        return [pltpu.VMEM((rows, LANES), F32), pltpu.VMEM((rows, 2 * LANES), F32)] * n_maps

    body = functools.partial(_flash_kernel, n_extra=len(extra), n_maps=n_maps, queries=queries, finish=finish)
    whole4 = [pl.BlockSpec(e.shape, lambda bi, h, i, j: (0, 0)) for e in extra]
    y = pl.pallas_call(
        functools.partial(body, latent=True),
        out_shape=jax.ShapeDtypeStruct((b, seq, heads * LANES), BF16),
        grid=(b, heads, seq // tq, seq // tk),
        in_specs=whole4 + [
            pl.BlockSpec((1, tq, qk_width), lambda bi, h, i, j: (bi, i, h)),
            pl.BlockSpec((1, ctx_len, qk_width), lambda bi, h, i, j: (bi, cb, h)),
            pl.BlockSpec((1, ctx_len, LANES), lambda bi, h, i, j: (bi, cb, h)),
            pl.BlockSpec((1, tk, qk_width), lambda bi, h, i, j: (bi, j, h)),
            pl.BlockSpec((1, tk, LANES), lambda bi, h, i, j: (bi, j, h))],
        out_specs=pl.BlockSpec((1, tq, LANES), lambda bi, h, i, j: (bi, i, h)),
        scratch_shapes=scratch(tq),
        compiler_params=pltpu.CompilerParams(
            dimension_semantics=("parallel", "parallel", "parallel", "arbitrary"),
            vmem_limit_bytes=VMEM_LIMIT),
        name=name,
    )(*extra, q, k, v, k, v)
    if not ctx_queries:
        return (y,)
    whole2 = [pl.BlockSpec(e.shape, lambda bi, h: (0, 0)) for e in extra]
    y_ctx = pl.pallas_call(
        functools.partial(body, latent=False),
        out_shape=jax.ShapeDtypeStruct((b, ctx_len, heads * LANES), BF16),
        grid=(b, heads),
        in_specs=whole2 + [
            pl.BlockSpec((1, ctx_len, qk_width), lambda bi, h: (bi, cb, h)),
            pl.BlockSpec((1, ctx_len, qk_width), lambda bi, h: (bi, cb, h)),
            pl.BlockSpec((1, ctx_len, LANES), lambda bi, h: (bi, cb, h))],
        out_specs=pl.BlockSpec((1, ctx_len, LANES), lambda bi, h: (bi, 0, h)),
        scratch_shapes=scratch(ctx_len),
        compiler_params=pltpu.CompilerParams(
            dimension_semantics=("parallel", "parallel"), vmem_limit_bytes=VMEM_LIMIT),
        name=name + "_ctx",
    )(*extra, q, k, v)
    return (y, y_ctx)


def _mla_attention(q, k, v, seq, ctx_queries):
    return _full_attention(q, k, v, seq, MLA_TK, C_HEADS, C_QK_PAD, 1, _mla_queries, _mla_finish, [],
                           ctx_queries, "mla_attention")


def _diff_attention(q, k, v, lam_vecs, gsub, lam_init, seq, ctx_queries):
    return _full_attention(q, k, v, seq, DIFF_TK, D_HEADS, 2 * D_HD, 2, _diff_queries,
                           functools.partial(_diff_finish, lam_init=lam_init), [lam_vecs, gsub],
                           ctx_queries, "diff_attention")


def _merge_ffn_kernel(*refs, n_stream, n_split, lat_tiles):
    yc_refs = refs[n_stream:n_stream + n_split]
    yd_refs = refs[n_stream + n_split:n_stream + 2 * n_split]
    (ya_ref, hf_ref, hb_ref, bo_ref, gate_ref, wbr_ref, wout_ref,
     mod_ref, gpost_ref, fpre_ref, fpost_ref, wg_ref, wu_ref, wd_ref, o_ref) = refs[n_stream + 2 * n_split:]
    d = o_ref.shape[2]
    yb = (bo_ref[0].astype(F32) * (hf_ref[0].astype(F32) + hb_ref[0].astype(F32))).astype(BF16)
    ys = (ya_ref[0], yb, _stream_tile(yc_refs, lat_tiles), _stream_tile(yd_refs, lat_tiles))
    merged = None
    for i, y in enumerate(ys):
        g = gate_ref[0, :, i * d:(i + 1) * d].astype(F32)
        term = g * jnp.dot(y, wbr_ref[i], preferred_element_type=F32)
        merged = term if merged is None else merged + term
    out = jnp.dot(merged.astype(BF16), wout_ref[...], preferred_element_type=F32)
    x = _stream_tile(refs[:n_stream], lat_tiles) + mod_ref[0, 2:3, :] * _rms(out, gpost_ref[...])
    u = (_rms(x, fpre_ref[...]) * (1.0 + mod_ref[0, 4:5, :]) + mod_ref[0, 3:4, :]).astype(BF16)
    gate = jnp.dot(u, wg_ref[...], preferred_element_type=F32)
    up = jnp.dot(u, wu_ref[...], preferred_element_type=F32)
    act = (gate * _sigmoid(gate) * up).astype(BF16)
    out = jnp.dot(act, wd_ref[...], preferred_element_type=F32)
    o_ref[0] = x + mod_ref[0, 5:6, :] * _rms(out, fpost_ref[...])


def _merge_ffn(p, ya, hf, hb, yc, yd, wbr, wout, stream, modall, gpost, fpre, fpost, wg, wu, wd, seq, rows):
    b, _, d = stream[0].shape
    tp = TOKEN_TILE
    assert len(yc) == len(yd) and (len(yc) == 2 or rows == seq), "context rows need the mixers' context outputs"
    tok = lambda width: pl.BlockSpec((1, tp, width), lambda bi, t: (bi, t, 0))
    vec = _resident((1, d))
    return pl.pallas_call(
        functools.partial(_merge_ffn_kernel, n_stream=len(stream), n_split=len(yc), lat_tiles=seq // tp),
        out_shape=jax.ShapeDtypeStruct((b, rows, d), F32),
        grid=(b, rows // tp),
        in_specs=_stream_specs(stream, tp) + _stream_specs(yc, tp) + _stream_specs(yd, tp) + [tok(BRANCH_W)] * 4 + [
            tok(N_BRANCH * d), _resident(wbr.shape), _resident(wout.shape),
            _mod_spec(d, seq // tp), vec, vec, vec,
            _resident(wg.shape), _resident(wu.shape), _resident(wd.shape)],
        out_specs=tok(d),
        compiler_params=pltpu.CompilerParams(
            dimension_semantics=("parallel", "parallel"), vmem_limit_bytes=VMEM_LIMIT),
        name="merge_ffn",
    )(*stream, *yc, *yd, ya, hf, hb, p['bo'], p['gate'], wbr, wout, modall, gpost, fpre, fpost, wg, wu, wd)


def _rope_tables(n_rows, ctx_len):
    row = jnp.repeat(jnp.arange(n_rows, dtype=F32), GRID_W)
    col = jnp.tile(jnp.arange(GRID_W, dtype=F32), n_rows)
    quarter = ROPE_DIM // 4
    inv_freq = ROPE_BASE ** (-jnp.arange(quarter, dtype=F32) / quarter)
    ang_r, ang_c = row[:, None] * inv_freq, col[:, None] * inv_freq
    ang = jnp.concatenate([ang_r, ang_r, ang_c, ang_c], axis=-1)
    cos, sin = jnp.cos(ang), jnp.sin(ang)
    cos = jnp.concatenate([cos, jnp.ones((ctx_len, ROPE_DIM), F32)], axis=0)
    sin = jnp.concatenate([sin, jnp.zeros((ctx_len, ROPE_DIM), F32)], axis=0)
    cos, sin = jnp.tile(cos, (1, LANES // ROPE_DIM)), jnp.tile(sin, (1, LANES // ROPE_DIM))
    first = (jnp.arange(LANES) % (ROPE_DIM // 2)) < quarter
    return cos, jnp.where(first, -sin, 0.0), jnp.where(first, 0.0, sin)


def _a_head_perm():
    idx = []
    for j in range(A_HEADS // 2):
        for half in range(2):
            head = j + (A_HEADS // 2) * half
            idx.extend(range(head * A_HD, (head + 1) * A_HD))
    return np.asarray(idx)


def _prep_layer(w_in, b_gate_bias, c_g_q, c_g_kv, c_w_uq, c_w_uk, c_w_uv, w_branch):
    d = w_in.shape[0]
    splits = (A_Q_W, A_KV_W, A_KV_W, B_W, B_W, B_W, B_GATES, B_W, C_Q_LORA, C_KV_LORA, C_ROPE,
              D_QK_W, D_QK_W, D_V_W, N_BRANCH * d)
    offs = np.concatenate([[0], np.cumsum(splits)])
    part = [w_in[:, offs[i]:offs[i + 1]] for i in range(len(splits))]
    (a_q, a_k, a_v, b_q, b_k, b_v, b_g, b_o, c_q, c_kv, c_kr, d_q, d_k, d_v, gate) = part
    perm = _a_head_perm()
    zeros = lambda n: jnp.zeros((d, n), w_in.dtype)
    uq = c_w_uq.reshape(C_Q_LORA, C_HEADS, C_NOPE + C_ROPE)
    uq = jnp.pad(uq, ((0, 0), (0, 0), (0, C_QK_PAD - C_NOPE - C_ROPE))).reshape(C_Q_LORA, C_HEADS * C_QK_PAD)
    w = {
        'wa': jnp.concatenate([a_q[:, perm], a_k, a_v], axis=1).astype(BF16),
        'wb': jnp.concatenate([b_q, b_v, b_o], axis=1).astype(BF16),
        'wgt': jnp.concatenate([b_k, b_g], axis=1).T.astype(BF16),
        'wc': jnp.concatenate([c_q, c_kv, c_kr, zeros(LANES - C_ROPE)], axis=1).astype(BF16),
        'wd': jnp.concatenate([d_q, d_k, d_v], axis=1).astype(BF16),
        'wgate': gate.astype(BF16),
        'gbiast': b_gate_bias.reshape(B_GATES, 1),
        'cgq': c_g_q.reshape(1, -1),
        'cgkv': c_g_kv.reshape(1, -1),
        'wuq': uq.astype(BF16),
        'wukv': jnp.concatenate([c_w_uk, c_w_uv], axis=1).astype(BF16),
    }
    wbr = jnp.concatenate([w_branch[0:1, perm], w_branch[1:]], axis=0).astype(BF16)
    return w, wbr


def kernel(x, c, ctx, c_ctx, w_mod, b_mod, g_mix_pre, g_mix_post, g_ffn_pre, g_ffn_post, w_in, a_sink,
           b_gate_bias, c_g_q, c_g_kv, c_w_uq, c_w_uk, c_w_uv, d_lam_q1, d_lam_k1, d_lam_q2, d_lam_k2,
           d_g_sub, w_branch, w_out, w_gate_up, w_down):
    b, seq, d = x.shape
    ctx_len = ctx.shape[1]
    depth = w_in.shape[0]
    tables = _rope_tables(seq // GRID_W, ctx_len)

    rows = jnp.concatenate([c, c_ctx[None, :], jnp.zeros((8 - b - 1, d), F32)], axis=0)
    mods = _modulation(rows, w_mod, b_mod).reshape(depth, 8, 6, d)

    assert ctx_len == TOKEN_TILE == WINDOW_TILE, "the context must be exactly one token tile"
    stream = (x, ctx)
    for l in range(depth):
        with_ctx = l < depth - 1
        modall = jnp.stack([jnp.broadcast_to(mods[l, b], (b, 6, d)), mods[l, :b]], axis=1).reshape(2 * b, 6, d)
        w, wbr = _prep_layer(w_in[l], b_gate_bias[l], c_g_q[l], c_g_kv[l], c_w_uq[l], c_w_uk[l],
                             c_w_uv[l], w_branch[l])
        lam_init = 0.8 - 0.6 * math.exp(-0.3 * l)
        lam_vecs = jnp.stack([d_lam_q1[l], d_lam_k1[l], d_lam_q2[l], d_lam_k2[l]], axis=0)
        hidden = w_down.shape[1]

        p = _project(stream, modall, g_mix_pre[l].reshape(1, d), tables, w)
        ya = _window_attention(p['aq'], p['ak'], p['av'], a_sink[l], ctx_len)
        hf, hb = _mlstm(p['bq'], p['bk'], p['bv'], p['bgt'], ctx_len)
        yc = _mla_attention(p['cq'], p['ck'], p['cv'], seq, with_ctx)
        yd = _diff_attention(p['dq'], p['dk'], p['dv'], lam_vecs, d_g_sub[l].reshape(1, -1), lam_init,
                             seq, with_ctx)
        rows = seq + ctx_len if with_ctx else seq
        hidden_out = _merge_ffn(
            p, ya, hf, hb, yc, yd, wbr, w_out[l].astype(BF16), stream, modall,
            g_mix_post[l].reshape(1, d), g_ffn_pre[l].reshape(1, d), g_ffn_post[l].reshape(1, d),
            w_gate_up[l, :, :hidden].astype(BF16), w_gate_up[l, :, hidden:].astype(BF16),
            w_down[l].astype(BF16), seq, rows)
        stream = (hidden_out,)
    return stream[0]
```

```python
import functools
import math

import numpy as np
import jax
import jax.numpy as jnp
from jax import lax
from jax.experimental import pallas as pl
from jax.experimental.pallas import tpu as pltpu

F32 = jnp.float32
BF16 = jnp.bfloat16

GRID_W = 64
ROPE_BASE = 10000.0
NORM_EPS = 1e-6
A_HEADS, A_KV_HEADS, A_HD, A_WINDOW = 8, 2, 64, 128
B_HEADS, B_HD, B_CHUNK = 4, 128, 128
C_HEADS, C_Q_LORA, C_KV_LORA, C_NOPE, C_ROPE, C_VD = 4, 512, 256, 128, 64, 128
D_HEADS, D_HD, D_VD = 4, 64, 128
N_BRANCH = 4
BRANCH_W = 512
ROPE_DIM = 64
A_Q_W, A_KV_W = A_HEADS * A_HD, A_KV_HEADS * A_HD
B_W, B_GATES = B_HEADS * B_HD, 4 * B_HEADS
C_KN_W, C_V_W = C_HEADS * C_NOPE, C_HEADS * C_VD
D_QK_W, D_V_W = D_HEADS * 2 * D_HD, D_HEADS * D_VD
C_QK_PAD = 256

LANES = 128
V7X_VMEM_BYTES = 64 * 1024 * 1024
VMEM_LIMIT = V7X_VMEM_BYTES * 7 // 8
MASKED = -1e30

TOKEN_TILE = 256
WINDOW_TILE = 256
MLA_TQ = 1024
DIFF_TQ = 512
FLASH_SUB = 256
MLSTM_STEP_CHUNKS = 2
LOG2E = math.log2(math.e)


def _resident(shape):
    nd = len(shape)
    return pl.BlockSpec(shape, lambda *_: (0,) * nd, pipeline_mode=pl.Buffered(1))


def _mod_spec(d, ctx_tile):
    return pl.BlockSpec((1, 6, d), lambda bi, t: (bi * 2 + jnp.where(t == ctx_tile, 0, 1), 0, 0))


def _sigmoid(x):
    return 1.0 / (1.0 + jnp.exp(-x))


def _log_sigmoid(x):
    return jnp.minimum(x, 0.0) - jnp.log(1.0 + jnp.exp(-jnp.abs(x)))


def _rms(x, g):
    return x * lax.rsqrt(jnp.mean(x * x, axis=-1, keepdims=True) + NORM_EPS) * g


def _rope(x, cos, sin_lo, sin_hi):
    quarter = ROPE_DIM // 4
    return x * cos + pltpu.roll(x, LANES - quarter, 1) * sin_lo + pltpu.roll(x, quarter, 1) * sin_hi


def _dot_nt(a, b):
    return lax.dot_general(a, b, (((1,), (1,)), ((), ())), preferred_element_type=F32)


def _mod_kernel(c_ref, w_ref, b_ref, o_ref):
    c = c_ref[...]
    s = c * _sigmoid(c)
    o_ref[0] = jnp.dot(s, w_ref[0], preferred_element_type=F32,
                       precision=lax.Precision.HIGHEST) + b_ref[0]


def _modulation(rows, w_mod, b_mod):
    depth, d, n = w_mod.shape
    tn = d
    return pl.pallas_call(
        _mod_kernel,
        out_shape=jax.ShapeDtypeStruct((depth, rows.shape[0], n), F32),
        grid=(depth, n // tn),
        in_specs=[pl.BlockSpec(rows.shape, lambda l, j: (0, 0)),
                  pl.BlockSpec((1, d, tn), lambda l, j: (l, 0, j)),
                  pl.BlockSpec((1, 1, tn), lambda l, j: (l, 0, j))],
        out_specs=pl.BlockSpec((1, rows.shape[0], tn), lambda l, j: (l, 0, j)),
        name="modulation",
    )(rows, w_mod, b_mod.reshape(depth, 1, n))


def _stream_specs(stream, tp):
    d = stream[0].shape[2]
    if len(stream) == 1:
        return [pl.BlockSpec((1, tp, d), lambda bi, t: (bi, t, 0))]
    last = stream[0].shape[1] // tp - 1
    return [pl.BlockSpec((1, tp, d), lambda bi, t: (bi, jnp.minimum(t, last), 0)),
            pl.BlockSpec((1, tp, d), lambda bi, t: (bi, 0, 0))]


def _stream_tile(refs, lat_tiles):
    if len(refs) == 1:
        return refs[0][0]
    return jnp.where(pl.program_id(1) == lat_tiles, refs[1][0], refs[0][0])


def _proj_kernel(*refs, n_stream, lat_tiles):
    (mod_ref, gpre_ref, cos_ref, slo_ref, shi_ref,
     wa_ref, wb_ref, wgt_ref, wc_ref, wd_ref, wgate_ref,
     gbiast_ref, cgq_ref, cgkv_ref, wuq_ref, wukv_ref,
     aq_ref, ak_ref, av_ref, bq_ref, bk_ref, bv_ref, bo_ref, bgt_ref,
     cq_ref, ck_ref, cv_ref, dq_ref, dk_ref, dv_ref, gate_ref) = refs[n_stream:]
    x = _stream_tile(refs[:n_stream], lat_tiles)
    y = _rms(x, gpre_ref[...])
    u = (y * (1.0 + mod_ref[0, 1:2, :]) + mod_ref[0, 0:1, :]).astype(BF16)
    cos, slo, shi = cos_ref[...], slo_ref[...], shi_ref[...]

    def rope(v):
        return _rope(v, cos, slo, shi)

    def chunks(v, n):
        return [v[:, i * LANES:(i + 1) * LANES] for i in range(n)]

    pa = jnp.dot(u, wa_ref[...], preferred_element_type=F32)
    for i, v in enumerate(chunks(pa, A_Q_W // LANES)):
        aq_ref[0, :, i * LANES:(i + 1) * LANES] = (rope(v) * (A_HD ** -0.5 * LOG2E)).astype(BF16)
    ak_ref[0] = rope(pa[:, A_Q_W:A_Q_W + A_KV_W]).astype(BF16)
    av_ref[0] = pa[:, A_Q_W + A_KV_W:A_Q_W + 2 * A_KV_W].astype(BF16)

    pb = jnp.dot(u, wb_ref[...], preferred_element_type=F32)
    bq_ref[0] = pb[:, 0:B_W].astype(BF16)
    bv_ref[0] = pb[:, B_W:2 * B_W].astype(BF16)
    bo_ref[0] = _sigmoid(pb[:, 2 * B_W:3 * B_W]).astype(BF16)
    kg = _dot_nt(wgt_ref[...], u)
    bk_ref[0] = (kg[0:B_W] * (B_HD ** -0.5)).astype(BF16)
    bgt_ref[0] = kg[B_W:B_W + B_GATES] + gbiast_ref[...]

    pc = jnp.dot(u, wc_ref[...], preferred_element_type=F32)
    cqn = _rms(pc[:, 0:C_Q_LORA], cgq_ref[...]).astype(BF16)
    ckvn = _rms(pc[:, C_Q_LORA:C_Q_LORA + C_KV_LORA], cgkv_ref[...]).astype(BF16)
    kr = rope(pc[:, C_Q_LORA + C_KV_LORA:C_Q_LORA + C_KV_LORA + LANES]).astype(BF16)
    cq = jnp.dot(cqn, wuq_ref[...], preferred_element_type=F32)
    ckv = jnp.dot(ckvn, wukv_ref[...], preferred_element_type=F32)
    c_scale = (C_NOPE + C_ROPE) ** -0.5 * LOG2E
    for h in range(C_HEADS):
        o = h * C_QK_PAD
        cq_ref[0, :, o:o + LANES] = (cq[:, o:o + LANES] * c_scale).astype(BF16)
        cq_ref[0, :, o + LANES:o + 2 * LANES] = (rope(cq[:, o + LANES:o + 2 * LANES]) * c_scale).astype(BF16)
        ck_ref[0, :, o:o + LANES] = ckv[:, h * LANES:(h + 1) * LANES].astype(BF16)
        ck_ref[0, :, o + LANES:o + 2 * LANES] = kr
    cv_ref[0] = ckv[:, C_KN_W:C_KN_W + C_V_W].astype(BF16)

    pd = jnp.dot(u, wd_ref[...], preferred_element_type=F32)
    for i in range(D_QK_W // LANES):
        sl = slice(i * LANES, (i + 1) * LANES)
        dq_ref[0, :, sl] = (rope(pd[:, sl]) * (D_HD ** -0.5 * LOG2E)).astype(BF16)
        dk_ref[0, :, sl] = rope(pd[:, D_QK_W + i * LANES:D_QK_W + (i + 1) * LANES]).astype(BF16)
    dv_ref[0] = pd[:, 2 * D_QK_W:2 * D_QK_W + D_V_W].astype(BF16)

    for i in range(N_BRANCH):
        sl = slice(i * x.shape[1], (i + 1) * x.shape[1])
        gate_ref[0, :, sl] = _sigmoid(jnp.dot(u, wgate_ref[:, sl], preferred_element_type=F32)).astype(BF16)


def _project(stream, modall, gpre, tables, w):
    b, _, d = stream[0].shape
    tt = sum(s.shape[1] for s in stream)
    tp = TOKEN_TILE
    nt = tt // tp
    cos, slo, shi = tables

    def tok(width):
        return pl.BlockSpec((1, tp, width), lambda bi, t: (bi, t, 0))

    tab = pl.BlockSpec((tp, LANES), lambda bi, t: (t, 0))
    in_specs = _stream_specs(stream, tp) + [
        _mod_spec(d, nt - 1),
        _resident((1, d)), tab, tab, tab,
        _resident(w['wa'].shape), _resident(w['wb'].shape),
        _resident(w['wgt'].shape), _resident(w['wc'].shape), _resident(w['wd'].shape),
        _resident(w['wgate'].shape), _resident(w['gbiast'].shape),
        _resident(w['cgq'].shape), _resident(w['cgkv'].shape), _resident(w['wuq'].shape),
        _resident(w['wukv'].shape),
    ]
    widths = dict(aq=A_Q_W, ak=A_KV_W, av=A_KV_W, bq=B_W, bv=B_W, bo=B_W,
                  cq=C_HEADS * C_QK_PAD, ck=C_HEADS * C_QK_PAD, cv=C_V_W,
                  dq=D_QK_W, dk=D_QK_W, dv=D_V_W, gate=N_BRANCH * d)
    tokens_on_lanes = dict(bk=(B_W, BF16), bgt=(B_GATES, F32))
    names = ['aq', 'ak', 'av', 'bq', 'bk', 'bv', 'bo', 'bgt',
             'cq', 'ck', 'cv', 'dq', 'dk', 'dv', 'gate']
    out_shape, out_specs = [], []
    for n in names:
        if n in tokens_on_lanes:
            feat, dtype = tokens_on_lanes[n]
            out_shape.append(jax.ShapeDtypeStruct((b, feat, tt), dtype))
            out_specs.append(pl.BlockSpec((1, feat, tp), lambda bi, t: (bi, 0, t)))
        else:
            out_shape.append(jax.ShapeDtypeStruct((b, tt, widths[n]), BF16))
            out_specs.append(tok(widths[n]))
    outs = pl.pallas_call(
        functools.partial(_proj_kernel, n_stream=len(stream), lat_tiles=nt - 1),
        out_shape=out_shape,
        grid=(b, nt),
        in_specs=in_specs,
        out_specs=out_specs,
        compiler_params=pltpu.CompilerParams(
            dimension_semantics=("parallel", "parallel"), vmem_limit_bytes=VMEM_LIMIT),
        name="project",
    )(*stream, modall, gpre, cos, slo, shi, w['wa'], w['wb'], w['wgt'], w['wc'], w['wd'],
      w['wgate'], w['gbiast'], w['cgq'], w['cgkv'], w['wuq'], w['wukv'])
    return dict(zip(names, outs))


def _window_kernel(sink_ref, q_ref, kc_ref, kp_ref, kk_ref, kn_ref,
                   vc_ref, vp_ref, vk_ref, vn_ref, o_ref, *, lat_tiles):
    t = pl.program_id(1)
    hb = WINDOW_TILE // 2
    n_ctx = kc_ref.shape[1]
    group = A_HEADS // A_KV_HEADS
    width = n_ctx + 3 * hb
    kwin = jnp.concatenate([kp_ref[0], kk_ref[0], kn_ref[0]], axis=0)
    vwin = jnp.concatenate([vp_ref[0], vk_ref[0], vn_ref[0]], axis=0)
    prev_ok = jnp.logical_and(t >= 1, t < lat_tiles).astype(jnp.int32)
    cur_ok = (t < lat_tiles).astype(jnp.int32)
    next_ok = (t < lat_tiles - 1).astype(jnp.int32)
    row = lax.broadcasted_iota(jnp.int32, (hb, width), 0)
    col = lax.broadcasted_iota(jnp.int32, (hb, width), 1)
    wcol = col - n_ctx
    near = jnp.abs(wcol - hb - row) <= A_WINDOW
    low = lax.broadcasted_iota(jnp.int32, (hb, LANES), 1) < A_HD
    for blk in range(2):
        k0 = hb * (1 + blk)
        kb = jnp.concatenate([kc_ref[0], kwin[k0:k0 + 3 * hb]], axis=0)
        vb = jnp.concatenate([vc_ref[0], vwin[k0:k0 + 3 * hb]], axis=0)
        vb_ext = jnp.concatenate([vb, jnp.ones_like(vb)], axis=1)
        if blk == 0:
            seg_ok = jnp.where(wcol < hb, prev_ok, cur_ok)
        else:
            seg_ok = jnp.where(wcol < 2 * hb, cur_ok, next_ok)
        valid = jnp.logical_or(col < n_ctx, jnp.logical_and(near, seg_ok > 0))
        bias = jnp.tile(jnp.where(valid, 0.0, MASKED), (group, 1))
        rows = slice(blk * hb, (blk + 1) * hb)
        outs = []
        for g in range(A_KV_HEADS):
            keep = low if g == 0 else jnp.logical_not(low)
            qs, sinks = [], []
            for j in range(group):
                qb = q_ref[0, rows, j * LANES:(j + 1) * LANES]
                qs.append(jnp.where(keep, qb, jnp.zeros_like(qb)))
                sinks.append(jnp.full((hb, LANES), sink_ref[j + group * g] * LOG2E, F32))
            sink = jnp.concatenate(sinks, axis=0)
            s = _dot_nt(jnp.concatenate(qs, axis=0), kb) + bias
            m = jnp.maximum(jnp.max(s, axis=-1, keepdims=True), sink)
            e = jnp.exp2(s - jnp.tile(m, (1, width // LANES)))
            pv = jnp.dot(e.astype(BF16), vb_ext, preferred_element_type=F32)
            outs.append(pv[:, :LANES] / (pv[:, LANES:] + jnp.exp2(sink - m)))
        for j in range(group):
            o_ref[0, rows, j * LANES:(j + 1) * LANES] = jnp.where(
                low, outs[0][j * hb:(j + 1) * hb], outs[1][j * hb:(j + 1) * hb]).astype(BF16)


def _window_attention(q, k, v, sink, ctx_len):
    b, tt, _ = q.shape
    tw = WINDOW_TILE
    assert ctx_len == tw == 2 * A_WINDOW, "context is one key tile; a query block is one window radius"
    nt = tt // tw
    nl = nt - 1
    kv = lambda f: pl.BlockSpec((1, tw, LANES), lambda bi, t: (bi, f(t), 0))
    specs = [kv(lambda t: nl), kv(lambda t: jnp.maximum(t - 1, 0)), kv(lambda t: t),
             kv(lambda t: jnp.minimum(t + 1, nl - 1))]
    return pl.pallas_call(
        functools.partial(_window_kernel, lat_tiles=nl),
        out_shape=jax.ShapeDtypeStruct((b, tt, A_Q_W), BF16),
        grid=(b, nt),
        in_specs=[pl.BlockSpec(memory_space=pltpu.SMEM),
                  pl.BlockSpec((1, tw, A_Q_W), lambda bi, t: (bi, t, 0))] + specs + specs,
        out_specs=pl.BlockSpec((1, tw, A_Q_W), lambda bi, t: (bi, t, 0)),
        compiler_params=pltpu.CompilerParams(
            dimension_semantics=("parallel", "parallel"), vmem_limit_bytes=VMEM_LIMIT),
        name="window_attention",
    )(sink, q, k, k, k, k, v, v, v, v)


def _bf16_terms(x):
    hi = x.astype(BF16)
    r1 = x - hi.astype(F32)
    mid = r1.astype(BF16)
    return [hi, mid, (r1 - mid.astype(F32)).astype(BF16)]


def _block_diag(a, b):
    return jnp.concatenate([jnp.concatenate([a, jnp.zeros_like(b)], axis=1),
                            jnp.concatenate([jnp.zeros_like(a), b], axis=1)], axis=0)


def _mlstm_kernel(qf_ref, kf_ref, vf_ref, gtf_ref, qb_ref, kb_ref, vb_ref, gtb_ref,
                  hf_ref, hb_ref, c_ref, m_ref):
    step = pl.program_id(1)

    @pl.when(step == 0)
    def _():
        c_ref[...] = jnp.zeros_like(c_ref)
        m_ref[...] = jnp.zeros_like(m_ref)

    L = B_CHUNK
    row = lax.broadcasted_iota(jnp.int32, (L, L), 0)
    col = lax.broadcasted_iota(jnp.int32, (L, L), 1)
    dirs = ((qf_ref, kf_ref, vf_ref, gtf_ref, hf_ref), (qb_ref, kb_ref, vb_ref, gtb_ref, hb_ref))
    for d, (q_ref, kt_ref, v_ref, gt_ref, out_ref) in enumerate(dirs):
        before = (col <= row) if d == 0 else (col >= row)
        cum3 = jnp.tile(before.astype(BF16), (1, 3))
        cum_t3 = jnp.tile(((row <= col) if d == 0 else (row >= col)).astype(BF16), (3, 1))
        last = L - 1 if d == 0 else 0
        n_chunks = q_ref.shape[1] // L
        for c in (range(n_chunks) if d == 0 else reversed(range(n_chunks))):
            rows = slice(c * L, (c + 1) * L)
            gt = gt_ref[0, :, rows]
            lf_all = _log_sigmoid(gt)
            frow_all = jnp.dot(jnp.concatenate(_bf16_terms(lf_all), axis=1), cum_t3, preferred_element_type=F32)
            cf0 = (2 * d + 1) * B_HEADS
            lf_heads = jnp.concatenate([jnp.broadcast_to(lf_all[cf0 + h:cf0 + h + 1, :], (LANES, L))
                                        for h in range(B_HEADS)], axis=0)
            fcol_all = _dot_nt(cum3, jnp.concatenate(_bf16_terms(lf_heads), axis=1))
            for h0 in range(0, B_HEADS, 2):
                pair = (h0, h0 + 1)
                kts = [kt_ref[0, h * B_HD:(h + 1) * B_HD, rows] for h in pair]
                s_pair = jnp.dot(q_ref[0, rows, h0 * B_HD:(h0 + 2) * B_HD], _block_diag(*kts),
                                 preferred_element_type=F32)
                weighted, v_exts, decays, cts = [], [], [], []
                for j, h in enumerate(pair):
                    idx = d * B_HEADS + h
                    ci, cf = (2 * d) * B_HEADS + h, cf0 + h
                    i_row, f_row = gt[ci:ci + 1, :], frow_all[cf:cf + 1, :]
                    f_col = fcol_all[:, h * LANES:(h + 1) * LANES]
                    a_row = i_row - f_row
                    b_last = f_row[:, last:last + 1]
                    m_prev = m_ref[idx][:, 0:1]
                    q = q_ref[0, rows, h * B_HD:(h + 1) * B_HD]
                    v = v_ref[0, rows, h * B_HD:(h + 1) * B_HD]
                    v_ext = jnp.concatenate([v, jnp.ones_like(v)], axis=1)
                    dmat = jnp.where(before, f_col + a_row, -jnp.inf)
                    m_t = jnp.maximum(f_col + m_prev, jnp.max(dmat, axis=-1, keepdims=True))
                    wqk = jnp.exp(dmat - m_t) * s_pair[:, j * L:(j + 1) * L]
                    a = jnp.exp(f_col + m_prev - m_t)
                    ct = c_ref[idx]
                    num_den = jnp.dot(
                        jnp.concatenate([wqk.astype(BF16), (a * q.astype(F32)).astype(BF16)], axis=1),
                        jnp.concatenate([v_ext, ct.astype(BF16)], axis=0), preferred_element_type=F32)
                    hval = num_den[:, :B_HD] / jnp.maximum(jnp.abs(num_den[:, B_HD:]), jnp.exp(-m_t))
                    out_ref[0, rows, h * B_HD:(h + 1) * B_HD] = hval.astype(BF16)
                    g_row = b_last + a_row
                    m_new = jnp.maximum(b_last + m_prev, jnp.max(g_row, axis=-1, keepdims=True))
                    decays.append(jnp.exp(b_last + m_prev - m_new))
                    weighted.append((kts[j].astype(F32) * jnp.exp(g_row - m_new)).astype(BF16))
                    v_exts.append(v_ext)
                    cts.append(ct)
                    m_ref[idx] = jnp.broadcast_to(m_new, (1, LANES))
                update = jnp.dot(_block_diag(*weighted), jnp.concatenate(v_exts, axis=0),
                                 preferred_element_type=F32)
                for j, h in enumerate(pair):
                    c_ref[d * B_HEADS + h] = decays[j] * cts[j] + update[j * B_HD:(j + 1) * B_HD]


def _mlstm(q, k, v, gt, ctx_len):
    b, tt, w = q.shape
    L = MLSTM_STEP_CHUNKS * B_CHUNK
    assert ctx_len % L == 0 and tt % L == 0
    nc = tt // L
    cc = ctx_len // L

    nl = nc - cc

    def fwd(s):
        return jnp.where(s < cc, nl + s, s - cc)

    def bwd(s):
        return nc - 1 - s

    def specs(f):
        tok = pl.BlockSpec((1, L, w), lambda bi, s: (bi, f(s), 0))
        lanes = lambda feat: pl.BlockSpec((1, feat, L), lambda bi, s: (bi, 0, f(s)))
        return [tok, lanes(w), tok, lanes(B_GATES)]

    out = jax.ShapeDtypeStruct((b, tt, w), BF16)
    return pl.pallas_call(
        _mlstm_kernel,
        out_shape=[out, out],
        grid=(b, nc),
        in_specs=specs(fwd) + specs(bwd),
        out_specs=[pl.BlockSpec((1, L, w), lambda bi, s: (bi, fwd(s), 0)),
                   pl.BlockSpec((1, L, w), lambda bi, s: (bi, bwd(s), 0))],
        scratch_shapes=[pltpu.VMEM((2 * B_HEADS, B_HD, 2 * B_HD), F32),
                        pltpu.VMEM((2 * B_HEADS, 1, LANES), F32)],
        compiler_params=pltpu.CompilerParams(
            dimension_semantics=("parallel", "arbitrary"), vmem_limit_bytes=VMEM_LIMIT),
        name="mlstm",
    )(q, k, v, gt, q, k, v, gt)


def _softmax_update(q, k, v_ext, m_old, acc):
    s = _dot_nt(q, k)
    m_new = jnp.maximum(m_old, jnp.max(s, axis=-1, keepdims=True))
    p = jnp.exp2(s - jnp.tile(m_new, (1, s.shape[1] // LANES)))
    alpha = jnp.exp2(m_old - m_new)
    acc = jnp.tile(alpha, (1, 2)) * acc + jnp.dot(p.astype(BF16), v_ext, preferred_element_type=F32)
    return m_new, acc


def _attend(qs, vals, k_ref, v_ref):
    sub = min(FLASH_SUB, k_ref.shape[1])
    for c in range(k_ref.shape[1] // sub):
        k = k_ref[0, c * sub:(c + 1) * sub, :]
        v = v_ref[0, c * sub:(c + 1) * sub, :]
        v_ext = jnp.concatenate([v, jnp.ones_like(v)], axis=1)
        vals = [_softmax_update(q, k, v_ext, m, acc) for q, (m, acc) in zip(qs, vals)]
    return vals


def _normalized(acc):
    return acc[:, :LANES] / acc[:, LANES:]


def _mla_queries(q):
    return [q]


def _mla_finish(accs):
    return _normalized(accs[0])


def _diff_queries(q):
    low = lax.broadcasted_iota(jnp.int32, q.shape, 1) < D_HD
    zero = jnp.zeros_like(q)
    return [jnp.where(low, q, zero), jnp.where(low, zero, q)]


def _diff_finish(accs, lam_ref, gsub_ref, *, lam_init):
    lq1, lk1, lq2, lk2 = (lam_ref[r:r + 1, :] for r in range(4))
    lam = (jnp.exp(jnp.sum(lq1 * lk1, axis=-1, keepdims=True))
           - jnp.exp(jnp.sum(lq2 * lk2, axis=-1, keepdims=True)) + lam_init)
    o = _normalized(accs[0]) - lam * _normalized(accs[1])
    return _rms(o, gsub_ref[...]) * (1.0 - lam_init)


def _flash_kernel(*refs, n_extra, n_maps, queries, finish, latent):
    extra, refs = refs[:n_extra], refs[n_extra:]
    if latent:
        q_ref, kc_ref, vc_ref, k_ref, v_ref, o_ref = refs
    else:
        q_ref, kc_ref, vc_ref, o_ref = refs
    qs = queries(q_ref[0])
    rows = q_ref.shape[1]
    vals = [(jnp.full((rows, LANES), -jnp.inf, F32), jnp.zeros((rows, 2 * LANES), F32))] * n_maps
    vals = _attend(qs, vals, kc_ref, vc_ref)
    if latent:
        vals = _attend(qs, vals, k_ref, v_ref)
    o_ref[0] = finish([acc for _, acc in vals], *extra).astype(BF16)


def _full_attention(q, k, v, seq, query_tile, heads, qk_width, n_maps, queries, finish, extra, ctx_queries, name):
    b, tt, _ = q.shape
    ctx_len = tt - seq
    cb = seq // ctx_len
    tq = min(query_tile, seq)
    assert seq % tq == 0 and seq % ctx_len == 0 and seq % FLASH_SUB == 0

    body = functools.partial(_flash_kernel, n_extra=len(extra), n_maps=n_maps, queries=queries, finish=finish)
    whole3 = [pl.BlockSpec(e.shape, lambda bi, h, i: (0, 0)) for e in extra]
    y = pl.pallas_call(
        functools.partial(body, latent=True),
        out_shape=jax.ShapeDtypeStruct((b, seq, heads * LANES), BF16),
        grid=(b, heads, seq // tq),
        in_specs=whole3 + [
            pl.BlockSpec((1, tq, qk_width), lambda bi, h, i: (bi, i, h)),
            pl.BlockSpec((1, ctx_len, qk_width), lambda bi, h, i: (bi, cb, h)),
            pl.BlockSpec((1, ctx_len, LANES), lambda bi, h, i: (bi, cb, h)),
            pl.BlockSpec((1, seq, qk_width), lambda bi, h, i: (bi, 0, h)),
            pl.BlockSpec((1, seq, LANES), lambda bi, h, i: (bi, 0, h))],
        out_specs=pl.BlockSpec((1, tq, LANES), lambda bi, h, i: (bi, i, h)),
        compiler_params=pltpu.CompilerParams(
            dimension_semantics=("parallel", "parallel", "parallel"), vmem_limit_bytes=VMEM_LIMIT),
        name=name,
    )(*extra, q, k, v, k, v)
    if not ctx_queries:
        return (y,)
    whole2 = [pl.BlockSpec(e.shape, lambda bi, h: (0, 0)) for e in extra]
    y_ctx = pl.pallas_call(
        functools.partial(body, latent=False),
        out_shape=jax.ShapeDtypeStruct((b, ctx_len, heads * LANES), BF16),
        grid=(b, heads),
        in_specs=whole2 + [
            pl.BlockSpec((1, ctx_len, qk_width), lambda bi, h: (bi, cb, h)),
            pl.BlockSpec((1, ctx_len, qk_width), lambda bi, h: (bi, cb, h)),
            pl.BlockSpec((1, ctx_len, LANES), lambda bi, h: (bi, cb, h))],
        out_specs=pl.BlockSpec((1, ctx_len, LANES), lambda bi, h: (bi, 0, h)),
        compiler_params=pltpu.CompilerParams(
            dimension_semantics=("parallel", "parallel"), vmem_limit_bytes=VMEM_LIMIT),
        name=name + "_ctx",
    )(*extra, q, k, v)
    return (y, y_ctx)


def _mla_attention(q, k, v, seq, ctx_queries):
    return _full_attention(q, k, v, seq, MLA_TQ, C_HEADS, C_QK_PAD, 1, _mla_queries, _mla_finish, [],
                           ctx_queries, "mla_attention")


def _diff_attention(q, k, v, lam_vecs, gsub, lam_init, seq, ctx_queries):
    return _full_attention(q, k, v, seq, DIFF_TQ, D_HEADS, 2 * D_HD, 2, _diff_queries,
                           functools.partial(_diff_finish, lam_init=lam_init), [lam_vecs, gsub],
                           ctx_queries, "diff_attention")


def _merge_ffn_kernel(*refs, n_stream, n_split, lat_tiles):
    yc_refs = refs[n_stream:n_stream + n_split]
    yd_refs = refs[n_stream + n_split:n_stream + 2 * n_split]
    (ya_ref, hf_ref, hb_ref, bo_ref, gate_ref, wbr_ref, wout_ref,
     mod_ref, gpost_ref, fpre_ref, fpost_ref, wg_ref, wu_ref, wd_ref, o_ref) = refs[n_stream + 2 * n_split:]
    d = o_ref.shape[2]
    yb = (bo_ref[0].astype(F32) * (hf_ref[0].astype(F32) + hb_ref[0].astype(F32))).astype(BF16)
    ys = (ya_ref[0], yb, _stream_tile(yc_refs, lat_tiles), _stream_tile(yd_refs, lat_tiles))
    merged = None
    for i, y in enumerate(ys):
        g = gate_ref[0, :, i * d:(i + 1) * d].astype(F32)
        term = g * jnp.dot(y, wbr_ref[i], preferred_element_type=F32)
        merged = term if merged is None else merged + term
    out = jnp.dot(merged.astype(BF16), wout_ref[0], preferred_element_type=F32)
    x = _stream_tile(refs[:n_stream], lat_tiles) + mod_ref[0, 2:3, :] * _rms(out, gpost_ref[...])
    u = (_rms(x, fpre_ref[...]) * (1.0 + mod_ref[0, 4:5, :]) + mod_ref[0, 3:4, :]).astype(BF16)
    gate = jnp.dot(u, wg_ref[0], preferred_element_type=F32)
    up = jnp.dot(u, wu_ref[0], preferred_element_type=F32)
    act = (gate * _sigmoid(gate) * up).astype(BF16)
    out = jnp.dot(act, wd_ref[0], preferred_element_type=F32)
    o_ref[0] = x + mod_ref[0, 5:6, :] * _rms(out, fpost_ref[...])


def _layer_block(shape, layer, col=0):
    return pl.BlockSpec((1,) + tuple(shape), lambda *_: (layer, 0, col), pipeline_mode=pl.Buffered(1))


def _merge_ffn(p, ya, hf, hb, yc, yd, wbr, wout, stream, modall, gpost, fpre, fpost, wgu, wd, layer, seq, rows):
    b, _, d = stream[0].shape
    hidden = wd.shape[1]
    tp = TOKEN_TILE
    assert len(yc) == len(yd) and (len(yc) == 2 or rows == seq), "context rows need the mixers' context outputs"
    assert hidden % LANES == 0 and wgu.shape[2] == 2 * hidden
    tok = lambda width: pl.BlockSpec((1, tp, width), lambda bi, t: (bi, t, 0))
    vec = _resident((1, d))
    return pl.pallas_call(
        functools.partial(_merge_ffn_kernel, n_stream=len(stream), n_split=len(yc), lat_tiles=seq // tp),
        out_shape=jax.ShapeDtypeStruct((b, rows, d), F32),
        grid=(b, rows // tp),
        in_specs=_stream_specs(stream, tp) + _stream_specs(yc, tp) + _stream_specs(yd, tp) + [tok(BRANCH_W)] * 4 + [
            tok(N_BRANCH * d), _resident(wbr.shape), _layer_block((d, d), layer),
            _mod_spec(d, seq // tp), vec, vec, vec,
            _layer_block((d, hidden), layer, 0), _layer_block((d, hidden), layer, 1),
            _layer_block((hidden, d), layer)],
        out_specs=tok(d),
        compiler_params=pltpu.CompilerParams(
            dimension_semantics=("parallel", "parallel"), vmem_limit_bytes=VMEM_LIMIT),
        name="merge_ffn",
    )(*stream, *yc, *yd, ya, hf, hb, p['bo'], p['gate'], wbr, wout, modall, gpost, fpre, fpost, wgu, wgu, wd)


def _rope_tables(n_rows, ctx_len):
    row = jnp.repeat(jnp.arange(n_rows, dtype=F32), GRID_W)
    col = jnp.tile(jnp.arange(GRID_W, dtype=F32), n_rows)
    quarter = ROPE_DIM // 4
    inv_freq = ROPE_BASE ** (-jnp.arange(quarter, dtype=F32) / quarter)
    ang_r, ang_c = row[:, None] * inv_freq, col[:, None] * inv_freq
    ang = jnp.concatenate([ang_r, ang_r, ang_c, ang_c], axis=-1)
    cos, sin = jnp.cos(ang), jnp.sin(ang)
    cos = jnp.concatenate([cos, jnp.ones((ctx_len, ROPE_DIM), F32)], axis=0)
    sin = jnp.concatenate([sin, jnp.zeros((ctx_len, ROPE_DIM), F32)], axis=0)
    cos, sin = jnp.tile(cos, (1, LANES // ROPE_DIM)), jnp.tile(sin, (1, LANES // ROPE_DIM))
    first = (jnp.arange(LANES) % (ROPE_DIM // 2)) < quarter
    return cos, jnp.where(first, -sin, 0.0), jnp.where(first, 0.0, sin)


def _a_head_perm():
    idx = []
    for j in range(A_HEADS // 2):
        for half in range(2):
            head = j + (A_HEADS // 2) * half
            idx.extend(range(head * A_HD, (head + 1) * A_HD))
    return np.asarray(idx)


def _prep_layer(w_in, b_gate_bias, c_g_q, c_g_kv, c_w_uq, c_w_uk, c_w_uv, w_branch):
    d = w_in.shape[0]
    splits = (A_Q_W, A_KV_W, A_KV_W, B_W, B_W, B_W, B_GATES, B_W, C_Q_LORA, C_KV_LORA, C_ROPE,
              D_QK_W, D_QK_W, D_V_W, N_BRANCH * d)
    offs = np.concatenate([[0], np.cumsum(splits)])
    part = [w_in[:, offs[i]:offs[i + 1]] for i in range(len(splits))]
    (a_q, a_k, a_v, b_q, b_k, b_v, b_g, b_o, c_q, c_kv, c_kr, d_q, d_k, d_v, gate) = part
    perm = _a_head_perm()
    zeros = lambda n: jnp.zeros((d, n), w_in.dtype)
    uq = c_w_uq.reshape(C_Q_LORA, C_HEADS, C_NOPE + C_ROPE)
    uq = jnp.pad(uq, ((0, 0), (0, 0), (0, C_QK_PAD - C_NOPE - C_ROPE))).reshape(C_Q_LORA, C_HEADS * C_QK_PAD)
    w = {
        'wa': jnp.concatenate([a_q[:, perm], a_k, a_v], axis=1).astype(BF16),
        'wb': jnp.concatenate([b_q, b_v, b_o], axis=1).astype(BF16),
        'wgt': jnp.concatenate([b_k, b_g], axis=1).T.astype(BF16),
        'wc': jnp.concatenate([c_q, c_kv, c_kr, zeros(LANES - C_ROPE)], axis=1).astype(BF16),
        'wd': jnp.concatenate([d_q, d_k, d_v], axis=1).astype(BF16),
        'wgate': gate.astype(BF16),
        'gbiast': b_gate_bias.reshape(B_GATES, 1),
        'cgq': c_g_q.reshape(1, -1),
        'cgkv': c_g_kv.reshape(1, -1),
        'wuq': uq.astype(BF16),
        'wukv': jnp.concatenate([c_w_uk, c_w_uv], axis=1).astype(BF16),
    }
    wbr = jnp.concatenate([w_branch[0:1, perm], w_branch[1:]], axis=0).astype(BF16)
    return w, wbr


def kernel(x, c, ctx, c_ctx, w_mod, b_mod, g_mix_pre, g_mix_post, g_ffn_pre, g_ffn_post, w_in, a_sink,
           b_gate_bias, c_g_q, c_g_kv, c_w_uq, c_w_uk, c_w_uv, d_lam_q1, d_lam_k1, d_lam_q2, d_lam_k2,
           d_g_sub, w_branch, w_out, w_gate_up, w_down):
    b, seq, d = x.shape
    ctx_len = ctx.shape[1]
    depth = w_in.shape[0]
    tables = _rope_tables(seq // GRID_W, ctx_len)

    rows = jnp.concatenate([c, c_ctx[None, :], jnp.zeros((8 - b - 1, d), F32)], axis=0)
    mods = _modulation(rows, w_mod, b_mod).reshape(depth, 8, 6, d)

    assert ctx_len == TOKEN_TILE == WINDOW_TILE, "the context must be exactly one token tile"
    w_out_bf, w_gate_up_bf, w_down_bf = w_out.astype(BF16), w_gate_up.astype(BF16), w_down.astype(BF16)
    stream = (x, ctx)
    for l in range(depth):
        with_ctx = l < depth - 1
        modall = jnp.stack([jnp.broadcast_to(mods[l, b], (b, 6, d)), mods[l, :b]], axis=1).reshape(2 * b, 6, d)
        w, wbr = _prep_layer(w_in[l], b_gate_bias[l], c_g_q[l], c_g_kv[l], c_w_uq[l], c_w_uk[l],
                             c_w_uv[l], w_branch[l])
        lam_init = 0.8 - 0.6 * math.exp(-0.3 * l)
        lam_vecs = jnp.stack([d_lam_q1[l], d_lam_k1[l], d_lam_q2[l], d_lam_k2[l]], axis=0)

        p = _project(stream, modall, g_mix_pre[l].reshape(1, d), tables, w)
        ya = _window_attention(p['aq'], p['ak'], p['av'], a_sink[l], ctx_len)
        hf, hb = _mlstm(p['bq'], p['bk'], p['bv'], p['bgt'], ctx_len)
        yc = _mla_attention(p['cq'], p['ck'], p['cv'], seq, with_ctx)
        yd = _diff_attention(p['dq'], p['dk'], p['dv'], lam_vecs, d_g_sub[l].reshape(1, -1), lam_init,
                             seq, with_ctx)
        rows = seq + ctx_len if with_ctx else seq
        hidden_out = _merge_ffn(
            p, ya, hf, hb, yc, yd, wbr, w_out_bf, stream, modall,
            g_mix_post[l].reshape(1, d), g_ffn_pre[l].reshape(1, d), g_ffn_post[l].reshape(1, d),
            w_gate_up_bf, w_down_bf, l, seq, rows)
        stream = (hidden_out,)
    return stream[0]
```
